```python
import math
import jax, jax.numpy as jnp
from jax import lax
import numpy as np

D_MODEL = 2048
BATCH = 4
SEQ = 2048
DEPTH = 2
DEC_BATCH = 128
DEC_SEQ = 8
PAST_LEN = 2048
PAGE_SIZE = 128

N_EVEN = (DEPTH + 1) // 2
N_ODD = DEPTH // 2
D_HALF = D_MODEL // 2
CONV_W = 31
CONV_C = D_HALF
FOX_HEADS = 8
FOX_HD = D_HALF // FOX_HEADS
FOX_SCALE = FOX_HD ** -0.5
Q_BLOCK = 128
HG_HEADS = 8
HG_DK = D_HALF // HG_HEADS
HG_DV = D_HALF // HG_HEADS
HG_CHUNK = 64
MLP_GROUPS = 8
MLP_GD = D_HALF // MLP_GROUPS
MLP_CHUNK = 128
N_EXPERTS = 16
N_GROUPS = 4
EXPERTS_PER_GROUP = N_EXPERTS // N_GROUPS
TOP_K = 2
D_EXPERT = D_MODEL // 2
DN_ALPHA = (2 * DEPTH) ** 0.25
DN_BETA = (8 * DEPTH) ** -0.25
LN_EPS = 1e-5
EVEN_IN = 2 * CONV_C + 3 * D_HALF + FOX_HEADS
EVEN_SPLITS = (CONV_C, 2 * CONV_C, 2 * CONV_C + D_HALF, 2 * CONV_C + 2 * D_HALF, 2 * CONV_C + 3 * D_HALF)
ODD_IN = 6 * D_HALF

kernel_name = "hybrid_conv_fox_hgrn2_gmlp_moe_step"


def layer_norm(x, g, b):
    xf = x.astype(jnp.float32)
    mu = jnp.mean(xf, axis=-1, keepdims=True)
    var = jnp.mean(jnp.square(xf - mu), axis=-1, keepdims=True)
    return ((xf - mu) * lax.rsqrt(var + LN_EPS) * g.astype(jnp.float32) + b.astype(jnp.float32)).astype(x.dtype)


def head_rms_norm(x, g):
    ms = jnp.mean(jnp.square(x), axis=-1, keepdims=True)
    return x * lax.rsqrt(ms + LN_EPS) * g.astype(jnp.float32)


def conformer_conv(a, gate, conv_buf, w_dw, b_dw, g_n, b_n):
    u = a * jax.nn.sigmoid(gate)
    ext = jnp.concatenate([conv_buf.astype(u.dtype), u], axis=1)
    y = lax.conv_general_dilated(ext, w_dw.astype(u.dtype)[:, None, :], window_strides=(1,), padding="VALID",
                                 dimension_numbers=("NWC", "WIO", "NWC"), feature_group_count=CONV_C)
    y = jax.nn.silu(layer_norm(y + b_dw.astype(y.dtype), g_n, b_n))
    return y, ext[:, -(CONV_W - 1):]


def fox_attend(q, k, v, c_q, c_k, pos_q, pos_k):
    s = jnp.einsum("bqhd,bkhd->bhqk", q, k, preferred_element_type=jnp.float32) * FOX_SCALE
    s = s + jnp.swapaxes(c_q, 1, 2)[:, :, :, None] - jnp.swapaxes(c_k, 1, 2)[:, :, None, :]
    s = jnp.where(pos_q[:, None] >= pos_k[None, :], s, -jnp.inf)
    p = jax.nn.softmax(s, axis=-1)
    return jnp.einsum("bhqk,bkhd->bqhd", p.astype(v.dtype), v)


def fox_prompt(q, k, v, logf):
    B, L, H, Dh = q.shape
    c = jnp.cumsum(logf, axis=1)
    nb = L // Q_BLOCK
    pos_k = jnp.arange(L)
    qb = jnp.moveaxis(q.reshape(B, nb, Q_BLOCK, H, Dh), 1, 0)
    cb = jnp.moveaxis(c.reshape(B, nb, Q_BLOCK, H), 1, 0)

    def block(args):
        q_i, c_i, i = args
        return fox_attend(q_i, k, v, c_i, c, i * Q_BLOCK + jnp.arange(Q_BLOCK), pos_k)

    o = lax.map(block, (qb, cb, jnp.arange(nb)))
    return jnp.moveaxis(o, 0, 1).reshape(B, L, H, Dh)


def hgrn2_scan(q, logf, kk, i, s0):
    B, L, H, _ = q.shape
    C = min(HG_CHUNK, L)
    n = -(-L // C)
    pad = n * C - L

    def to_chunks(t):
        t = jnp.pad(t, ((0, 0), (0, pad), (0, 0), (0, 0)))
        return jnp.swapaxes(t.reshape(B, n, C, H, t.shape[-1]), 0, 1)

    tri = jnp.tril(jnp.ones((C, C), dtype=bool))[None, :, :, None, None]

    def step(S, xs):
        qc, lfc, kc, ic = xs
        b = jnp.cumsum(lfc, axis=1)
        rel = b[:, :, None] - b[:, None, :]
        dec = jnp.exp(jnp.where(tri, rel, -jnp.inf))
        att = jnp.einsum("bthk,btshk,bshk->bhts", qc, dec, kc)
        intra = jnp.einsum("bhts,bshv->bthv", att, ic)
        inter = jnp.einsum("bthk,bhkv->bthv", qc * jnp.exp(b), S)
        b_last = b[:, -1]
        S_new = jnp.exp(b_last)[..., None] * S + jnp.einsum("bshk,bshv->bhkv", kc * jnp.exp(b_last[:, None] - b), ic)
        return S_new, intra + inter

    S, o = lax.scan(step, s0, (to_chunks(q), to_chunks(logf), to_chunks(kk), to_chunks(i)))
    o = jnp.swapaxes(o, 0, 1).reshape(B, n * C, H, HG_DV)[:, :L]
    return o, S


def chunk_token_mlp(u, v, w_s, b_s):
    B, L, G, Dg = v.shape
    cs = min(MLP_CHUNK, L)
    n = -(-L // cs)
    pad = n * cs - L
    vp = jnp.pad(v, ((0, 0), (0, pad), (0, 0), (0, 0))).reshape(B, n, cs, G, Dg)
    w = jnp.where(jnp.tril(jnp.ones((cs, cs), dtype=bool))[None], w_s[:, :cs, :cs], 0.0)
    mixed = jnp.einsum("gts,bnsgd->bntgd", w.astype(vp.dtype), vp) + b_s[:, :cs].T.astype(vp.dtype)[:, :, None]
    mixed = mixed.reshape(B, n * cs, G, Dg)[:, :L]
    return u * mixed


def moe(x, w_router, b_router, w_gate, w_up, w_down):
    B, L, D = x.shape
    T = B * L
    xt = x.reshape(T, D)
    probs = jax.nn.softmax(jnp.dot(xt, w_router, preferred_element_type=jnp.float32), axis=-1)
    sel = probs + b_router.astype(jnp.float32)
    group_score = jnp.sum(lax.top_k(sel.reshape(T, N_GROUPS, EXPERTS_PER_GROUP), TOP_K)[0], axis=-1)
    group_mask = jnp.argmax(group_score, axis=-1)[:, None] == jnp.arange(N_GROUPS)[None, :]
    sel_in_group = jnp.where(jnp.repeat(group_mask, EXPERTS_PER_GROUP, axis=-1), sel, -jnp.inf)
    _, idx = lax.top_k(sel_in_group, TOP_K)
    gp = jnp.take_along_axis(probs, idx, axis=-1)
    gp = gp / jnp.sum(gp, axis=-1, keepdims=True)
    gates = jnp.sum(jax.nn.one_hot(idx, N_EXPERTS, dtype=jnp.float32) * gp[..., None], axis=1)
    y = jnp.zeros((T, D), jnp.float32)
    for e in range(N_EXPERTS):
        h = jax.nn.silu(xt @ w_gate[e]) * (xt @ w_up[e])
        y = y + gates[:, e:e + 1] * jnp.dot(h, w_down[e], preferred_element_type=jnp.float32)
    return y.astype(x.dtype).reshape(B, L, D)


def even_project(x, w_in, b_fg):
    B, L, _ = x.shape
    a, gt, q, k, v, fz = jnp.split(jnp.einsum("bld,de->ble", x, w_in), EVEN_SPLITS, axis=-1)
    hs = (B, L, FOX_HEADS, FOX_HD)
    logf = jax.nn.log_sigmoid(fz.astype(jnp.float32) + b_fg.astype(jnp.float32))
    return a, gt, q.reshape(hs), k.reshape(hs), v.reshape(hs), logf


def even_merge(conv_out, att, w_out):
    B, L = conv_out.shape[:2]
    h = jnp.concatenate([conv_out, att.reshape(B, L, D_HALF).astype(conv_out.dtype)], axis=-1)
    return jnp.einsum("ble,ed->bld", h, w_out)


def odd_mixer(x, s0, w_in, lb, g_o, g_v, b_v, w_s, b_s, w_out):
    f32 = jnp.float32
    B, L, _ = x.shape
    qz, fz, iz, gz, uz, vz = jnp.split(jnp.einsum("bld,de->ble", x, w_in), 6, axis=-1)
    hs = (B, L, HG_HEADS, HG_DK)
    q = jax.nn.silu(qz.astype(f32)).reshape(hs)
    lbh = lb.reshape(HG_HEADS, HG_DK)
    f = lbh + (1.0 - lbh) * jax.nn.sigmoid(fz.astype(f32).reshape(hs))
    o, s_new = hgrn2_scan(q, jnp.log(f), 1.0 - f, iz.astype(f32).reshape(B, L, HG_HEADS, HG_DV), s0)
    o = head_rms_norm(o, g_o.reshape(HG_HEADS, HG_DV)).reshape(B, L, D_HALF) * jax.nn.silu(gz.astype(f32))
    u = jax.nn.gelu(uz).reshape(B, L, MLP_GROUPS, MLP_GD)
    v = layer_norm(jax.nn.gelu(vz), g_v, b_v)
    d = chunk_token_mlp(u, v.reshape(B, L, MLP_GROUPS, MLP_GD), w_s, b_s).reshape(B, L, D_HALF)
    h = jnp.concatenate([o.astype(x.dtype), d.astype(x.dtype)], axis=-1)
    return jnp.einsum("ble,ed->bld", h, w_out), s_new, v


def setup_inputs(seed: int = 0) -> dict:
    key = jax.random.key(seed)
    ks = iter(jax.random.split(key, 48))

    def nrm(shape, scale):
        return scale * jax.random.normal(next(ks), shape, jnp.float32)

    n_pages = PAST_LEN // PAGE_SIZE
    n_used = DEC_BATCH * n_pages
    n_phys = n_used + max(1, n_used // 4)
    x_prompt = nrm((BATCH, SEQ, D_MODEL), 1.0)
    x_sample = nrm((DEC_BATCH, DEC_SEQ, D_MODEL), 1.0)
    cache_k = nrm((N_EVEN, n_phys, PAGE_SIZE, FOX_HEADS, FOX_HD), 1.0)
    cache_v = nrm((N_EVEN, n_phys, PAGE_SIZE, FOX_HEADS, FOX_HD), 1.0)
    cache_logf = jax.nn.log_sigmoid(3.0 + nrm((N_EVEN, n_phys, PAGE_SIZE, FOX_HEADS), 1.0))
    state_conv = nrm((N_EVEN, DEC_BATCH, CONV_W - 1, CONV_C), 0.5)
    state_hgrn = nrm((N_ODD, DEC_BATCH, HG_HEADS, HG_DK, HG_DV), 0.5)
    page_table = jax.random.permutation(next(ks), n_phys)[:n_used].reshape(DEC_BATCH, n_pages).astype(jnp.int32)
    return {
        "x_prompt": x_prompt,
        "x_sample": x_sample,
        "cache_k": cache_k,
        "cache_v": cache_v,
        "cache_logf": cache_logf,
        "state_conv": state_conv,
        "state_hgrn": state_hgrn,
        "page_table": page_table,
        "w_in_even": nrm((N_EVEN, D_MODEL, EVEN_IN), D_MODEL ** -0.5),
        "b_fgate": 3.0 + nrm((N_EVEN, FOX_HEADS), 0.5),
        "w_dw": nrm((N_EVEN, CONV_W, CONV_C), CONV_W ** -0.5),
        "b_dw": nrm((N_EVEN, CONV_C), 0.02),
        "g_cnorm": 1.0 + nrm((N_EVEN, CONV_C), 0.02),
        "b_cnorm": nrm((N_EVEN, CONV_C), 0.02),
        "w_out_even": nrm((N_EVEN, D_MODEL, D_MODEL), DN_BETA * D_MODEL ** -0.5),
        "w_in_odd": nrm((N_ODD, D_MODEL, ODD_IN), D_MODEL ** -0.5),
        "lb_logits": nrm((DEPTH, D_HALF), 0.5),
        "g_onorm": 1.0 + nrm((N_ODD, D_HALF), 0.02),
        "g_vnorm": 1.0 + nrm((N_ODD, D_HALF), 0.02),
        "b_vnorm": nrm((N_ODD, D_HALF), 0.02),
        "w_sgu": nrm((N_ODD, MLP_GROUPS, MLP_CHUNK, MLP_CHUNK), 0.5 * MLP_CHUNK ** -0.5),
        "b_sgu": 1.0 + nrm((N_ODD, MLP_GROUPS, MLP_CHUNK), 0.02),
        "w_out_odd": nrm((N_ODD, D_MODEL, D_MODEL), DN_BETA * D_MODEL ** -0.5),
        "ln1_g": 1.0 + nrm((DEPTH, D_MODEL), 0.02),
        "ln1_b": nrm((DEPTH, D_MODEL), 0.02),
        "ln2_g": 1.0 + nrm((DEPTH, D_MODEL), 0.02),
        "ln2_b": nrm((DEPTH, D_MODEL), 0.02),
        "w_router": nrm((D_MODEL, N_EXPERTS), D_MODEL ** -0.5),
        "b_router": nrm((N_EXPERTS,), 0.01),
        "w_gate": nrm((DEPTH, N_EXPERTS, D_MODEL, D_EXPERT), D_MODEL ** -0.5),
        "w_up": nrm((DEPTH, N_EXPERTS, D_MODEL, D_EXPERT), D_MODEL ** -0.5),
        "w_down": nrm((DEPTH, N_EXPERTS, D_EXPERT, D_MODEL), DN_BETA * D_EXPERT ** -0.5),
    }


def reference(x_prompt, x_sample, cache_k, cache_v, cache_logf, state_conv, state_hgrn, page_table,
              w_in_even, b_fgate, w_dw, b_dw, g_cnorm, b_cnorm, w_out_even,
              w_in_odd, lb_logits, g_onorm, g_vnorm, b_vnorm, w_sgu, b_sgu, w_out_odd,
              ln1_g, ln1_b, ln2_g, ln2_b, w_router, b_router, w_gate, w_up, w_down):
    f32 = jnp.float32
    Bp = x_prompt.shape[0]
    Bs, Ls, _ = x_sample.shape
    past = page_table.shape[1] * PAGE_SIZE
    lb_p = jax.nn.softmax(lb_logits.astype(f32), axis=0)
    lb_all = jnp.cumsum(lb_p, axis=0) - lb_p[0]

    xp, xs = x_prompt, x_sample
    kp_rows, vp_rows, lfp_rows, convp_rows, hgp_rows = [], [], [], [], []
    ks_rows, vs_rows, lfs_rows, convs_rows, hgs_rows, mlpv_rows = [], [], [], [], [], []
    for l in range(DEPTH):
        j = l // 2
        if l % 2 == 0:
            conv_args = (w_dw[j], b_dw[j], g_cnorm[j], b_cnorm[j])
            a, gt, q, k, v, lf = even_project(xp, w_in_even[j], b_fgate[j])
            ca, cst = conformer_conv(a, gt, jnp.zeros((Bp, CONV_W - 1, CONV_C), a.dtype), *conv_args)
            att = fox_prompt(q, k, v, lf)
            mix_p = even_merge(ca, att, w_out_even[j])
            kp_rows.append(k)
            vp_rows.append(v)
            lfp_rows.append(lf)
            convp_rows.append(cst)
            a, gt, q, k, v, lf = even_project(xs, w_in_even[j], b_fgate[j])
            ca, cst = conformer_conv(a, gt, state_conv[j], *conv_args)
            k_past = cache_k[j, page_table].reshape(Bs, past, FOX_HEADS, FOX_HD)
            v_past = cache_v[j, page_table].reshape(Bs, past, FOX_HEADS, FOX_HD)
            lf_past = cache_logf[j, page_table].reshape(Bs, past, FOX_HEADS).astype(f32)
            k_all = jnp.concatenate([k_past.astype(k.dtype), k], axis=1)
            v_all = jnp.concatenate([v_past.astype(v.dtype), v], axis=1)
            c = jnp.cumsum(jnp.concatenate([lf_past, lf], axis=1), axis=1)
            att = fox_attend(q, k_all, v_all, c[:, past:], c, past + jnp.arange(Ls), jnp.arange(past + Ls))
            mix_s = even_merge(ca, att, w_out_even[j])
            ks_rows.append(k)
            vs_rows.append(v)
            lfs_rows.append(lf)
            convs_rows.append(cst)
        else:
            odd_args = (w_in_odd[j], lb_all[l], g_onorm[j], g_vnorm[j], b_vnorm[j], w_sgu[j], b_sgu[j], w_out_odd[j])
            mix_p, s_p, _ = odd_mixer(xp, jnp.zeros((Bp, HG_HEADS, HG_DK, HG_DV), f32), *odd_args)
            mix_s, s_s, v_s = odd_mixer(xs, state_hgrn[j].astype(f32), *odd_args)
            hgp_rows.append(s_p)
            hgs_rows.append(s_s)
            mlpv_rows.append(v_s)
        xp = layer_norm(DN_ALPHA * xp + mix_p, ln1_g[l], ln1_b[l])
        xs = layer_norm(DN_ALPHA * xs + mix_s, ln1_g[l], ln1_b[l])
        xp = layer_norm(DN_ALPHA * xp + moe(xp, w_router, b_router, w_gate[l], w_up[l], w_down[l]), ln2_g[l], ln2_b[l])
        xs = layer_norm(DN_ALPHA * xs + moe(xs, w_router, b_router, w_gate[l], w_up[l], w_down[l]), ln2_g[l], ln2_b[l])

    return (xp, xs,
            jnp.stack(kp_rows), jnp.stack(vp_rows), jnp.stack(lfp_rows), jnp.stack(convp_rows), jnp.stack(hgp_rows),
            jnp.stack(ks_rows), jnp.stack(vs_rows), jnp.stack(lfs_rows), jnp.stack(convs_rows), jnp.stack(hgs_rows),
            jnp.stack(mlpv_rows))
```

```python
import functools

import jax
import jax.numpy as jnp
from jax import lax
from jax.experimental import pallas as pl
from jax.experimental.pallas import tpu as pltpu

F32 = jnp.float32
BF16 = jnp.bfloat16
HIGHEST = lax.Precision.HIGHEST

LANE = 128
VMEM_LIMIT = 56 * 1024 * 1024
LN_EPS = 1e-5
HG_CHUNK = 64
HG_SUB = 16
CONV_HALO = 32
N_GROUPS = 4
NT = (((1,), (1,)), ((), ()))
TN = (((0,), (0,)), ((), ()))


def _params(*sem):
    return pltpu.CompilerParams(dimension_semantics=sem, vmem_limit_bytes=VMEM_LIMIT)


def _tile(n, prefs):
    for t in prefs:
        if n % t == 0:
            return t
    raise ValueError(f"no tile in {prefs} divides {n}")


def _sigmoid(x):
    return 1.0 / (1.0 + jnp.exp(-x))


def _silu(x):
    return x * _sigmoid(x)


def _gelu(x):
    return 0.5 * x * (1.0 + jnp.tanh(0.7978845608028654 * (x + 0.044715 * (x * x * x))))


def _log_sigmoid(x):
    return -(jnp.maximum(-x, 0.0) + jnp.log1p(jnp.exp(-jnp.abs(x))))


def _layer_norm(x, g, b):
    mu = jnp.mean(x, axis=-1, keepdims=True)
    xc = x - mu
    var = jnp.mean(xc * xc, axis=-1, keepdims=True)
    return xc * lax.rsqrt(var + LN_EPS) * g + b


def _iota(shape, dim):
    return lax.broadcasted_iota(jnp.int32, shape, dim)


def _matmul_kernel(x_ref, w_ref, o_ref, wbf_ref):
    @pl.when(pl.program_id(1) == 0)
    def _():
        wbf_ref[...] = w_ref[...].astype(BF16)

    o_ref[...] = jnp.dot(x_ref[...], wbf_ref[...], preferred_element_type=F32).astype(o_ref.dtype)


def matmul_cols(x_bf, w, col_start, n_cols, out_dtype=F32):
    M, K = x_bf.shape
    tm = _tile(M, (1024, 512, 256, 128))
    tn = _tile(n_cols, (512, 256, 128))
    assert col_start % tn == 0
    off = col_start // tn
    return pl.pallas_call(
        _matmul_kernel,
        grid=(n_cols // tn, M // tm),
        in_specs=[pl.BlockSpec((tm, K), lambda j, i: (i, 0)),
                  pl.BlockSpec((K, tn), lambda j, i: (0, j + off))],
        out_specs=pl.BlockSpec((tm, tn), lambda j, i: (i, j)),
        out_shape=jax.ShapeDtypeStruct((M, n_cols), out_dtype),
        scratch_shapes=[pltpu.VMEM((K, tn), BF16)],
        compiler_params=_params("arbitrary", "arbitrary"),
        name="matmul_cols",
    )(x_bf, w)


def _logf_kernel(x_ref, w_ref, wt_ref, b_ref, bt_ref, lf_ref, lft_ref, c_ref, ct_ref,
                 carry_ref, carryt_ref, *, tiles_per_seq):
    i = pl.program_id(0)
    x = x_ref[...]
    tm = x.shape[0]
    fz = jnp.dot(x, w_ref[...].astype(BF16), preferred_element_type=F32) + b_ref[...]
    fzt = lax.dot_general(wt_ref[...].astype(BF16), x, NT, preferred_element_type=F32) + bt_ref[...]
    lf = _log_sigmoid(fz)
    lft = _log_sigmoid(fzt)
    lf_ref[...] = lf
    lft_ref[...] = lft

    @pl.when(i % tiles_per_seq == 0)
    def _():
        carry_ref[...] = jnp.zeros_like(carry_ref)
        carryt_ref[...] = jnp.zeros_like(carryt_ref)

    row = _iota((tm, tm), 0)
    col = _iota((tm, tm), 1)
    lower = (col <= row).astype(F32)
    upper = (row <= col).astype(F32)
    c = jnp.dot(lower, lf, precision=HIGHEST, preferred_element_type=F32) + carry_ref[...]
    ct = jnp.dot(lft, upper, precision=HIGHEST, preferred_element_type=F32) + carryt_ref[...]
    c_ref[...] = c
    ct_ref[...] = ct
    carry_ref[...] = c[tm - 1:tm, :]
    carryt_ref[...] = ct[:, tm - 1:tm]


def logf_project(x_bf, w_f, b_f, seq_len):
    T, D = x_bf.shape
    H = w_f.shape[1]
    tm = _tile(seq_len, (512, 256, 128))
    assert T % tm == 0
    outs = pl.pallas_call(
        functools.partial(_logf_kernel, tiles_per_seq=seq_len // tm),
        grid=(T // tm,),
        in_specs=[pl.BlockSpec((tm, D), lambda i: (i, 0)),
                  pl.BlockSpec((D, H), lambda i: (0, 0)),
                  pl.BlockSpec((H, D), lambda i: (0, 0)),
                  pl.BlockSpec((1, H), lambda i: (0, 0)),
                  pl.BlockSpec((H, 1), lambda i: (0, 0))],
        out_specs=[pl.BlockSpec((tm, H), lambda i: (i, 0)),
                   pl.BlockSpec((H, tm), lambda i: (0, i)),
                   pl.BlockSpec((tm, H), lambda i: (i, 0)),
                   pl.BlockSpec((H, tm), lambda i: (0, i))],
        out_shape=[jax.ShapeDtypeStruct((T, H), F32), jax.ShapeDtypeStruct((H, T), F32),
                   jax.ShapeDtypeStruct((T, H), F32), jax.ShapeDtypeStruct((H, T), F32)],
        scratch_shapes=[pltpu.VMEM((1, H), F32), pltpu.VMEM((H, 1), F32)],
        compiler_params=_params("arbitrary"),
        name="logf_project",
    )(x_bf, w_f, w_f.T, b_f.reshape(1, H), b_f.reshape(H, 1))
    return outs


def _conv_tail(acc, bdw, g, b):
    return _silu(_layer_norm(acc + bdw, g, b))


def _conv_prompt_kernel(ag_ref, wdw_ref, bdw_ref, g_ref, b_ref, y_ref, st_ref, ext_ref, *, W, tt, C):
    t = pl.program_id(1)

    @pl.when(t == 0)
    def _():
        ext_ref[0:CONV_HALO, :] = jnp.zeros((CONV_HALO, C), F32)

    @pl.when(t > 0)
    def _():
        ext_ref[0:CONV_HALO, :] = ext_ref[tt:tt + CONV_HALO, :]

    u = ag_ref[:, 0:C] * _sigmoid(ag_ref[:, C:2 * C])
    ext_ref[CONV_HALO:CONV_HALO + tt, :] = u
    base = CONV_HALO - (W - 1)
    acc = ext_ref[base:base + tt, :] * wdw_ref[0:1, :]
    for w in range(1, W):
        acc = acc + ext_ref[base + w:base + w + tt, :] * wdw_ref[w:w + 1, :]
    y_ref[...] = _conv_tail(acc, bdw_ref[...], g_ref[...], b_ref[...]).astype(y_ref.dtype)

    @pl.when(t == pl.num_programs(1) - 1)
    def _():
        st_ref[...] = ext_ref[CONV_HALO + tt - (W - 1):CONV_HALO + tt, :]


def conv_prompt(ag, n_seq, seq_len, w_dw, b_dw, g_n, b_n):
    W, C = w_dw.shape
    assert W - 1 <= CONV_HALO
    tt = _tile(seq_len, (256, 128))
    nt = seq_len // tt
    vec = lambda a: a.reshape(1, C)
    return pl.pallas_call(
        functools.partial(_conv_prompt_kernel, W=W, tt=tt, C=C),
        grid=(n_seq, nt),
        in_specs=[pl.BlockSpec((tt, 2 * C), lambda b, t: (b * nt + t, 0)),
                  pl.BlockSpec((W, C), lambda b, t: (0, 0)),
                  pl.BlockSpec((1, C), lambda b, t: (0, 0)),
                  pl.BlockSpec((1, C), lambda b, t: (0, 0)),
                  pl.BlockSpec((1, C), lambda b, t: (0, 0))],
        out_specs=[pl.BlockSpec((tt, C), lambda b, t: (b * nt + t, 0)),
                   pl.BlockSpec((None, W - 1, C), lambda b, t: (b, 0, 0))],
        out_shape=[jax.ShapeDtypeStruct((n_seq * seq_len, C), BF16),
                   jax.ShapeDtypeStruct((n_seq, W - 1, C), F32)],
        scratch_shapes=[pltpu.VMEM((CONV_HALO + tt, C), F32)],
        compiler_params=_params("arbitrary", "arbitrary"),
        name="conv_prompt",
    )(ag, w_dw, vec(b_dw), vec(g_n), vec(b_n))


def _conv_sample_kernel(ag_ref, st_ref, wdw_ref, bdw_ref, g_ref, b_ref, y_ref, nst_ref, ext_ref, *, W, Ls, C, bb):
    u = ag_ref[:, 0:C] * _sigmoid(ag_ref[:, C:2 * C])
    ext_ref[:, 0:W - 1, :] = st_ref[...]
    ext_ref[:, W - 1:W - 1 + Ls, :] = u.reshape(bb, Ls, C)
    acc = ext_ref[:, 0:Ls, :] * wdw_ref[0:1, :]
    for w in range(1, W):
        acc = acc + ext_ref[:, w:w + Ls, :] * wdw_ref[w:w + 1, :]
    y = _conv_tail(acc.reshape(bb * Ls, C), bdw_ref[...], g_ref[...], b_ref[...])
    y_ref[...] = y.astype(y_ref.dtype)
    nst_ref[...] = ext_ref[:, Ls:Ls + W - 1, :]


def conv_sample(ag, row_start, n_seq, Ls, state, w_dw, b_dw, g_n, b_n):
    W, C = w_dw.shape
    bb = _tile(n_seq, (8, 4, 2, 1))
    rows = bb * Ls
    assert Ls % 8 == 0 and row_start % rows == 0
    r0 = row_start // rows
    vec = lambda a: a.reshape(1, C)
    return pl.pallas_call(
        functools.partial(_conv_sample_kernel, W=W, Ls=Ls, C=C, bb=bb),
        grid=(n_seq // bb,),
        in_specs=[pl.BlockSpec((rows, 2 * C), lambda i: (r0 + i, 0)),
                  pl.BlockSpec((bb, W - 1, C), lambda i: (i, 0, 0)),
                  pl.BlockSpec((W, C), lambda i: (0, 0)),
                  pl.BlockSpec((1, C), lambda i: (0, 0)),
                  pl.BlockSpec((1, C), lambda i: (0, 0)),
                  pl.BlockSpec((1, C), lambda i: (0, 0))],
        out_specs=[pl.BlockSpec((rows, C), lambda i: (i, 0)),
                   pl.BlockSpec((bb, W - 1, C), lambda i: (i, 0, 0))],
        out_shape=[jax.ShapeDtypeStruct((n_seq * Ls, C), BF16),
                   jax.ShapeDtypeStruct((n_seq, W - 1, C), F32)],
        scratch_shapes=[pltpu.VMEM((bb, W - 1 + Ls, C), F32)],
        compiler_params=_params("arbitrary"),
        name="conv_sample",
    )(ag, state, w_dw, vec(b_dw), vec(g_n), vec(b_n))


def _fox_prompt_kernel(q_ref, k_ref, v_ref, c_ref, ct_ref, o_ref, *, tq, scale, H):
    h = pl.program_id(1)
    qi = pl.program_id(2)
    q = q_ref[...].astype(BF16)
    cq = jnp.sum(jnp.where(_iota((tq, H), 1) == h, c_ref[...], 0.0), axis=1, keepdims=True)
    row = qi * tq + _iota((tq, tq), 0)
    col0 = _iota((tq, tq), 1)

    def body(ki, carry):
        m, l, acc = carry
        start = pl.multiple_of(ki * tq, tq)
        k = k_ref[pl.ds(start, tq), :].astype(BF16)
        v = v_ref[pl.ds(start, tq), :].astype(BF16)
        ck = ct_ref[pl.ds(h, 1), pl.ds(start, tq)]
        s = lax.dot_general(q, k, NT, preferred_element_type=F32) * scale + (cq - ck)
        s = jnp.where(row >= start + col0, s, -jnp.inf)
        m_new = jnp.maximum(m, jnp.max(s, axis=1, keepdims=True))
        alpha = jnp.exp(m - m_new)
        p = jnp.exp(s - m_new)
        l = alpha * l + jnp.sum(p, axis=1, keepdims=True)
        acc = alpha * acc + jnp.dot(p.astype(BF16), v, preferred_element_type=F32)
        return m_new, l, acc

    init = (jnp.full((tq, 1), -jnp.inf, F32), jnp.zeros((tq, 1), F32), jnp.zeros((tq, LANE), F32))
    _, l, acc = lax.fori_loop(0, qi + 1, body, init)
    o_ref[...] = (acc / l).astype(o_ref.dtype)


def fox_prompt(q, k, v, c, ct, n_seq, seq_len, H):
    tq = _tile(seq_len, (512, 256, 128))
    nq = seq_len // tq
    return pl.pallas_call(
        functools.partial(_fox_prompt_kernel, tq=tq, scale=LANE ** -0.5, H=H),
        grid=(n_seq, H, nq),
        in_specs=[pl.BlockSpec((tq, LANE), lambda b, h, i: (b * nq + i, h)),
                  pl.BlockSpec((seq_len, LANE), lambda b, h, i: (b, h)),
                  pl.BlockSpec((seq_len, LANE), lambda b, h, i: (b, h)),
                  pl.BlockSpec((tq, H), lambda b, h, i: (b * nq + i, 0)),
                  pl.BlockSpec((H, seq_len), lambda b, h, i: (0, b))],
        out_specs=pl.BlockSpec((tq, LANE), lambda b, h, i: (b * nq + i, h)),
        out_shape=jax.ShapeDtypeStruct((n_seq * seq_len, H * LANE), BF16),
        compiler_params=_params("arbitrary", "arbitrary", "arbitrary"),
        name="fox_prompt",
    )(q, k, v, c, ct)


def _fox_sample_kernel(pt_ref, q_ref, kn_ref, vn_ref, lfn_ref, ck_ref, cv_ref, clf_ref, o_ref,
                       qexp_ref, m_ref, l_ref, acc_ref, rc_ref, *, H, Ls, scale):
    p = pl.program_id(1)
    n_pages = pl.num_programs(1)
    HQ = H * Ls
    D = H * LANE
    page = ck_ref.shape[0]
    head_of_row = _iota((HQ, D), 0) // Ls
    head_of_col = _iota((HQ, D), 1) // LANE

    @pl.when(p == 0)
    def _():
        q = q_ref[...]
        qexp_ref[...] = jnp.where(head_of_row == head_of_col, jnp.concatenate([q] * H, axis=0), 0.0).astype(BF16)
        m_ref[...] = jnp.full((HQ, 1), -jnp.inf, F32)
        l_ref[...] = jnp.zeros((HQ, 1), F32)
        acc_ref[...] = jnp.zeros((HQ, D), F32)
        rc_ref[...] = jnp.zeros((HQ, 1), F32)

    def update(s, v_bf):
        m_old = m_ref[...]
        m_new = jnp.maximum(m_old, jnp.max(s, axis=1, keepdims=True))
        alpha = jnp.exp(m_old - m_new)
        pr = jnp.exp(s - m_new)
        l_ref[...] = alpha * l_ref[...] + jnp.sum(pr, axis=1, keepdims=True)
        acc_ref[...] = alpha * acc_ref[...] + jnp.dot(pr.astype(BF16), v_bf, preferred_element_type=F32)
        m_ref[...] = m_new

    sel = (_iota((HQ, H), 0) // Ls == _iota((HQ, H), 1)).astype(F32)
    lf_new = lax.dot_general(sel, lfn_ref[...], NT, precision=HIGHEST, preferred_element_type=F32)
    jj = _iota((HQ, Ls), 1)
    ii = _iota((HQ, Ls), 0) % Ls
    causal = jj <= ii
    a_r = jnp.sum(jnp.where(causal, lf_new, 0.0), axis=1, keepdims=True)

    lf_page = jnp.dot(sel, clf_ref[...], precision=HIGHEST, preferred_element_type=F32)
    later = (_iota((page, page), 0) > _iota((page, page), 1)).astype(F32)
    r_in = jnp.dot(lf_page, later, precision=HIGHEST, preferred_element_type=F32) + rc_ref[...]
    rc_ref[...] = rc_ref[...] + jnp.sum(lf_page, axis=1, keepdims=True)
    qexp = qexp_ref[...]
    s = lax.dot_general(qexp, ck_ref[...].astype(BF16), NT, preferred_element_type=F32) * scale + (a_r + r_in)
    update(s, cv_ref[...].astype(BF16))

    @pl.when(p == n_pages - 1)
    def _():
        s2 = lax.dot_general(qexp, kn_ref[...].astype(BF16), NT, preferred_element_type=F32) * scale
        upto = (_iota((Ls, Ls), 0) <= _iota((Ls, Ls), 1)).astype(F32)
        cum = jnp.dot(lf_new, upto, precision=HIGHEST, preferred_element_type=F32)
        s2 = jnp.where(causal, s2 + (a_r - cum), -jnp.inf)
        update(s2, vn_ref[...].astype(BF16))
        full = jnp.where(head_of_row == head_of_col, acc_ref[...] / l_ref[...], 0.0)
        out = full[0:Ls, :]
        for h in range(1, H):
            out = out + full[h * Ls:(h + 1) * Ls, :]
        o_ref[...] = out


def fox_sample(q, k, v, lf, row_start, n_seq, Ls, cache_k, cache_v, cache_lft, layer, page_table, H):
    D = H * LANE
    n_pages = page_table.shape[1]
    page = cache_k.shape[2]
    assert Ls % 8 == 0 and row_start % Ls == 0
    r0 = row_start // Ls
    HQ = H * Ls
    rows = lambda b, p, pt: (r0 + b, 0)
    phys = lambda b, p, pt: (layer, pt[b * n_pages + (n_pages - 1 - p)], 0, 0)
    grid_spec = pltpu.PrefetchScalarGridSpec(
        num_scalar_prefetch=1,
        grid=(n_seq, n_pages),
        in_specs=[pl.BlockSpec((Ls, D), rows), pl.BlockSpec((Ls, D), rows), pl.BlockSpec((Ls, D), rows),
                  pl.BlockSpec((Ls, H), rows),
                  pl.BlockSpec((None, None, page, D), phys),
                  pl.BlockSpec((None, None, page, D), phys),
                  pl.BlockSpec((None, None, H, page), phys)],
        out_specs=pl.BlockSpec((Ls, D), lambda b, p, pt: (b, 0)),
        scratch_shapes=[pltpu.VMEM((HQ, D), BF16), pltpu.VMEM((HQ, 1), F32), pltpu.VMEM((HQ, 1), F32),
                        pltpu.VMEM((HQ, D), F32), pltpu.VMEM((HQ, 1), F32)],
    )
    return pl.pallas_call(
        functools.partial(_fox_sample_kernel, H=H, Ls=Ls, scale=LANE ** -0.5),
        grid_spec=grid_spec,
        out_shape=jax.ShapeDtypeStruct((n_seq * Ls, D), F32),
        compiler_params=_params("arbitrary", "arbitrary"),
        name="fox_sample",
    )(page_table.reshape(-1), q, k, v, lf, cache_k, cache_v, cache_lft)


def _outproj_ln_kernel(h1_ref, h2_ref, w_ref, res_ref, g_ref, b_ref, o_ref, obf_ref, *, alpha, half):
    y = jnp.dot(h1_ref[...], w_ref[0:half, :], preferred_element_type=F32)
    y = y + jnp.dot(h2_ref[...], w_ref[half:2 * half, :], preferred_element_type=F32)
    z = _layer_norm(alpha * res_ref[...] + y, g_ref[...], b_ref[...])
    o_ref[...] = z
    obf_ref[...] = z.astype(BF16)


def outproj_ln(h1, h2, w_bf, res, g, b, alpha):
    T, half = h1.shape
    D = w_bf.shape[1]
    tm = _tile(T, (256, 128))
    return pl.pallas_call(
        functools.partial(_outproj_ln_kernel, alpha=alpha, half=half),
        grid=(T // tm,),
        in_specs=[pl.BlockSpec((tm, half), lambda i: (i, 0)),
                  pl.BlockSpec((tm, half), lambda i: (i, 0)),
                  pl.BlockSpec((2 * half, D), lambda i: (0, 0)),
                  pl.BlockSpec((tm, D), lambda i: (i, 0)),
                  pl.BlockSpec((1, D), lambda i: (0, 0)),
                  pl.BlockSpec((1, D), lambda i: (0, 0))],
        out_specs=[pl.BlockSpec((tm, D), lambda i: (i, 0)), pl.BlockSpec((tm, D), lambda i: (i, 0))],
        out_shape=[jax.ShapeDtypeStruct((T, D), F32), jax.ShapeDtypeStruct((T, D), BF16)],
        compiler_params=_params("arbitrary"),
        name="outproj_ln",
    )(h1, h2, w_bf, res, g.reshape(1, D), b.reshape(1, D))


def _hgrn_kernel(*refs, C, sb, tl, bb, has_s0):
    if has_s0:
        qz_ref, fz_ref, iz_ref, gz_ref, lb_ref, go_ref, s0_ref, o_ref, sn_ref, st_ref = refs
    else:
        qz_ref, fz_ref, iz_ref, gz_ref, lb_ref, go_ref, o_ref, sn_ref, st_ref = refs
    t = pl.program_id(2)
    nsb = C // sb

    @pl.when(t == 0)
    def _():
        for s in range(bb):
            st_ref[s] = s0_ref[s].T if has_s0 else jnp.zeros((LANE, LANE), F32)

    lb = lb_ref[...]
    go = go_ref[...]
    lower = (_iota((C, C), 1) <= _iota((C, C), 0))
    lower_f = lower.astype(F32)

    def chunk(s, r0):
        q = _silu(qz_ref[pl.ds(r0, C), :])
        f = lb + (1.0 - lb) * _sigmoid(fz_ref[pl.ds(r0, C), :])
        kk = 1.0 - f
        i_bf = iz_ref[pl.ds(r0, C), :].astype(BF16)
        b = jnp.dot(lower_f, jnp.log(f), precision=HIGHEST, preferred_element_type=F32)
        b_last = b[C - 1:C, :]
        starts = [jnp.zeros((1, LANE), F32)] + [b[I * sb - 1:I * sb, :] for I in range(1, nsb)]
        lasts = [b[(I + 1) * sb - 1:(I + 1) * sb, :] for I in range(nsb)]
        blk = lambda a, I: a[I * sb:(I + 1) * sb, :]
        kd = [blk(kk, J) * jnp.exp(lasts[J] - blk(b, J)) for J in range(nsb)]
        att_rows = []
        for I in range(nsb):
            qd = blk(q, I) * jnp.exp(blk(b, I) - starts[I])
            parts = [kd[J] * jnp.exp(starts[I] - lasts[J]) for J in range(I)]
            parts.append(blk(kk, I) * jnp.exp(starts[I] - blk(b, I)))
            if I + 1 < nsb:
                parts.append(jnp.zeros(((nsb - I - 1) * sb, LANE), F32))
            kmat = jnp.concatenate(parts, axis=0) if len(parts) > 1 else parts[0]
            att_rows.append(lax.dot_general(qd.astype(BF16), kmat.astype(BF16), NT, preferred_element_type=F32))
        att = jnp.concatenate(att_rows, axis=0) if nsb > 1 else att_rows[0]
        att = jnp.where(lower, att, 0.0)
        st = st_ref[s]
        o = jnp.dot(att.astype(BF16), i_bf, preferred_element_type=F32)
        o = o + lax.dot_general((q * jnp.exp(b)).astype(BF16), st.astype(BF16), NT, preferred_element_type=F32)
        kst = (kk * jnp.exp(b_last - b)).astype(BF16)
        st_ref[s] = st * jnp.exp(b_last) + lax.dot_general(i_bf, kst, TN, preferred_element_type=F32)
        ms = jnp.mean(o * o, axis=-1, keepdims=True)
        o = o * lax.rsqrt(ms + LN_EPS) * go * _silu(gz_ref[pl.ds(r0, C), :])
        o_ref[pl.ds(r0, C), :] = o.astype(o_ref.dtype)

    n_chunks = tl // C
    for s in range(bb):
        if n_chunks == 1:
            chunk(s, s * tl)
        else:
            def body(c, carry, s=s):
                chunk(s, pl.multiple_of(s * tl + c * C, C))
                return carry
            lax.fori_loop(0, n_chunks, body, 0)

    @pl.when(t == pl.num_programs(2) - 1)
    def _():
        for s in range(bb):
            sn_ref[s] = st_ref[s].T


def hgrn(proj, row_start, n_seq, seq_len, lb, g_o, s0, H):
    C = min(HG_CHUNK, seq_len)
    sb = min(HG_SUB, C)
    assert seq_len % C == 0 and C % sb == 0
    if seq_len >= 512:
        tl, bb = _tile(seq_len, (512,)), 1
    else:
        tl, bb = seq_len, _tile(n_seq, (8, 4, 2, 1))
    nt = seq_len // tl
    rows = bb * tl
    assert row_start % rows == 0
    r0 = row_start // rows
    col = lambda k: (lambda b, h, t: (r0 + b * nt + t, k * H + h))
    in_specs = [pl.BlockSpec((rows, LANE), col(0)), pl.BlockSpec((rows, LANE), col(1)),
                pl.BlockSpec((rows, LANE), col(2)), pl.BlockSpec((rows, LANE), col(3)),
                pl.BlockSpec((1, LANE), lambda b, h, t: (0, h)),
                pl.BlockSpec((1, LANE), lambda b, h, t: (0, h))]
    args = [proj, proj, proj, proj, lb.reshape(1, H * LANE), g_o.reshape(1, H * LANE)]
    if s0 is not None:
        in_specs.append(pl.BlockSpec((bb, None, LANE, LANE), lambda b, h, t: (b, h, 0, 0)))
        args.append(s0)
    return pl.pallas_call(
        functools.partial(_hgrn_kernel, C=C, sb=sb, tl=tl, bb=bb, has_s0=s0 is not None),
        grid=(n_seq // bb, H, nt),
        in_specs=in_specs,
        out_specs=[pl.BlockSpec((rows, LANE), lambda b, h, t: (b * nt + t, h)),
                   pl.BlockSpec((bb, None, LANE, LANE), lambda b, h, t: (b, h, 0, 0))],
        out_shape=[jax.ShapeDtypeStruct((n_seq * seq_len, H * LANE), BF16),
                   jax.ShapeDtypeStruct((n_seq, H, LANE, LANE), F32)],
        scratch_shapes=[pltpu.VMEM((bb, LANE, LANE), F32)],
        compiler_params=_params("arbitrary", "arbitrary", "arbitrary"),
        name="hgrn",
    )(*args)


def _gmlp_kernel(uz_ref, vz_ref, w_ref, bt_ref, g_ref, b_ref, d_ref, v_ref, *, G, cs):
    n = uz_ref.shape[0]
    v = _layer_norm(_gelu(vz_ref[...]), g_ref[...], b_ref[...])
    v_ref[...] = v
    u = _gelu(uz_ref[...])
    t = _iota((n, n), 0)
    s = _iota((n, n), 1)
    keep = (s <= t) & (t // cs == s // cs)
    for g in range(G):
        w = jnp.where(keep, w_ref[g], 0.0).astype(BF16)
        vg = v[:, g * LANE:(g + 1) * LANE].astype(BF16)
        mixed = jnp.dot(w, vg, preferred_element_type=F32) + bt_ref[:, g:g + 1]
        d_ref[:, g * LANE:(g + 1) * LANE] = (u[:, g * LANE:(g + 1) * LANE] * mixed).astype(d_ref.dtype)


def gmlp(proj, row_start, n_rows, seq_len, w_s, b_s, g_v, b_v):
    G, n, _ = w_s.shape
    Dh = G * LANE
    cs = min(n, seq_len)
    assert n % cs == 0 and seq_len % cs == 0 and n_rows % n == 0 and row_start % n == 0
    if cs < n:
        w_s = jnp.tile(w_s[:, :cs, :cs], (1, n // cs, n // cs))
        b_s = jnp.tile(b_s[:, :cs], (1, n // cs))
    r0 = row_start // n
    return pl.pallas_call(
        functools.partial(_gmlp_kernel, G=G, cs=cs),
        grid=(n_rows // n,),
        in_specs=[pl.BlockSpec((n, Dh), lambda i: (r0 + i, 4)),
                  pl.BlockSpec((n, Dh), lambda i: (r0 + i, 5)),
                  pl.BlockSpec((G, n, n), lambda i: (0, 0, 0)),
                  pl.BlockSpec((n, G), lambda i: (0, 0)),
                  pl.BlockSpec((1, Dh), lambda i: (0, 0)),
                  pl.BlockSpec((1, Dh), lambda i: (0, 0))],
        out_specs=[pl.BlockSpec((n, Dh), lambda i: (i, 0)), pl.BlockSpec((n, Dh), lambda i: (i, 0))],
        out_shape=[jax.ShapeDtypeStruct((n_rows, Dh), BF16), jax.ShapeDtypeStruct((n_rows, Dh), F32)],
        compiler_params=_params("arbitrary"),
        name="gmlp",
    )(proj, proj, w_s, b_s.T, g_v.reshape(1, Dh), b_v.reshape(1, Dh))


def _router_kernel(x_ref, wt_ref, bt_ref, idx_ref, gate_ref, *, E):
    tm = x_ref.shape[0]
    per = E // N_GROUPS
    logits = lax.dot_general(wt_ref[...], x_ref[...], NT, precision=HIGHEST, preferred_element_type=F32)
    z = jnp.exp(logits - jnp.max(logits, axis=0, keepdims=True))
    probs = z / jnp.sum(z, axis=0, keepdims=True)
    sel = probs + bt_ref[...]
    io = _iota((per, tm), 0)

    def top2(sg):
        m1 = jnp.max(sg, axis=0, keepdims=True)
        i1 = jnp.min(jnp.where(sg == m1, io, per), axis=0, keepdims=True)
        rest = jnp.where(io == i1, -jnp.inf, sg)
        m2 = jnp.max(rest, axis=0, keepdims=True)
        i2 = jnp.min(jnp.where(rest == m2, io, per), axis=0, keepdims=True)
        return m1 + m2, i1, i2

    best, e1, e2 = top2(sel[0:per, :])
    for g in range(1, N_GROUPS):
        score, i1, i2 = top2(sel[g * per:(g + 1) * per, :])
        better = score > best
        best = jnp.where(better, score, best)
        e1 = jnp.where(better, i1 + g * per, e1)
        e2 = jnp.where(better, i2 + g * per, e2)
    eo = _iota((E, tm), 0)
    p1 = jnp.sum(jnp.where(eo == e1, probs, 0.0), axis=0, keepdims=True)
    p2 = jnp.sum(jnp.where(eo == e2, probs, 0.0), axis=0, keepdims=True)
    tot = p1 + p2
    pad_i = jnp.zeros((6, tm), jnp.int32)
    pad_f = jnp.zeros((6, tm), F32)
    idx_ref[...] = jnp.concatenate([e1, e2, pad_i], axis=0)
    gate_ref[...] = jnp.concatenate([p1 / tot, p2 / tot, pad_f], axis=0)


def router(x, w_router, b_router):
    T, D = x.shape
    E = w_router.shape[1]
    tm = _tile(T, (512, 256, 128))
    return pl.pallas_call(
        functools.partial(_router_kernel, E=E),
        grid=(T // tm,),
        in_specs=[pl.BlockSpec((tm, D), lambda i: (i, 0)),
                  pl.BlockSpec((E, D), lambda i: (0, 0)),
                  pl.BlockSpec((E, 1), lambda i: (0, 0))],
        out_specs=[pl.BlockSpec((8, tm), lambda i: (0, i)), pl.BlockSpec((8, tm), lambda i: (0, i))],
        out_shape=[jax.ShapeDtypeStruct((8, T), jnp.int32), jax.ShapeDtypeStruct((8, T), F32)],
        compiler_params=_params("arbitrary"),
        name="router",
    )(x, w_router.T, b_router.reshape(E, 1))


def _row_copy(src_hbm, row, dst_ref, slot, sem):
    return pltpu.make_async_copy(src_hbm.at[pl.ds(row, 1)], dst_ref.at[pl.ds(slot, 1)], sem)


def _gather_kernel(src_ref, x_hbm, o_ref, buf_ref, sem, *, tr):
    base = pl.program_id(0) * tr

    def start(r, carry):
        _row_copy(x_hbm, src_ref[base + r], buf_ref, r, sem).start()
        return carry

    def wait(r, carry):
        _row_copy(x_hbm, 0, buf_ref, r, sem).wait()
        return carry

    lax.fori_loop(0, tr, start, 0)
    lax.fori_loop(0, tr, wait, 0)
    o_ref[...] = buf_ref[...].astype(o_ref.dtype)


def gather_rows(x, src_rows, tr):
    R = src_rows.shape[0]
    D = x.shape[1]
    grid_spec = pltpu.PrefetchScalarGridSpec(
        num_scalar_prefetch=1,
        grid=(R // tr,),
        in_specs=[pl.BlockSpec(memory_space=pl.ANY)],
        out_specs=pl.BlockSpec((tr, D), lambda i, src: (i, 0)),
        scratch_shapes=[pltpu.VMEM((tr, D), F32), pltpu.SemaphoreType.DMA(())],
    )
    return pl.pallas_call(
        functools.partial(_gather_kernel, tr=tr),
        grid_spec=grid_spec,
        out_shape=jax.ShapeDtypeStruct((R, D), BF16),
        compiler_params=_params("arbitrary"),
        name="moe_gather",
    )(src_rows, x)


def _expert_kernel(te_ref, nv_ref, xs_ref, wg_ref, wu_ref, wd_ref, rg_ref, y_ref):
    i = pl.program_id(0)
    j = pl.program_id(1)
    valid = i < nv_ref[0]

    @pl.when(valid)
    def _():
        x = xs_ref[...]
        hg = jnp.dot(x, wg_ref[...].astype(BF16), preferred_element_type=F32)
        hu = jnp.dot(x, wu_ref[...].astype(BF16), preferred_element_type=F32)
        h = (_silu(hg) * hu).astype(BF16)
        part = jnp.dot(h, wd_ref[...].astype(BF16), preferred_element_type=F32)

        @pl.when(j == 0)
        def _():
            y_ref[...] = part

        @pl.when(j > 0)
        def _():
            y_ref[...] = y_ref[...] + part

        @pl.when(j == pl.num_programs(1) - 1)
        def _():
            y_ref[...] = y_ref[...] * rg_ref[...]

    @pl.when(jnp.logical_not(valid) & (j == 0))
    def _():
        y_ref[...] = jnp.zeros_like(y_ref)


def expert_ffn(xs, tile_expert, n_valid, row_gate, w_gate, w_up, w_down, layer, tr):
    R, D = xs.shape
    De = w_gate.shape[-1]
    te = _tile(De, (256, 128))
    grid_spec = pltpu.PrefetchScalarGridSpec(
        num_scalar_prefetch=2,
        grid=(R // tr, De // te),
        in_specs=[pl.BlockSpec((tr, D), lambda i, j, e, nv: (i, 0)),
                  pl.BlockSpec((None, None, D, te), lambda i, j, e, nv: (layer, e[i], 0, j)),
                  pl.BlockSpec((None, None, D, te), lambda i, j, e, nv: (layer, e[i], 0, j)),
                  pl.BlockSpec((None, None, te, D), lambda i, j, e, nv: (layer, e[i], j, 0)),
                  pl.BlockSpec((tr, 1), lambda i, j, e, nv: (i, 0))],
        out_specs=pl.BlockSpec((tr, D), lambda i, j, e, nv: (i, 0)),
    )
    return pl.pallas_call(
        _expert_kernel,
        grid_spec=grid_spec,
        out_shape=jax.ShapeDtypeStruct((R, D), F32),
        compiler_params=_params("arbitrary", "arbitrary"),
        name="moe_experts",
    )(tile_expert, n_valid, xs, w_gate, w_up, w_down, row_gate)


def _combine_ln_kernel(pos_ref, y_hbm, res_ref, g_ref, b_ref, o_ref, obf_ref, buf_ref, sem, *, tm, alpha):
    base = pl.program_id(0) * tm

    def start(r, carry):
        _row_copy(y_hbm, pos_ref[2 * (base + r)], buf_ref.at[0], r, sem).start()
        _row_copy(y_hbm, pos_ref[2 * (base + r) + 1], buf_ref.at[1], r, sem).start()
        return carry

    def wait(r, carry):
        _row_copy(y_hbm, 0, buf_ref.at[0], r, sem).wait()
        _row_copy(y_hbm, 0, buf_ref.at[1], r, sem).wait()
        return carry

    lax.fori_loop(0, tm, start, 0)
    lax.fori_loop(0, tm, wait, 0)
    z = _layer_norm(alpha * res_ref[...] + (buf_ref[0] + buf_ref[1]), g_ref[...], b_ref[...])
    o_ref[...] = z
    obf_ref[...] = z.astype(BF16)


def combine_ln(y_sorted, pos, res, g, b, alpha):
    T, D = res.shape
    tm = _tile(T, (256, 128))
    grid_spec = pltpu.PrefetchScalarGridSpec(
        num_scalar_prefetch=1,
        grid=(T // tm,),
        in_specs=[pl.BlockSpec(memory_space=pl.ANY),
                  pl.BlockSpec((tm, D), lambda i, pos: (i, 0)),
                  pl.BlockSpec((1, D), lambda i, pos: (0, 0)),
                  pl.BlockSpec((1, D), lambda i, pos: (0, 0))],
        out_specs=[pl.BlockSpec((tm, D), lambda i, pos: (i, 0)), pl.BlockSpec((tm, D), lambda i, pos: (i, 0))],
        scratch_shapes=[pltpu.VMEM((2, tm, D), F32), pltpu.SemaphoreType.DMA(())],
    )
    return pl.pallas_call(
        functools.partial(_combine_ln_kernel, tm=tm, alpha=alpha),
        grid_spec=grid_spec,
        out_shape=[jax.ShapeDtypeStruct((T, D), F32), jax.ShapeDtypeStruct((T, D), BF16)],
        compiler_params=_params("arbitrary"),
        name="moe_combine_ln",
    )(pos, y_sorted, res, g.reshape(1, D), b.reshape(1, D))


def _dispatch_plan(e_idx, gates, E, tr):
    T = e_idx.shape[1]
    pairs = 2 * T
    e_flat = e_idx[0:2, :].T.reshape(pairs)
    g_flat = gates[0:2, :].T.reshape(pairs)
    order = jnp.argsort(e_flat, stable=True).astype(jnp.int32)
    counts = jnp.zeros((E,), jnp.int32).at[e_flat].add(1)
    padded = ((counts + tr - 1) // tr) * tr
    start = jnp.cumsum(counts) - counts
    pstart = jnp.cumsum(padded) - padded
    n_tiles = pairs // tr + E
    R = n_tiles * tr
    rows = jnp.arange(R, dtype=jnp.int32)
    pend = pstart + padded
    row_e = jnp.minimum(jnp.sum((rows[:, None] >= pend[None, :]).astype(jnp.int32), axis=1), E - 1)
    off = rows - pstart[row_e]
    valid = off < counts[row_e]
    src_pair = order[jnp.clip(start[row_e] + off, 0, pairs - 1)]
    src_tok = jnp.where(valid, src_pair // 2, 0).astype(jnp.int32)
    row_gate = jnp.where(valid, g_flat[src_pair], 0.0).reshape(R, 1)
    rank = jnp.argsort(order).astype(jnp.int32)
    pos = (pstart[e_flat] + rank - start[e_flat]).astype(jnp.int32)
    n_valid = (jnp.sum(padded) // tr).astype(jnp.int32).reshape(1)
    tile_e = row_e[::tr]
    last_e = tile_e[jnp.maximum(n_valid[0] - 1, 0)]
    tile_e = jnp.where(jnp.arange(n_tiles) < n_valid[0], tile_e, last_e).astype(jnp.int32)
    return src_tok, row_gate, pos, tile_e, n_valid


def moe_ln(x, x_res_scale, w_router, b_router, w_gate, w_up, w_down, layer, g, b):
    T, D = x.shape
    E = w_router.shape[1]
    tr = 512 if (2 * T) % 512 == 0 else 128
    e_idx, gates = router(x, w_router, b_router)
    src_tok, row_gate, pos, tile_e, n_valid = _dispatch_plan(e_idx, gates, E, tr)
    xs = gather_rows(x, src_tok, _tile(src_tok.shape[0], (256, 128)))
    ys = expert_ffn(xs, tile_e, n_valid, row_gate, w_gate, w_up, w_down, layer, tr)
    return combine_ln(ys, pos, x, g, b, x_res_scale)


def kernel(x_prompt, x_sample, cache_k, cache_v, cache_logf, state_conv, state_hgrn, page_table, w_in_even, b_fgate, w_dw, b_dw, g_cnorm, b_cnorm, w_out_even, w_in_odd, lb_logits, g_onorm, g_vnorm, b_vnorm, w_sgu, b_sgu, w_out_odd, ln1_g, ln1_b, ln2_g, ln2_b, w_router, b_router, w_gate, w_up, w_down):
    Bp, Lp, D = x_prompt.shape
    Bs, Ls, _ = x_sample.shape
    Dh = D // 2
    H = b_fgate.shape[1]
    HG = state_hgrn.shape[2]
    assert Dh == H * LANE and Dh == HG * LANE and Dh == w_sgu.shape[1] * LANE
    depth = ln1_g.shape[0]
    alpha = (2 * depth) ** 0.25
    Tp, Ts = Bp * Lp, Bs * Ls
    n_even = cache_k.shape[0]
    n_phys, page = cache_k.shape[1], cache_k.shape[2]

    lb_p = jax.nn.softmax(lb_logits.astype(F32), axis=0)
    lb_all = jnp.cumsum(lb_p, axis=0) - lb_p[0]

    x = jnp.concatenate([x_prompt.reshape(Tp, D), x_sample.reshape(Ts, D)], axis=0)
    x_bf = x.astype(BF16)
    ck = cache_k.reshape(n_even, n_phys, page, Dh)
    cv = cache_v.reshape(n_even, n_phys, page, Dh)
    clft = jnp.swapaxes(cache_logf, 2, 3)

    out = {k: [] for k in ("kp", "vp", "lfp", "convp", "hgp", "ks", "vs", "lfs", "convs", "hgs", "mlpv")}
    for l in range(depth):
        j = l // 2
        if l % 2 == 0:
            w_in = w_in_even[j]
            ag = matmul_cols(x_bf, w_in, 0, 2 * Dh)
            q = matmul_cols(x_bf, w_in, 2 * Dh, Dh)
            k = matmul_cols(x_bf, w_in, 3 * Dh, Dh)
            v = matmul_cols(x_bf, w_in, 4 * Dh, Dh)
            lf, _, c, ct = logf_project(x_bf, w_in[:, 5 * Dh:], b_fgate[j], Lp)
            conv_args = (w_dw[j], b_dw[j], g_cnorm[j], b_cnorm[j])
            ca_p, cst_p = conv_prompt(ag, Bp, Lp, *conv_args)
            ca_s, cst_s = conv_sample(ag, Tp, Bs, Ls, state_conv[j], *conv_args)
            att_p = fox_prompt(q, k, v, c, ct, Bp, Lp, H)
            att_s = fox_sample(q, k, v, lf, Tp, Bs, Ls, ck, cv, clft, j, page_table, H)
            h1 = jnp.concatenate([ca_p, ca_s], axis=0)
            h2 = jnp.concatenate([att_p, att_s.astype(BF16)], axis=0)
            w_out = w_out_even[j]
            out["kp"].append(k[:Tp].reshape(Bp, Lp, H, LANE))
            out["vp"].append(v[:Tp].reshape(Bp, Lp, H, LANE))
            out["lfp"].append(lf[:Tp].reshape(Bp, Lp, H))
            out["convp"].append(cst_p)
            out["ks"].append(k[Tp:].reshape(Bs, Ls, H, LANE))
            out["vs"].append(v[Tp:].reshape(Bs, Ls, H, LANE))
            out["lfs"].append(lf[Tp:].reshape(Bs, Ls, H))
            out["convs"].append(cst_s)
        else:
            proj = matmul_cols(x_bf, w_in_odd[j], 0, 6 * Dh)
            o_p, s_p = hgrn(proj, 0, Bp, Lp, lb_all[l], g_onorm[j], None, HG)
            o_s, s_s = hgrn(proj, Tp, Bs, Ls, lb_all[l], g_onorm[j], state_hgrn[j].astype(F32), HG)
            mlp_args = (w_sgu[j], b_sgu[j], g_vnorm[j], b_vnorm[j])
            d_p, _ = gmlp(proj, 0, Tp, Lp, *mlp_args)
            d_s, v_s = gmlp(proj, Tp, Ts, Ls, *mlp_args)
            h1 = jnp.concatenate([o_p, o_s], axis=0)
            h2 = jnp.concatenate([d_p, d_s], axis=0)
            w_out = w_out_odd[j]
            out["hgp"].append(s_p)
            out["hgs"].append(s_s)
            out["mlpv"].append(v_s.reshape(Bs, Ls, Dh))
        x, _ = outproj_ln(h1, h2, w_out.astype(BF16), x, ln1_g[l], ln1_b[l], alpha)
        x, x_bf = moe_ln(x, alpha, w_router, b_router, w_gate, w_up, w_down, l, ln2_g[l], ln2_b[l])

    stack = lambda name: jnp.stack(out[name])
    return (x[:Tp].reshape(Bp, Lp, D), x[Tp:].reshape(Bs, Ls, D),
            stack("kp"), stack("vp"), stack("lfp"), stack("convp"), stack("hgp"),
            stack("ks"), stack("vs"), stack("lfs"), stack("convs"), stack("hgs"), stack("mlpv"))
```

```python
import functools

import jax
import jax.numpy as jnp
from jax import lax
from jax.experimental import pallas as pl
from jax.experimental.pallas import tpu as pltpu

F32 = jnp.float32
BF16 = jnp.bfloat16
HIGHEST = lax.Precision.HIGHEST

LANE = 128
VMEM_LIMIT = 56 * 1024 * 1024
LN_EPS = 1e-5
HG_CHUNK = 64
HG_SUB = 16
CONV_HALO = 32
N_GROUPS = 4
NT = (((1,), (1,)), ((), ()))
TN = (((0,), (0,)), ((), ()))


def _params(*sem):
    return pltpu.CompilerParams(dimension_semantics=sem, vmem_limit_bytes=VMEM_LIMIT)


def _tile(n, prefs):
    for t in prefs:
        if n % t == 0:
            return t
    raise ValueError(f"no tile in {prefs} divides {n}")


def _sigmoid(x):
    return 1.0 / (1.0 + jnp.exp(-x))


def _silu(x):
    return x * _sigmoid(x)


def _gelu(x):
    return 0.5 * x * (1.0 + jnp.tanh(0.7978845608028654 * (x + 0.044715 * (x * x * x))))


def _log_sigmoid(x):
    return -(jnp.maximum(-x, 0.0) + jnp.log1p(jnp.exp(-jnp.abs(x))))


def _layer_norm(x, g, b):
    mu = jnp.mean(x, axis=-1, keepdims=True)
    xc = x - mu
    var = jnp.mean(xc * xc, axis=-1, keepdims=True)
    return xc * lax.rsqrt(var + LN_EPS) * g + b


def _iota(shape, dim):
    return lax.broadcasted_iota(jnp.int32, shape, dim)


def _matmul_kernel(x_ref, w_ref, o_ref, wbf_ref):
    @pl.when(pl.program_id(1) == 0)
    def _():
        wbf_ref[...] = w_ref[...].astype(BF16)

    o_ref[...] = jnp.dot(x_ref[...], wbf_ref[...], preferred_element_type=F32).astype(o_ref.dtype)


def matmul_cols(x_bf, w, col_start, n_cols, out_dtype=F32):
    M, K = x_bf.shape
    tm = _tile(M, (1024, 512, 256, 128))
    tn = _tile(n_cols, (512, 256, 128))
    assert col_start % tn == 0
    off = col_start // tn
    return pl.pallas_call(
        _matmul_kernel,
        grid=(n_cols // tn, M // tm),
        in_specs=[pl.BlockSpec((tm, K), lambda j, i: (i, 0)),
                  pl.BlockSpec((K, tn), lambda j, i: (0, j + off))],
        out_specs=pl.BlockSpec((tm, tn), lambda j, i: (i, j)),
        out_shape=jax.ShapeDtypeStruct((M, n_cols), out_dtype),
        scratch_shapes=[pltpu.VMEM((K, tn), BF16)],
        compiler_params=_params("arbitrary", "arbitrary"),
        name="matmul_cols",
    )(x_bf, w)


def _logf_kernel(x_ref, w_ref, wt_ref, b_ref, bt_ref, lf_ref, lft_ref, c_ref, ct_ref,
                 carry_ref, carryt_ref, *, tiles_per_seq):
    i = pl.program_id(0)
    x = x_ref[...]
    tm = x.shape[0]
    fz = jnp.dot(x, w_ref[...].astype(BF16), preferred_element_type=F32) + b_ref[...]
    fzt = lax.dot_general(wt_ref[...].astype(BF16), x, NT, preferred_element_type=F32) + bt_ref[...]
    lf = _log_sigmoid(fz)
    lft = _log_sigmoid(fzt)
    lf_ref[...] = lf
    lft_ref[...] = lft

    @pl.when(i % tiles_per_seq == 0)
    def _():
        carry_ref[...] = jnp.zeros_like(carry_ref)
        carryt_ref[...] = jnp.zeros_like(carryt_ref)

    row = _iota((tm, tm), 0)
    col = _iota((tm, tm), 1)
    lower = (col <= row).astype(F32)
    upper = (row <= col).astype(F32)
    c = jnp.dot(lower, lf, precision=HIGHEST, preferred_element_type=F32) + carry_ref[...]
    ct = jnp.dot(lft, upper, precision=HIGHEST, preferred_element_type=F32) + carryt_ref[...]
    c_ref[...] = c
    ct_ref[...] = ct
    carry_ref[...] = c[tm - 1:tm, :]
    carryt_ref[...] = ct[:, tm - 1:tm]


def logf_project(x_bf, w_f, b_f, seq_len):
    T, D = x_bf.shape
    H = w_f.shape[1]
    tm = _tile(seq_len, (512, 256, 128))
    assert T % tm == 0
    outs = pl.pallas_call(
        functools.partial(_logf_kernel, tiles_per_seq=seq_len // tm),
        grid=(T // tm,),
        in_specs=[pl.BlockSpec((tm, D), lambda i: (i, 0)),
                  pl.BlockSpec((D, H), lambda i: (0, 0)),
                  pl.BlockSpec((H, D), lambda i: (0, 0)),
                  pl.BlockSpec((1, H), lambda i: (0, 0)),
                  pl.BlockSpec((H, 1), lambda i: (0, 0))],
        out_specs=[pl.BlockSpec((tm, H), lambda i: (i, 0)),
                   pl.BlockSpec((H, tm), lambda i: (0, i)),
                   pl.BlockSpec((tm, H), lambda i: (i, 0)),
                   pl.BlockSpec((H, tm), lambda i: (0, i))],
        out_shape=[jax.ShapeDtypeStruct((T, H), F32), jax.ShapeDtypeStruct((H, T), F32),
                   jax.ShapeDtypeStruct((T, H), F32), jax.ShapeDtypeStruct((H, T), F32)],
        scratch_shapes=[pltpu.VMEM((1, H), F32), pltpu.VMEM((H, 1), F32)],
        compiler_params=_params("arbitrary"),
        name="logf_project",
    )(x_bf, w_f, w_f.T, b_f.reshape(1, H), b_f.reshape(H, 1))
    return outs


def _conv_tail(acc, bdw, g, b):
    return _silu(_layer_norm(acc + bdw, g, b))


def _conv_prompt_kernel(ag_ref, wdw_ref, bdw_ref, g_ref, b_ref, y_ref, st_ref, ext_ref, *, W, tt, C):
    t = pl.program_id(1)

    @pl.when(t == 0)
    def _():
        ext_ref[0:CONV_HALO, :] = jnp.zeros((CONV_HALO, C), F32)

    @pl.when(t > 0)
    def _():
        ext_ref[0:CONV_HALO, :] = ext_ref[tt:tt + CONV_HALO, :]

    u = ag_ref[:, 0:C] * _sigmoid(ag_ref[:, C:2 * C])
    ext_ref[CONV_HALO:CONV_HALO + tt, :] = u
    base = CONV_HALO - (W - 1)
    acc = ext_ref[base:base + tt, :] * wdw_ref[0:1, :]
    for w in range(1, W):
        acc = acc + ext_ref[base + w:base + w + tt, :] * wdw_ref[w:w + 1, :]
    y_ref[...] = _conv_tail(acc, bdw_ref[...], g_ref[...], b_ref[...]).astype(y_ref.dtype)

    @pl.when(t == pl.num_programs(1) - 1)
    def _():
        st_ref[...] = ext_ref[CONV_HALO + tt - (W - 1):CONV_HALO + tt, :]


def conv_prompt(ag, n_seq, seq_len, w_dw, b_dw, g_n, b_n):
    W, C = w_dw.shape
    assert W - 1 <= CONV_HALO
    tt = _tile(seq_len, (256, 128))
    nt = seq_len // tt
    vec = lambda a: a.reshape(1, C)
    return pl.pallas_call(
        functools.partial(_conv_prompt_kernel, W=W, tt=tt, C=C),
        grid=(n_seq, nt),
        in_specs=[pl.BlockSpec((tt, 2 * C), lambda b, t: (b * nt + t, 0)),
                  pl.BlockSpec((W, C), lambda b, t: (0, 0)),
                  pl.BlockSpec((1, C), lambda b, t: (0, 0)),
                  pl.BlockSpec((1, C), lambda b, t: (0, 0)),
                  pl.BlockSpec((1, C), lambda b, t: (0, 0))],
        out_specs=[pl.BlockSpec((tt, C), lambda b, t: (b * nt + t, 0)),
                   pl.BlockSpec((None, W - 1, C), lambda b, t: (b, 0, 0))],
        out_shape=[jax.ShapeDtypeStruct((n_seq * seq_len, C), BF16),
                   jax.ShapeDtypeStruct((n_seq, W - 1, C), F32)],
        scratch_shapes=[pltpu.VMEM((CONV_HALO + tt, C), F32)],
        compiler_params=_params("arbitrary", "arbitrary"),
        name="conv_prompt",
    )(ag, w_dw, vec(b_dw), vec(g_n), vec(b_n))


def _conv_sample_kernel(ag_ref, st_ref, wdw_ref, bdw_ref, g_ref, b_ref, y_ref, nst_ref, ext_ref, *, W, Ls, C, bb):
    u = ag_ref[:, 0:C] * _sigmoid(ag_ref[:, C:2 * C])
    ext_ref[:, 0:W - 1, :] = st_ref[...]
    ext_ref[:, W - 1:W - 1 + Ls, :] = u.reshape(bb, Ls, C)
    acc = ext_ref[:, 0:Ls, :] * wdw_ref[0:1, :]
    for w in range(1, W):
        acc = acc + ext_ref[:, w:w + Ls, :] * wdw_ref[w:w + 1, :]
    y = _conv_tail(acc.reshape(bb * Ls, C), bdw_ref[...], g_ref[...], b_ref[...])
    y_ref[...] = y.astype(y_ref.dtype)
    nst_ref[...] = ext_ref[:, Ls:Ls + W - 1, :]


def conv_sample(ag, row_start, n_seq, Ls, state, w_dw, b_dw, g_n, b_n):
    W, C = w_dw.shape
    bb = _tile(n_seq, (8, 4, 2, 1))
    rows = bb * Ls
    assert Ls % 8 == 0 and row_start % rows == 0
    r0 = row_start // rows
    vec = lambda a: a.reshape(1, C)
    return pl.pallas_call(
        functools.partial(_conv_sample_kernel, W=W, Ls=Ls, C=C, bb=bb),
        grid=(n_seq // bb,),
        in_specs=[pl.BlockSpec((rows, 2 * C), lambda i: (r0 + i, 0)),
                  pl.BlockSpec((bb, W - 1, C), lambda i: (i, 0, 0)),
                  pl.BlockSpec((W, C), lambda i: (0, 0)),
                  pl.BlockSpec((1, C), lambda i: (0, 0)),
                  pl.BlockSpec((1, C), lambda i: (0, 0)),
                  pl.BlockSpec((1, C), lambda i: (0, 0))],
        out_specs=[pl.BlockSpec((rows, C), lambda i: (i, 0)),
                   pl.BlockSpec((bb, W - 1, C), lambda i: (i, 0, 0))],
        out_shape=[jax.ShapeDtypeStruct((n_seq * Ls, C), BF16),
                   jax.ShapeDtypeStruct((n_seq, W - 1, C), F32)],
        scratch_shapes=[pltpu.VMEM((bb, W - 1 + Ls, C), F32)],
        compiler_params=_params("arbitrary"),
        name="conv_sample",
    )(ag, state, w_dw, vec(b_dw), vec(g_n), vec(b_n))


def _fox_prompt_kernel(q_ref, k_ref, v_ref, c_ref, ct_ref, o_ref, *, tq, scale, H):
    h = pl.program_id(1)
    qi = pl.program_id(2)
    q = q_ref[...].astype(BF16)
    cq = jnp.sum(jnp.where(_iota((tq, H), 1) == h, c_ref[...], 0.0), axis=1, keepdims=True)
    row = qi * tq + _iota((tq, tq), 0)
    col0 = _iota((tq, tq), 1)

    def body(ki, carry):
        m, l, acc = carry
        start = pl.multiple_of(ki * tq, tq)
        k = k_ref[pl.ds(start, tq), :].astype(BF16)
        v = v_ref[pl.ds(start, tq), :].astype(BF16)
        ck = ct_ref[pl.ds(h, 1), pl.ds(start, tq)]
        s = lax.dot_general(q, k, NT, preferred_element_type=F32) * scale + (cq - ck)
        s = jnp.where(row >= start + col0, s, -jnp.inf)
        m_new = jnp.maximum(m, jnp.max(s, axis=1, keepdims=True))
        alpha = jnp.exp(m - m_new)
        p = jnp.exp(s - m_new)
        l = alpha * l + jnp.sum(p, axis=1, keepdims=True)
        acc = alpha * acc + jnp.dot(p.astype(BF16), v, preferred_element_type=F32)
        return m_new, l, acc

    init = (jnp.full((tq, 1), -jnp.inf, F32), jnp.zeros((tq, 1), F32), jnp.zeros((tq, LANE), F32))
    _, l, acc = lax.fori_loop(0, qi + 1, body, init)
    o_ref[...] = (acc / l).astype(o_ref.dtype)


def fox_prompt(q, k, v, c, ct, n_seq, seq_len, H):
    tq = _tile(seq_len, (512, 256, 128))
    nq = seq_len // tq
    return pl.pallas_call(
        functools.partial(_fox_prompt_kernel, tq=tq, scale=LANE ** -0.5, H=H),
        grid=(n_seq, H, nq),
        in_specs=[pl.BlockSpec((tq, LANE), lambda b, h, i: (b * nq + i, h)),
                  pl.BlockSpec((seq_len, LANE), lambda b, h, i: (b, h)),
                  pl.BlockSpec((seq_len, LANE), lambda b, h, i: (b, h)),
                  pl.BlockSpec((tq, H), lambda b, h, i: (b * nq + i, 0)),
                  pl.BlockSpec((H, seq_len), lambda b, h, i: (0, b))],
        out_specs=pl.BlockSpec((tq, LANE), lambda b, h, i: (b * nq + i, h)),
        out_shape=jax.ShapeDtypeStruct((n_seq * seq_len, H * LANE), BF16),
        compiler_params=_params("arbitrary", "arbitrary", "arbitrary"),
        name="fox_prompt",
    )(q, k, v, c, ct)


def _split3_dot(x, w_bf):
    x1 = x.astype(BF16)
    r1 = x - x1.astype(F32)
    x2 = r1.astype(BF16)
    x3 = (r1 - x2.astype(F32)).astype(BF16)
    dot = lambda a: jnp.dot(a, w_bf, preferred_element_type=F32)
    return dot(x1) + dot(x2) + dot(x3)


def _page_suffix_kernel(lf_ref, rin_ref, tot_ref, later_ref, same_ref, *, H):
    n = lf_ref.shape[1]

    @pl.when(pl.program_id(0) == 0)
    def _():
        r = _iota((n, n), 0)
        c = _iota((n, n), 1)
        same = (r % H) == (c % H)
        later_ref[...] = jnp.where(same & (r > c), 1.0, 0.0).astype(BF16)
        same_ref[...] = jnp.where(same, 1.0, 0.0).astype(BF16)

    x = lf_ref[...]
    rin_ref[...] = _split3_dot(x, later_ref[...])
    tot_ref[...] = _split3_dot(x, same_ref[...])


def page_suffix(lf_pages, H):
    P, n = lf_pages.shape
    tp = _tile(P, (256, 128, 64, 32, 16, 8))
    return pl.pallas_call(
        functools.partial(_page_suffix_kernel, H=H),
        grid=(P // tp,),
        in_specs=[pl.BlockSpec((tp, n), lambda i: (i, 0))],
        out_specs=[pl.BlockSpec((tp, n), lambda i: (i, 0)), pl.BlockSpec((tp, n), lambda i: (i, 0))],
        out_shape=[jax.ShapeDtypeStruct((P, n), F32), jax.ShapeDtypeStruct((P, n), F32)],
        scratch_shapes=[pltpu.VMEM((n, n), BF16), pltpu.VMEM((n, n), BF16)],
        compiler_params=_params("arbitrary"),
        name="page_suffix",
    )(lf_pages)


def _fox_sample_kernel(*refs, H, Ls, scale, pps):
    pt_ref, q_ref, kn_ref, vn_ref, lfn_ref = refs[0:5]
    ck_refs = refs[5:5 + pps]
    cv_refs = refs[5 + pps:5 + 2 * pps]
    rin_refs = refs[5 + 2 * pps:5 + 3 * pps]
    tot_refs = refs[5 + 3 * pps:5 + 4 * pps]
    o_ref, qh_ref, a_ref, m_ref, l_ref, acc_ref, rc_ref = refs[5 + 4 * pps:]
    p = pl.program_id(1)
    HQ = H * Ls
    cols = ck_refs[0].shape[0] * H
    sel = (_iota((HQ, H), 0) // Ls == _iota((HQ, H), 1)).astype(F32)
    causal = _iota((HQ, Ls), 1) <= _iota((HQ, Ls), 0) % Ls

    def new_logf():
        return lax.dot_general(sel, lfn_ref[...], NT, precision=HIGHEST, preferred_element_type=F32)

    @pl.when(p == 0)
    def _():
        q = q_ref[...]
        qh_ref[...] = jnp.concatenate([q[:, h * LANE:(h + 1) * LANE] for h in range(H)], axis=0).astype(BF16)
        a_ref[...] = jnp.sum(jnp.where(causal, new_logf(), 0.0), axis=1, keepdims=True)
        m_ref[...] = jnp.full((HQ, 1), -jnp.inf, F32)
        l_ref[...] = jnp.zeros((HQ, 1), F32)
        acc_ref[...] = jnp.zeros((HQ, LANE), F32)
        rc_ref[...] = jnp.zeros((1, cols), F32)

    def update(scores, pvs):
        m_old = m_ref[...]
        m_new = m_old
        for s in scores:
            m_new = jnp.maximum(m_new, jnp.max(s, axis=1, keepdims=True))
        alpha = jnp.exp(m_old - m_new)
        l_new = alpha * l_ref[...]
        acc = alpha * acc_ref[...]
        for s, pv in zip(scores, pvs):
            pr = jnp.exp(s - m_new)
            l_new = l_new + jnp.sum(pr, axis=1, keepdims=True)
            acc = acc + pv(pr.astype(BF16))
        l_ref[...] = l_new
        acc_ref[...] = acc
        m_ref[...] = m_new

    qh = qh_ref[...]
    own_head = (_iota((HQ, cols), 0) // Ls) == (_iota((HQ, cols), 1) % H)
    rc = rc_ref[...]
    scores, pvs = [], []
    for s_ in range(pps):
        kx = ck_refs[s_][...].reshape(cols, LANE).astype(BF16)
        vx = cv_refs[s_][...].reshape(cols, LANE).astype(BF16)
        bias = a_ref[...] + (rin_refs[s_][...] + rc)
        rc = rc + tot_refs[s_][...]
        sc = lax.dot_general(qh, kx, NT, preferred_element_type=F32) * scale + bias
        scores.append(jnp.where(own_head, sc, -jnp.inf))
        pvs.append(lambda pr, vx=vx: jnp.dot(pr, vx, preferred_element_type=F32))
    rc_ref[...] = rc
    update(scores, pvs)

    @pl.when(p == pl.num_programs(1) - 1)
    def _():
        head = lambda a, h: a[:, h * LANE:(h + 1) * LANE].astype(BF16)
        rows = lambda a, h: a[h * Ls:(h + 1) * Ls, :]
        kn = kn_ref[...]
        vn = vn_ref[...]
        s2 = jnp.concatenate([lax.dot_general(rows(qh, h), head(kn, h), NT, preferred_element_type=F32)
                              for h in range(H)], axis=0)
        upto = (_iota((Ls, Ls), 0) <= _iota((Ls, Ls), 1)).astype(F32)
        cum = jnp.dot(new_logf(), upto, precision=HIGHEST, preferred_element_type=F32)
        s2 = jnp.where(causal, s2 * scale + (a_ref[...] - cum), -jnp.inf)
        update([s2], [lambda pr: jnp.concatenate(
            [jnp.dot(rows(pr, h), head(vn, h), preferred_element_type=F32) for h in range(H)], axis=0)])
        out = acc_ref[...] / l_ref[...]
        for h in range(H):
            o_ref[:, h * LANE:(h + 1) * LANE] = rows(out, h)


def fox_sample(q, k, v, lf, row_start, n_seq, Ls, cache_k, cache_v, rin, tot, layer, page_table, H):
    D = H * LANE
    n_pages = page_table.shape[1]
    n_phys, page = cache_k.shape[1], cache_k.shape[2]
    pps = _tile(n_pages, (4, 2, 1))
    assert Ls % 8 == 0 and row_start % Ls == 0
    r0 = row_start // Ls
    HQ = H * Ls
    rows = lambda b, p, pt: (r0 + b, 0)

    def phys(s_):
        return lambda b, p, pt: pt[b * n_pages + (n_pages - 1 - (p * pps + s_))]

    kv_spec = lambda s_: pl.BlockSpec((None, None, page, H, LANE),
                                      lambda b, p, pt: (layer, phys(s_)(b, p, pt), 0, 0, 0))
    row_spec = lambda s_: pl.BlockSpec((None, 1, page * H),
                                       lambda b, p, pt: (layer * n_phys + phys(s_)(b, p, pt), 0, 0))
    grid_spec = pltpu.PrefetchScalarGridSpec(
        num_scalar_prefetch=1,
        grid=(n_seq, n_pages // pps),
        in_specs=([pl.BlockSpec((Ls, D), rows), pl.BlockSpec((Ls, D), rows), pl.BlockSpec((Ls, D), rows),
                   pl.BlockSpec((Ls, H), rows)]
                  + [kv_spec(s_) for s_ in range(pps)] + [kv_spec(s_) for s_ in range(pps)]
                  + [row_spec(s_) for s_ in range(pps)] + [row_spec(s_) for s_ in range(pps)]),
        out_specs=pl.BlockSpec((Ls, D), lambda b, p, pt: (b, 0)),
        scratch_shapes=[pltpu.VMEM((HQ, LANE), BF16), pltpu.VMEM((HQ, 1), F32), pltpu.VMEM((HQ, 1), F32),
                        pltpu.VMEM((HQ, 1), F32), pltpu.VMEM((HQ, LANE), F32), pltpu.VMEM((1, page * H), F32)],
    )
    return pl.pallas_call(
        functools.partial(_fox_sample_kernel, H=H, Ls=Ls, scale=LANE ** -0.5, pps=pps),
        grid_spec=grid_spec,
        out_shape=jax.ShapeDtypeStruct((n_seq * Ls, D), F32),
        compiler_params=_params("arbitrary", "arbitrary"),
        name="fox_sample",
    )(page_table.reshape(-1), q, k, v, lf, *([cache_k] * pps), *([cache_v] * pps),
      *([rin] * pps), *([tot] * pps))


def _to_slabs(ref, x):
    for k in range(ref.shape[1]):
        ref[:, k, :] = x[:, k * LANE:(k + 1) * LANE]


def _from_slabs(load, S):
    return jnp.concatenate([load(k) for k in range(S)], axis=1)


def _outproj_ln_kernel(h1_ref, h2_ref, w_ref, res_ref, g_ref, b_ref, o_ref, oslab_ref, *, alpha, half):
    y = jnp.dot(h1_ref[...], w_ref[0:half, :], preferred_element_type=F32)
    y = y + jnp.dot(h2_ref[...], w_ref[half:2 * half, :], preferred_element_type=F32)
    z = _layer_norm(alpha * res_ref[...] + y, g_ref[...], b_ref[...])
    o_ref[...] = z
    _to_slabs(oslab_ref, z)


def outproj_ln(h1, h2, w_bf, res, g, b, alpha):
    T, half = h1.shape
    D = w_bf.shape[1]
    S = D // LANE
    tm = _tile(T, (256, 128))
    return pl.pallas_call(
        functools.partial(_outproj_ln_kernel, alpha=alpha, half=half),
        grid=(T // tm,),
        in_specs=[pl.BlockSpec((tm, half), lambda i: (i, 0)),
                  pl.BlockSpec((tm, half), lambda i: (i, 0)),
                  pl.BlockSpec((2 * half, D), lambda i: (0, 0)),
                  pl.BlockSpec((tm, D), lambda i: (i, 0)),
                  pl.BlockSpec((1, D), lambda i: (0, 0)),
                  pl.BlockSpec((1, D), lambda i: (0, 0))],
        out_specs=[pl.BlockSpec((tm, D), lambda i: (i, 0)), pl.BlockSpec((tm, S, LANE), lambda i: (i, 0, 0))],
        out_shape=[jax.ShapeDtypeStruct((T, D), F32), jax.ShapeDtypeStruct((T, S, LANE), F32)],
        compiler_params=_params("arbitrary"),
        name="outproj_ln",
    )(h1, h2, w_bf, res, g.reshape(1, D), b.reshape(1, D))


def _hgrn_kernel(*refs, C, sb, tl, bb, hh, has_s0):
    if has_s0:
        qz_ref, fz_ref, iz_ref, gz_ref, lb_ref, go_ref, s0_ref, o_ref, sn_ref, st_ref = refs
    else:
        qz_ref, fz_ref, iz_ref, gz_ref, lb_ref, go_ref, o_ref, sn_ref, st_ref = refs
    t = pl.program_id(2)
    nsb = C // sb

    @pl.when(t == 0)
    def _():
        for s in range(bb):
            for h in range(hh):
                st_ref[s, h] = s0_ref[s, h].T if has_s0 else jnp.zeros((LANE, LANE), F32)

    lower = (_iota((C, C), 1) <= _iota((C, C), 0))
    lower_f = lower.astype(F32)

    def head_chunk(qz, fz, iz, gz, lb, go, st):
        q = _silu(qz)
        f = lb + (1.0 - lb) * _sigmoid(fz)
        kk = 1.0 - f
        i_bf = iz.astype(BF16)
        b = jnp.dot(lower_f, jnp.log(f), precision=HIGHEST, preferred_element_type=F32)
        b_last = b[C - 1:C, :]
        starts = [jnp.zeros((1, LANE), F32)] + [b[I * sb - 1:I * sb, :] for I in range(1, nsb)]
        lasts = [b[(I + 1) * sb - 1:(I + 1) * sb, :] for I in range(nsb)]
        blk = lambda a, I: a[I * sb:(I + 1) * sb, :]
        kd = [blk(kk, J) * jnp.exp(lasts[J] - blk(b, J)) for J in range(nsb)]
        att_rows = []
        for I in range(nsb):
            qd = blk(q, I) * jnp.exp(blk(b, I) - starts[I])
            parts = [kd[J] * jnp.exp(starts[I] - lasts[J]) for J in range(I)]
            parts.append(blk(kk, I) * jnp.exp(starts[I] - blk(b, I)))
            if I + 1 < nsb:
                parts.append(jnp.zeros(((nsb - I - 1) * sb, LANE), F32))
            kmat = jnp.concatenate(parts, axis=0) if len(parts) > 1 else parts[0]
            att_rows.append(lax.dot_general(qd.astype(BF16), kmat.astype(BF16), NT, preferred_element_type=F32))
        att = jnp.concatenate(att_rows, axis=0) if nsb > 1 else att_rows[0]
        att = jnp.where(lower, att, 0.0)
        o = jnp.dot(att.astype(BF16), i_bf, preferred_element_type=F32)
        o = o + lax.dot_general((q * jnp.exp(b)).astype(BF16), st.astype(BF16), NT, preferred_element_type=F32)
        kst = (kk * jnp.exp(b_last - b)).astype(BF16)
        st_new = st * jnp.exp(b_last) + lax.dot_general(i_bf, kst, TN, preferred_element_type=F32)
        ms = jnp.mean(o * o, axis=-1, keepdims=True)
        return o * lax.rsqrt(ms + LN_EPS) * go * _silu(gz), st_new

    def chunk(s, r0):
        qz, fz, iz, gz = (ref[pl.ds(r0, C), :] for ref in (qz_ref, fz_ref, iz_ref, gz_ref))
        lb, go = lb_ref[...], go_ref[...]
        head = lambda a, h: a[:, h * LANE:(h + 1) * LANE]
        res = [head_chunk(*(head(a, h) for a in (qz, fz, iz, gz, lb, go)), st_ref[s, h]) for h in range(hh)]
        o = jnp.concatenate([r[0] for r in res], axis=1) if hh > 1 else res[0][0]
        return o.astype(o_ref.dtype), jnp.stack([r[1] for r in res])

    if tl == C:
        res = [chunk(s, s * tl) for s in range(bb)]
        o_ref[...] = jnp.concatenate([r[0] for r in res], axis=0) if bb > 1 else res[0][0]
        st_ref[...] = jnp.stack([r[1] for r in res])
    else:
        assert bb == 1

        def body(c, carry):
            r0 = pl.multiple_of(c * C, C)
            o, st_new = chunk(0, r0)
            o_ref[pl.ds(r0, C), :] = o
            st_ref[0] = st_new
            return carry

        lax.fori_loop(0, tl // C, body, 0)

    @pl.when(t == pl.num_programs(2) - 1)
    def _():
        for s in range(bb):
            for h in range(hh):
                sn_ref[s, h] = st_ref[s, h].T


def hgrn(proj, row_start, n_seq, seq_len, lb, g_o, s0, H):
    C = min(HG_CHUNK, seq_len)
    sb = min(HG_SUB, C)
    assert seq_len % C == 0 and C % sb == 0
    if seq_len >= 512:
        tl, bb, hh = _tile(seq_len, (512,)), 1, _tile(H, (4, 2, 1))
    else:
        assert seq_len == C
        tl, bb, hh = seq_len, _tile(n_seq, (8, 4, 2, 1)), 1
    nt = seq_len // tl
    rows = bb * tl
    W = hh * LANE
    assert row_start % rows == 0
    r0 = row_start // rows
    col = lambda k: (lambda b, h, t: (r0 + b * nt + t, k * (H // hh) + h))
    in_specs = [pl.BlockSpec((rows, W), col(0)), pl.BlockSpec((rows, W), col(1)),
                pl.BlockSpec((rows, W), col(2)), pl.BlockSpec((rows, W), col(3)),
                pl.BlockSpec((1, W), lambda b, h, t: (0, h)),
                pl.BlockSpec((1, W), lambda b, h, t: (0, h))]
    args = [proj, proj, proj, proj, lb.reshape(1, H * LANE), g_o.reshape(1, H * LANE)]
    if s0 is not None:
        in_specs.append(pl.BlockSpec((bb, hh, LANE, LANE), lambda b, h, t: (b, h, 0, 0)))
        args.append(s0)
    return pl.pallas_call(
        functools.partial(_hgrn_kernel, C=C, sb=sb, tl=tl, bb=bb, hh=hh, has_s0=s0 is not None),
        grid=(n_seq // bb, H // hh, nt),
        in_specs=in_specs,
        out_specs=[pl.BlockSpec((rows, W), lambda b, h, t: (b * nt + t, h)),
                   pl.BlockSpec((bb, hh, LANE, LANE), lambda b, h, t: (b, h, 0, 0))],
        out_shape=[jax.ShapeDtypeStruct((n_seq * seq_len, H * LANE), BF16),
                   jax.ShapeDtypeStruct((n_seq, H, LANE, LANE), F32)],
        scratch_shapes=[pltpu.VMEM((bb, hh, LANE, LANE), F32)],
        compiler_params=_params("arbitrary", "arbitrary", "arbitrary"),
        name="hgrn",
    )(*args)


def _gmlp_kernel(uz_ref, vz_ref, w_ref, bt_ref, g_ref, b_ref, d_ref, v_ref, *, G, cs):
    n = uz_ref.shape[0]
    v = _layer_norm(_gelu(vz_ref[...]), g_ref[...], b_ref[...])
    v_ref[...] = v
    u = _gelu(uz_ref[...])
    t = _iota((n, n), 0)
    s = _iota((n, n), 1)
    keep = (s <= t) & (t // cs == s // cs)
    for g in range(G):
        w = jnp.where(keep, w_ref[g], 0.0).astype(BF16)
        vg = v[:, g * LANE:(g + 1) * LANE].astype(BF16)
        mixed = jnp.dot(w, vg, preferred_element_type=F32) + bt_ref[:, g:g + 1]
        d_ref[:, g * LANE:(g + 1) * LANE] = (u[:, g * LANE:(g + 1) * LANE] * mixed).astype(d_ref.dtype)


def gmlp(proj, row_start, n_rows, seq_len, w_s, b_s, g_v, b_v):
    G, n, _ = w_s.shape
    Dh = G * LANE
    cs = min(n, seq_len)
    assert n % cs == 0 and seq_len % cs == 0 and n_rows % n == 0 and row_start % n == 0
    if cs < n:
        w_s = jnp.tile(w_s[:, :cs, :cs], (1, n // cs, n // cs))
        b_s = jnp.tile(b_s[:, :cs], (1, n // cs))
    r0 = row_start // n
    return pl.pallas_call(
        functools.partial(_gmlp_kernel, G=G, cs=cs),
        grid=(n_rows // n,),
        in_specs=[pl.BlockSpec((n, Dh), lambda i: (r0 + i, 4)),
                  pl.BlockSpec((n, Dh), lambda i: (r0 + i, 5)),
                  pl.BlockSpec((G, n, n), lambda i: (0, 0, 0)),
                  pl.BlockSpec((n, G), lambda i: (0, 0)),
                  pl.BlockSpec((1, Dh), lambda i: (0, 0)),
                  pl.BlockSpec((1, Dh), lambda i: (0, 0))],
        out_specs=[pl.BlockSpec((n, Dh), lambda i: (i, 0)), pl.BlockSpec((n, Dh), lambda i: (i, 0))],
        out_shape=[jax.ShapeDtypeStruct((n_rows, Dh), BF16), jax.ShapeDtypeStruct((n_rows, Dh), F32)],
        compiler_params=_params("arbitrary"),
        name="gmlp",
    )(proj, proj, w_s, b_s.T, g_v.reshape(1, Dh), b_v.reshape(1, Dh))


def _router_kernel(x_ref, wt_ref, bt_ref, idx_ref, gate_ref, *, E):
    tm = x_ref.shape[0]
    per = E // N_GROUPS
    logits = lax.dot_general(wt_ref[...], x_ref[...], NT, precision=HIGHEST, preferred_element_type=F32)
    z = jnp.exp(logits - jnp.max(logits, axis=0, keepdims=True))
    probs = z / jnp.sum(z, axis=0, keepdims=True)
    sel = probs + bt_ref[...]
    io = _iota((per, tm), 0)

    def top2(sg):
        m1 = jnp.max(sg, axis=0, keepdims=True)
        i1 = jnp.min(jnp.where(sg == m1, io, per), axis=0, keepdims=True)
        rest = jnp.where(io == i1, -jnp.inf, sg)
        m2 = jnp.max(rest, axis=0, keepdims=True)
        i2 = jnp.min(jnp.where(rest == m2, io, per), axis=0, keepdims=True)
        return m1 + m2, i1, i2

    best, e1, e2 = top2(sel[0:per, :])
    for g in range(1, N_GROUPS):
        score, i1, i2 = top2(sel[g * per:(g + 1) * per, :])
        better = score > best
        best = jnp.where(better, score, best)
        e1 = jnp.where(better, i1 + g * per, e1)
        e2 = jnp.where(better, i2 + g * per, e2)
    eo = _iota((E, tm), 0)
    p1 = jnp.sum(jnp.where(eo == e1, probs, 0.0), axis=0, keepdims=True)
    p2 = jnp.sum(jnp.where(eo == e2, probs, 0.0), axis=0, keepdims=True)
    tot = p1 + p2
    pad_i = jnp.zeros((6, tm), jnp.int32)
    pad_f = jnp.zeros((6, tm), F32)
    idx_ref[...] = jnp.concatenate([e1, e2, pad_i], axis=0)
    gate_ref[...] = jnp.concatenate([p1 / tot, p2 / tot, pad_f], axis=0)


def router(x, w_router, b_router):
    T, D = x.shape
    E = w_router.shape[1]
    tm = _tile(T, (512, 256, 128))
    return pl.pallas_call(
        functools.partial(_router_kernel, E=E),
        grid=(T // tm,),
        in_specs=[pl.BlockSpec((tm, D), lambda i: (i, 0)),
                  pl.BlockSpec((E, D), lambda i: (0, 0)),
                  pl.BlockSpec((E, 1), lambda i: (0, 0))],
        out_specs=[pl.BlockSpec((8, tm), lambda i: (0, i)), pl.BlockSpec((8, tm), lambda i: (0, i))],
        out_shape=[jax.ShapeDtypeStruct((8, T), jnp.int32), jax.ShapeDtypeStruct((8, T), F32)],
        compiler_params=_params("arbitrary"),
        name="router",
    )(x, w_router.T, b_router.reshape(E, 1))


GATHER_UNROLL = 8


def _slab_copy(src_hbm, row, dst_ref, slot, sem):
    return pltpu.make_async_copy(src_hbm.at[row], dst_ref.at[slot], sem)


def _wait_slabs(src_hbm, dst_ref, sem):
    pltpu.make_async_copy(src_hbm.at[pl.ds(0, dst_ref.shape[0])], dst_ref, sem).wait()


def _gather_kernel(src_ref, x_hbm, o_ref, buf_ref, sem, *, tr):
    base = pl.program_id(0) * tr

    def start(r, carry):
        _slab_copy(x_hbm, src_ref[base + r], buf_ref, r, sem).start()
        return carry

    lax.fori_loop(0, tr, start, 0, unroll=GATHER_UNROLL)
    _wait_slabs(x_hbm, buf_ref, sem)
    o_ref[...] = _from_slabs(lambda k: buf_ref[:, k, :], buf_ref.shape[1]).astype(o_ref.dtype)


def gather_rows(x_slabs, src_rows, tr):
    R = src_rows.shape[0]
    _, S, _ = x_slabs.shape
    grid_spec = pltpu.PrefetchScalarGridSpec(
        num_scalar_prefetch=1,
        grid=(R // tr,),
        in_specs=[pl.BlockSpec(memory_space=pl.ANY)],
        out_specs=pl.BlockSpec((tr, S * LANE), lambda i, src: (i, 0)),
        scratch_shapes=[pltpu.VMEM((tr, S, LANE), F32), pltpu.SemaphoreType.DMA(())],
    )
    return pl.pallas_call(
        functools.partial(_gather_kernel, tr=tr),
        grid_spec=grid_spec,
        out_shape=jax.ShapeDtypeStruct((R, S * LANE), BF16),
        compiler_params=_params("arbitrary"),
        name="moe_gather",
    )(src_rows, x_slabs)


def _expert_kernel(te_ref, nv_ref, fresh_ref, xs_ref, wg_ref, wu_ref, wd_ref, rg_ref, y_ref,
                   wg_bf, wu_bf, wd_bf, acc_ref):
    i = pl.program_id(0)
    j = pl.program_id(1)
    valid = i < nv_ref[0]
    last = j == pl.num_programs(1) - 1

    @pl.when(valid & (fresh_ref[i] == 1))
    def _():
        wg_bf[j] = wg_ref[...].astype(BF16)
        wu_bf[j] = wu_ref[...].astype(BF16)
        wd_bf[j] = wd_ref[...].astype(BF16)

    @pl.when(valid)
    def _():
        x = xs_ref[...]
        hg = jnp.dot(x, wg_bf[j], preferred_element_type=F32)
        hu = jnp.dot(x, wu_bf[j], preferred_element_type=F32)
        h = (_silu(hg) * hu).astype(BF16)
        part = jnp.dot(h, wd_bf[j], preferred_element_type=F32)

        @pl.when(j == 0)
        def _():
            acc_ref[...] = part

        @pl.when(j > 0)
        def _():
            acc_ref[...] = acc_ref[...] + part

        @pl.when(last)
        def _():
            _to_slabs(y_ref, acc_ref[...] * rg_ref[...])

    @pl.when(jnp.logical_not(valid) & last)
    def _():
        y_ref[...] = jnp.zeros_like(y_ref)


def expert_ffn(xs, tile_expert, n_valid, fresh, row_gate, w_gate, w_up, w_down, layer, tr):
    R, D = xs.shape
    S = D // LANE
    De = w_gate.shape[-1]
    te = _tile(De, (512, 256, 128))
    nj = De // te
    chunk = lambda i, j, fr: jnp.where(fr[i] == 1, j, nj - 1)
    grid_spec = pltpu.PrefetchScalarGridSpec(
        num_scalar_prefetch=3,
        grid=(R // tr, nj),
        in_specs=[pl.BlockSpec((tr, D), lambda i, j, e, nv, fr: (i, 0)),
                  pl.BlockSpec((None, None, D, te), lambda i, j, e, nv, fr: (layer, e[i], 0, chunk(i, j, fr))),
                  pl.BlockSpec((None, None, D, te), lambda i, j, e, nv, fr: (layer, e[i], 0, chunk(i, j, fr))),
                  pl.BlockSpec((None, None, te, D), lambda i, j, e, nv, fr: (layer, e[i], chunk(i, j, fr), 0)),
                  pl.BlockSpec((tr, 1), lambda i, j, e, nv, fr: (i, 0))],
        out_specs=pl.BlockSpec((tr, S, LANE), lambda i, j, e, nv, fr: (i, 0, 0)),
        scratch_shapes=[pltpu.VMEM((nj, D, te), BF16), pltpu.VMEM((nj, D, te), BF16),
                        pltpu.VMEM((nj, te, D), BF16), pltpu.VMEM((tr, D), F32)],
    )
    return pl.pallas_call(
        _expert_kernel,
        grid_spec=grid_spec,
        out_shape=jax.ShapeDtypeStruct((R, S, LANE), F32),
        compiler_params=_params("arbitrary", "arbitrary"),
        name="moe_experts",
    )(tile_expert, n_valid, fresh, xs, w_gate, w_up, w_down, row_gate)


def _combine_ln_kernel(pos_ref, y_hbm, res_ref, g_ref, b_ref, o_ref, obf_ref, buf0_ref, buf1_ref, sem, *, tm, alpha):
    base = pl.program_id(0) * tm

    def start(r, carry):
        _slab_copy(y_hbm, pos_ref[2 * (base + r)], buf0_ref, r, sem).start()
        _slab_copy(y_hbm, pos_ref[2 * (base + r) + 1], buf1_ref, r, sem).start()
        return carry

    lax.fori_loop(0, tm, start, 0, unroll=GATHER_UNROLL)
    _wait_slabs(y_hbm, buf0_ref, sem)
    _wait_slabs(y_hbm, buf1_ref, sem)
    moe = _from_slabs(lambda k: buf0_ref[:, k, :] + buf1_ref[:, k, :], buf0_ref.shape[1])
    z = _layer_norm(alpha * res_ref[...] + moe, g_ref[...], b_ref[...])
    o_ref[...] = z
    obf_ref[...] = z.astype(BF16)


def combine_ln(y_slabs, pos, res, g, b, alpha):
    T, D = res.shape
    S = D // LANE
    tm = _tile(T, (256, 128))
    grid_spec = pltpu.PrefetchScalarGridSpec(
        num_scalar_prefetch=1,
        grid=(T // tm,),
        in_specs=[pl.BlockSpec(memory_space=pl.ANY),
                  pl.BlockSpec((tm, D), lambda i, pos: (i, 0)),
                  pl.BlockSpec((1, D), lambda i, pos: (0, 0)),
                  pl.BlockSpec((1, D), lambda i, pos: (0, 0))],
        out_specs=[pl.BlockSpec((tm, D), lambda i, pos: (i, 0)), pl.BlockSpec((tm, D), lambda i, pos: (i, 0))],
        scratch_shapes=[pltpu.VMEM((tm, S, LANE), F32), pltpu.VMEM((tm, S, LANE), F32),
                        pltpu.SemaphoreType.DMA(())],
    )
    return pl.pallas_call(
        functools.partial(_combine_ln_kernel, tm=tm, alpha=alpha),
        grid_spec=grid_spec,
        out_shape=[jax.ShapeDtypeStruct((T, D), F32), jax.ShapeDtypeStruct((T, D), BF16)],
        compiler_params=_params("arbitrary"),
        name="moe_combine_ln",
    )(pos, y_slabs, res, g.reshape(1, D), b.reshape(1, D))


def _dispatch_plan(e_idx, gates, E, tr):
    T = e_idx.shape[1]
    pairs = 2 * T
    e_flat = e_idx[0:2, :].T.reshape(pairs)
    g_flat = gates[0:2, :].T.reshape(pairs)
    order = jnp.argsort(e_flat, stable=True).astype(jnp.int32)
    counts = jnp.zeros((E,), jnp.int32).at[e_flat].add(1)
    padded = ((counts + tr - 1) // tr) * tr
    start = jnp.cumsum(counts) - counts
    pstart = jnp.cumsum(padded) - padded
    n_tiles = pairs // tr + E
    R = n_tiles * tr
    rows = jnp.arange(R, dtype=jnp.int32)
    pend = pstart + padded
    row_e = jnp.minimum(jnp.sum((rows[:, None] >= pend[None, :]).astype(jnp.int32), axis=1), E - 1)
    off = rows - pstart[row_e]
    valid = off < counts[row_e]
    src_pair = order[jnp.clip(start[row_e] + off, 0, pairs - 1)]
    src_tok = jnp.where(valid, src_pair // 2, 0).astype(jnp.int32)
    row_gate = jnp.where(valid, g_flat[src_pair], 0.0).reshape(R, 1)
    rank = jnp.argsort(order).astype(jnp.int32)
    pos = (pstart[e_flat] + rank - start[e_flat]).astype(jnp.int32)
    n_valid = (jnp.sum(padded) // tr).astype(jnp.int32).reshape(1)
    tile_e = row_e[::tr]
    last_e = tile_e[jnp.maximum(n_valid[0] - 1, 0)]
    tile_e = jnp.where(jnp.arange(n_tiles) < n_valid[0], tile_e, last_e).astype(jnp.int32)
    fresh = jnp.concatenate([jnp.ones((1,), jnp.int32), (tile_e[1:] != tile_e[:-1]).astype(jnp.int32)])
    return src_tok, row_gate, pos, tile_e, n_valid, fresh


def moe_ln(x, x_slabs, x_res_scale, w_router, b_router, w_gate, w_up, w_down, layer, g, b):
    T, D = x.shape
    E = w_router.shape[1]
    tr = _tile(2 * T, (256, 128))
    e_idx, gates = router(x, w_router, b_router)
    src_tok, row_gate, pos, tile_e, n_valid, fresh = _dispatch_plan(e_idx, gates, E, tr)
    xs = gather_rows(x_slabs, src_tok, tr)
    ys = expert_ffn(xs, tile_e, n_valid, fresh, row_gate, w_gate, w_up, w_down, layer, tr)
    return combine_ln(ys, pos, x, g, b, x_res_scale)


def kernel(x_prompt, x_sample, cache_k, cache_v, cache_logf, state_conv, state_hgrn, page_table, w_in_even, b_fgate, w_dw, b_dw, g_cnorm, b_cnorm, w_out_even, w_in_odd, lb_logits, g_onorm, g_vnorm, b_vnorm, w_sgu, b_sgu, w_out_odd, ln1_g, ln1_b, ln2_g, ln2_b, w_router, b_router, w_gate, w_up, w_down):
    Bp, Lp, D = x_prompt.shape
    Bs, Ls, _ = x_sample.shape
    Dh = D // 2
    H = b_fgate.shape[1]
    HG = state_hgrn.shape[2]
    assert Dh == H * LANE and Dh == HG * LANE and Dh == w_sgu.shape[1] * LANE
    depth = ln1_g.shape[0]
    alpha = (2 * depth) ** 0.25
    Tp, Ts = Bp * Lp, Bs * Ls
    n_even = cache_k.shape[0]
    n_phys, page = cache_k.shape[1], cache_k.shape[2]

    lb_p = jax.nn.softmax(lb_logits.astype(F32), axis=0)
    lb_all = jnp.cumsum(lb_p, axis=0) - lb_p[0]

    x = jnp.concatenate([x_prompt.reshape(Tp, D), x_sample.reshape(Ts, D)], axis=0)
    x_bf = x.astype(BF16)
    rin, tot = page_suffix(cache_logf.astype(F32).reshape(n_even * n_phys, page * H), H)
    rin = rin.reshape(n_even * n_phys, 1, page * H)
    tot = tot.reshape(n_even * n_phys, 1, page * H)

    out = {k: [] for k in ("kp", "vp", "lfp", "convp", "hgp", "ks", "vs", "lfs", "convs", "hgs", "mlpv")}
    for l in range(depth):
        j = l // 2
        if l % 2 == 0:
            w_in = w_in_even[j]
            ag = matmul_cols(x_bf, w_in, 0, 2 * Dh)
            q = matmul_cols(x_bf, w_in, 2 * Dh, Dh)
            k = matmul_cols(x_bf, w_in, 3 * Dh, Dh)
            v = matmul_cols(x_bf, w_in, 4 * Dh, Dh)
            lf, _, c, ct = logf_project(x_bf, w_in[:, 5 * Dh:], b_fgate[j], Lp)
            conv_args = (w_dw[j], b_dw[j], g_cnorm[j], b_cnorm[j])
            ca_p, cst_p = conv_prompt(ag, Bp, Lp, *conv_args)
            ca_s, cst_s = conv_sample(ag, Tp, Bs, Ls, state_conv[j], *conv_args)
            att_p = fox_prompt(q, k, v, c, ct, Bp, Lp, H)
            att_s = fox_sample(q, k, v, lf, Tp, Bs, Ls, cache_k, cache_v, rin, tot, j, page_table, H)
            h1 = jnp.concatenate([ca_p, ca_s], axis=0)
            h2 = jnp.concatenate([att_p, att_s.astype(BF16)], axis=0)
            w_out = w_out_even[j]
            out["kp"].append(k[:Tp].reshape(Bp, Lp, H, LANE))
            out["vp"].append(v[:Tp].reshape(Bp, Lp, H, LANE))
            out["lfp"].append(lf[:Tp].reshape(Bp, Lp, H))
            out["convp"].append(cst_p)
            out["ks"].append(k[Tp:].reshape(Bs, Ls, H, LANE))
            out["vs"].append(v[Tp:].reshape(Bs, Ls, H, LANE))
            out["lfs"].append(lf[Tp:].reshape(Bs, Ls, H))
            out["convs"].append(cst_s)
        else:
            proj = matmul_cols(x_bf, w_in_odd[j], 0, 6 * Dh)
            o_p, s_p = hgrn(proj, 0, Bp, Lp, lb_all[l], g_onorm[j], None, HG)
            o_s, s_s = hgrn(proj, Tp, Bs, Ls, lb_all[l], g_onorm[j], state_hgrn[j].astype(F32), HG)
            mlp_args = (w_sgu[j], b_sgu[j], g_vnorm[j], b_vnorm[j])
            d_p, _ = gmlp(proj, 0, Tp, Lp, *mlp_args)
            d_s, v_s = gmlp(proj, Tp, Ts, Ls, *mlp_args)
            h1 = jnp.concatenate([o_p, o_s], axis=0)
            h2 = jnp.concatenate([d_p, d_s], axis=0)
            w_out = w_out_odd[j]
            out["hgp"].append(s_p)
            out["hgs"].append(s_s)
            out["mlpv"].append(v_s.reshape(Bs, Ls, Dh))
        x, x_slabs = outproj_ln(h1, h2, w_out.astype(BF16), x, ln1_g[l], ln1_b[l], alpha)
        x, x_bf = moe_ln(x, x_slabs, alpha, w_router, b_router, w_gate, w_up, w_down, l, ln2_g[l], ln2_b[l])

    stack = lambda name: jnp.stack(out[name])
    return (x[:Tp].reshape(Bp, Lp, D), x[Tp:].reshape(Bs, Ls, D),
            stack("kp"), stack("vp"), stack("lfp"), stack("convp"), stack("hgp"),
            stack("ks"), stack("vs"), stack("lfs"), stack("convs"), stack("hgs"), stack("mlpv"))
```

```python
import functools
import math

import jax
import jax.numpy as jnp
from jax import lax
from jax.experimental import pallas as pl
from jax.experimental.pallas import tpu as pltpu

F32 = jnp.float32
BF16 = jnp.bfloat16
HIGHEST = lax.Precision.HIGHEST

LANE = 128
VMEM_LIMIT = 56 * 1024 * 1024
LN_EPS = 1e-5
HG_CHUNK = 64
HG_SUB = 16
CONV_HALO = 32
N_GROUPS = 4
NT = (((1,), (1,)), ((), ()))
TN = (((0,), (0,)), ((), ()))


def _params(*sem):
    return pltpu.CompilerParams(dimension_semantics=sem, vmem_limit_bytes=VMEM_LIMIT)


def _tile(n, prefs):
    for t in prefs:
        if n % t == 0:
            return t
    raise ValueError(f"no tile in {prefs} divides {n}")


def _sigmoid(x):
    return 1.0 / (1.0 + jnp.exp(-x))


def _silu(x):
    return x * _sigmoid(x)


def _gelu(x):
    return 0.5 * x * (1.0 + jnp.tanh(0.7978845608028654 * (x + 0.044715 * (x * x * x))))


def _log_sigmoid(x):
    return -(jnp.maximum(-x, 0.0) + jnp.log1p(jnp.exp(-jnp.abs(x))))


def _layer_norm(x, g, b):
    mu = jnp.mean(x, axis=-1, keepdims=True)
    xc = x - mu
    var = jnp.mean(xc * xc, axis=-1, keepdims=True)
    return xc * lax.rsqrt(var + LN_EPS) * g + b


def _iota(shape, dim):
    return lax.broadcasted_iota(jnp.int32, shape, dim)


def _matmul_kernel(x_ref, w_ref, o_ref, wbf_ref):
    @pl.when(pl.program_id(1) == 0)
    def _():
        wbf_ref[...] = w_ref[...].astype(BF16)

    o_ref[...] = jnp.dot(x_ref[...], wbf_ref[...], preferred_element_type=F32).astype(o_ref.dtype)


def _matmul_split_kernel(x_ref, w_ref, o1_ref, o2_ref, wbf_ref, *, n_first):
    i = pl.program_id(1)

    @pl.when(i == 0)
    def _():
        wbf_ref[...] = w_ref[...].astype(BF16)

    y = jnp.dot(x_ref[...], wbf_ref[...], preferred_element_type=F32)

    @pl.when(i < n_first)
    def _():
        o1_ref[...] = y

    @pl.when(i >= n_first)
    def _():
        o2_ref[...] = y


def matmul_cols(x_bf, w, col_start, n_cols, split_rows=None):
    M, K = x_bf.shape
    tm = _tile(M if split_rows is None else math.gcd(split_rows, M - split_rows), (1024, 512, 256, 128))
    tn = _tile(n_cols, (512, 256, 128))
    assert col_start % tn == 0 and M % tm == 0
    off = col_start // tn
    common = dict(
        grid=(n_cols // tn, M // tm),
        in_specs=[pl.BlockSpec((tm, K), lambda j, i: (i, 0)),
                  pl.BlockSpec((K, tn), lambda j, i: (0, j + off))],
        scratch_shapes=[pltpu.VMEM((K, tn), BF16)],
        compiler_params=_params("arbitrary", "arbitrary"),
    )
    if split_rows is None:
        return pl.pallas_call(
            _matmul_kernel,
            out_specs=pl.BlockSpec((tm, tn), lambda j, i: (i, j)),
            out_shape=jax.ShapeDtypeStruct((M, n_cols), F32),
            name="matmul_cols", **common,
        )(x_bf, w)
    n1 = split_rows // tm
    assert (M - split_rows) % tm == 0 and 0 < n1 < M // tm
    return pl.pallas_call(
        functools.partial(_matmul_split_kernel, n_first=n1),
        out_specs=[pl.BlockSpec((tm, tn), lambda j, i: (jnp.minimum(i, n1 - 1), j)),
                   pl.BlockSpec((tm, tn), lambda j, i: (jnp.maximum(i - n1, 0), j))],
        out_shape=[jax.ShapeDtypeStruct((split_rows, n_cols), F32),
                   jax.ShapeDtypeStruct((M - split_rows, n_cols), F32)],
        name="matmul_cols_split", **common,
    )(x_bf, w)


def _logf_kernel(x_ref, w_ref, wt_ref, b_ref, bt_ref, lf_ref, lft_ref, c_ref, ct_ref,
                 carry_ref, carryt_ref, *, tiles_per_seq):
    i = pl.program_id(0)
    x = x_ref[...]
    tm = x.shape[0]
    fz = jnp.dot(x, w_ref[...].astype(BF16), preferred_element_type=F32) + b_ref[...]
    fzt = lax.dot_general(wt_ref[...].astype(BF16), x, NT, preferred_element_type=F32) + bt_ref[...]
    lf = _log_sigmoid(fz)
    lft = _log_sigmoid(fzt)
    lf_ref[...] = lf
    lft_ref[...] = lft

    @pl.when(i % tiles_per_seq == 0)
    def _():
        carry_ref[...] = jnp.zeros_like(carry_ref)
        carryt_ref[...] = jnp.zeros_like(carryt_ref)

    row = _iota((tm, tm), 0)
    col = _iota((tm, tm), 1)
    lower = (col <= row).astype(F32)
    upper = (row <= col).astype(F32)
    c = jnp.dot(lower, lf, precision=HIGHEST, preferred_element_type=F32) + carry_ref[...]
    ct = jnp.dot(lft, upper, precision=HIGHEST, preferred_element_type=F32) + carryt_ref[...]
    c_ref[...] = c
    ct_ref[...] = ct
    carry_ref[...] = c[tm - 1:tm, :]
    carryt_ref[...] = ct[:, tm - 1:tm]


def logf_project(x_bf, w_f, b_f, seq_len):
    T, D = x_bf.shape
    H = w_f.shape[1]
    tm = _tile(seq_len, (512, 256, 128))
    assert T % tm == 0
    outs = pl.pallas_call(
        functools.partial(_logf_kernel, tiles_per_seq=seq_len // tm),
        grid=(T // tm,),
        in_specs=[pl.BlockSpec((tm, D), lambda i: (i, 0)),
                  pl.BlockSpec((D, H), lambda i: (0, 0)),
                  pl.BlockSpec((H, D), lambda i: (0, 0)),
                  pl.BlockSpec((1, H), lambda i: (0, 0)),
                  pl.BlockSpec((H, 1), lambda i: (0, 0))],
        out_specs=[pl.BlockSpec((tm, H), lambda i: (i, 0)),
                   pl.BlockSpec((H, tm), lambda i: (0, i)),
                   pl.BlockSpec((tm, H), lambda i: (i, 0)),
                   pl.BlockSpec((H, tm), lambda i: (0, i))],
        out_shape=[jax.ShapeDtypeStruct((T, H), F32), jax.ShapeDtypeStruct((H, T), F32),
                   jax.ShapeDtypeStruct((T, H), F32), jax.ShapeDtypeStruct((H, T), F32)],
        scratch_shapes=[pltpu.VMEM((1, H), F32), pltpu.VMEM((H, 1), F32)],
        compiler_params=_params("arbitrary"),
        name="logf_project",
    )(x_bf, w_f, w_f.T, b_f.reshape(1, H), b_f.reshape(H, 1))
    return outs


def _conv_tail(acc, bdw, g, b):
    return _silu(_layer_norm(acc + bdw, g, b))


def _conv_prompt_kernel(ag_ref, wdw_ref, bdw_ref, g_ref, b_ref, y_ref, st_ref, ext_ref, *, W, tt, C):
    t = pl.program_id(1)

    @pl.when(t == 0)
    def _():
        ext_ref[0:CONV_HALO, :] = jnp.zeros((CONV_HALO, C), F32)

    @pl.when(t > 0)
    def _():
        ext_ref[0:CONV_HALO, :] = ext_ref[tt:tt + CONV_HALO, :]

    u = ag_ref[:, 0:C] * _sigmoid(ag_ref[:, C:2 * C])
    ext_ref[CONV_HALO:CONV_HALO + tt, :] = u
    base = CONV_HALO - (W - 1)
    acc = ext_ref[base:base + tt, :] * wdw_ref[0:1, :]
    for w in range(1, W):
        acc = acc + ext_ref[base + w:base + w + tt, :] * wdw_ref[w:w + 1, :]
    y_ref[...] = _conv_tail(acc, bdw_ref[...], g_ref[...], b_ref[...]).astype(y_ref.dtype)

    @pl.when(t == pl.num_programs(1) - 1)
    def _():
        st_ref[...] = ext_ref[CONV_HALO + tt - (W - 1):CONV_HALO + tt, :]


def conv_prompt(ag, n_seq, seq_len, w_dw, b_dw, g_n, b_n):
    W, C = w_dw.shape
    assert W - 1 <= CONV_HALO
    tt = _tile(seq_len, (256, 128))
    nt = seq_len // tt
    vec = lambda a: a.reshape(1, C)
    return pl.pallas_call(
        functools.partial(_conv_prompt_kernel, W=W, tt=tt, C=C),
        grid=(n_seq, nt),
        in_specs=[pl.BlockSpec((tt, 2 * C), lambda b, t: (b * nt + t, 0)),
                  pl.BlockSpec((W, C), lambda b, t: (0, 0)),
                  pl.BlockSpec((1, C), lambda b, t: (0, 0)),
                  pl.BlockSpec((1, C), lambda b, t: (0, 0)),
                  pl.BlockSpec((1, C), lambda b, t: (0, 0))],
        out_specs=[pl.BlockSpec((tt, C), lambda b, t: (b * nt + t, 0)),
                   pl.BlockSpec((None, W - 1, C), lambda b, t: (b, 0, 0))],
        out_shape=[jax.ShapeDtypeStruct((n_seq * seq_len, C), BF16),
                   jax.ShapeDtypeStruct((n_seq, W - 1, C), F32)],
        scratch_shapes=[pltpu.VMEM((CONV_HALO + tt, C), F32)],
        compiler_params=_params("arbitrary", "arbitrary"),
        name="conv_prompt",
    )(ag, w_dw, vec(b_dw), vec(g_n), vec(b_n))


def _conv_sample_kernel(ag_ref, st_ref, wdw_ref, bdw_ref, g_ref, b_ref, y_ref, nst_ref, ext_ref, *, W, Ls, C, bb):
    u = ag_ref[:, 0:C] * _sigmoid(ag_ref[:, C:2 * C])
    ext_ref[:, 0:W - 1, :] = st_ref[...]
    ext_ref[:, W - 1:W - 1 + Ls, :] = u.reshape(bb, Ls, C)
    acc = ext_ref[:, 0:Ls, :] * wdw_ref[0:1, :]
    for w in range(1, W):
        acc = acc + ext_ref[:, w:w + Ls, :] * wdw_ref[w:w + 1, :]
    y = _conv_tail(acc.reshape(bb * Ls, C), bdw_ref[...], g_ref[...], b_ref[...])
    y_ref[...] = y.astype(y_ref.dtype)
    nst_ref[...] = ext_ref[:, Ls:Ls + W - 1, :]


def conv_sample(ag, row_start, n_seq, Ls, state, w_dw, b_dw, g_n, b_n):
    W, C = w_dw.shape
    bb = _tile(n_seq, (8, 4, 2, 1))
    rows = bb * Ls
    assert Ls % 8 == 0 and row_start % rows == 0
    r0 = row_start // rows
    vec = lambda a: a.reshape(1, C)
    return pl.pallas_call(
        functools.partial(_conv_sample_kernel, W=W, Ls=Ls, C=C, bb=bb),
        grid=(n_seq // bb,),
        in_specs=[pl.BlockSpec((rows, 2 * C), lambda i: (r0 + i, 0)),
                  pl.BlockSpec((bb, W - 1, C), lambda i: (i, 0, 0)),
                  pl.BlockSpec((W, C), lambda i: (0, 0)),
                  pl.BlockSpec((1, C), lambda i: (0, 0)),
                  pl.BlockSpec((1, C), lambda i: (0, 0)),
                  pl.BlockSpec((1, C), lambda i: (0, 0))],
        out_specs=[pl.BlockSpec((rows, C), lambda i: (i, 0)),
                   pl.BlockSpec((bb, W - 1, C), lambda i: (i, 0, 0))],
        out_shape=[jax.ShapeDtypeStruct((n_seq * Ls, C), BF16),
                   jax.ShapeDtypeStruct((n_seq, W - 1, C), F32)],
        scratch_shapes=[pltpu.VMEM((bb, W - 1 + Ls, C), F32)],
        compiler_params=_params("arbitrary"),
        name="conv_sample",
    )(ag, state, w_dw, vec(b_dw), vec(g_n), vec(b_n))


def _fox_prompt_kernel(q_ref, k_ref, v_ref, c_ref, ct_ref, o_ref, *, tq, scale, H):
    h = pl.program_id(1)
    qi = pl.program_id(2)
    q = q_ref[...].astype(BF16)
    cq = jnp.sum(jnp.where(_iota((tq, H), 1) == h, c_ref[...], 0.0), axis=1, keepdims=True)
    row = qi * tq + _iota((tq, tq), 0)
    col0 = _iota((tq, tq), 1)

    def body(ki, carry):
        m, l, acc = carry
        start = pl.multiple_of(ki * tq, tq)
        k = k_ref[pl.ds(start, tq), :].astype(BF16)
        v = v_ref[pl.ds(start, tq), :].astype(BF16)
        ck = ct_ref[pl.ds(h, 1), pl.ds(start, tq)]
        s = lax.dot_general(q, k, NT, preferred_element_type=F32) * scale + (cq - ck)
        s = jnp.where(row >= start + col0, s, -jnp.inf)
        m_new = jnp.maximum(m, jnp.max(s, axis=1, keepdims=True))
        alpha = jnp.exp(m - m_new)
        p = jnp.exp(s - m_new)
        l = alpha * l + jnp.sum(p, axis=1, keepdims=True)
        acc = alpha * acc + jnp.dot(p.astype(BF16), v, preferred_element_type=F32)
        return m_new, l, acc

    init = (jnp.full((tq, 1), -jnp.inf, F32), jnp.zeros((tq, 1), F32), jnp.zeros((tq, LANE), F32))
    _, l, acc = lax.fori_loop(0, qi + 1, body, init)
    o_ref[...] = (acc / l).astype(o_ref.dtype)


def fox_prompt(q, k, v, c, ct, n_seq, seq_len, H):
    tq = _tile(seq_len, (512, 256, 128))
    nq = seq_len // tq
    return pl.pallas_call(
        functools.partial(_fox_prompt_kernel, tq=tq, scale=LANE ** -0.5, H=H),
        grid=(n_seq, H, nq),
        in_specs=[pl.BlockSpec((tq, LANE), lambda b, h, i: (b * nq + i, h)),
                  pl.BlockSpec((seq_len, LANE), lambda b, h, i: (b, h)),
                  pl.BlockSpec((seq_len, LANE), lambda b, h, i: (b, h)),
                  pl.BlockSpec((tq, H), lambda b, h, i: (b * nq + i, 0)),
                  pl.BlockSpec((H, seq_len), lambda b, h, i: (0, b))],
        out_specs=pl.BlockSpec((tq, LANE), lambda b, h, i: (b * nq + i, h)),
        out_shape=jax.ShapeDtypeStruct((n_seq * seq_len, H * LANE), BF16),
        compiler_params=_params("arbitrary", "arbitrary", "arbitrary"),
        name="fox_prompt",
    )(q, k, v, c, ct)


def _split3_dot(x, w_bf):
    x1 = x.astype(BF16)
    r1 = x - x1.astype(F32)
    x2 = r1.astype(BF16)
    x3 = (r1 - x2.astype(F32)).astype(BF16)
    dot = lambda a: jnp.dot(a, w_bf, preferred_element_type=F32)
    return dot(x1) + dot(x2) + dot(x3)


def _page_suffix_kernel(lf_ref, rin_ref, tot_ref, later_ref, same_ref, *, H):
    n = lf_ref.shape[1]

    @pl.when(pl.program_id(0) == 0)
    def _():
        r = _iota((n, n), 0)
        c = _iota((n, n), 1)
        same = (r % H) == (c % H)
        later_ref[...] = jnp.where(same & (r > c), 1.0, 0.0).astype(BF16)
        same_ref[...] = jnp.where(same, 1.0, 0.0).astype(BF16)

    x = lf_ref[...]
    rin_ref[...] = _split3_dot(x, later_ref[...])
    tot_ref[...] = _split3_dot(x, same_ref[...])


def page_suffix(lf_pages, H):
    P, n = lf_pages.shape
    tp = _tile(P, (256, 128, 64, 32, 16, 8))
    return pl.pallas_call(
        functools.partial(_page_suffix_kernel, H=H),
        grid=(P // tp,),
        in_specs=[pl.BlockSpec((tp, n), lambda i: (i, 0))],
        out_specs=[pl.BlockSpec((tp, n), lambda i: (i, 0)), pl.BlockSpec((tp, n), lambda i: (i, 0))],
        out_shape=[jax.ShapeDtypeStruct((P, n), F32), jax.ShapeDtypeStruct((P, n), F32)],
        scratch_shapes=[pltpu.VMEM((n, n), BF16), pltpu.VMEM((n, n), BF16)],
        compiler_params=_params("arbitrary"),
        name="page_suffix",
    )(lf_pages)


def _fox_sample_kernel(*refs, H, Ls, scale, pps):
    pt_ref, q_ref, kn_ref, vn_ref, lfn_ref = refs[0:5]
    ck_refs = refs[5:5 + pps]
    cv_refs = refs[5 + pps:5 + 2 * pps]
    rin_refs = refs[5 + 2 * pps:5 + 3 * pps]
    tot_refs = refs[5 + 3 * pps:5 + 4 * pps]
    o_ref, qh_ref, a_ref, m_ref, l_ref, acc_ref, rc_ref = refs[5 + 4 * pps:]
    p = pl.program_id(1)
    HQ = H * Ls
    cols = ck_refs[0].shape[0] * H
    sel = (_iota((HQ, H), 0) // Ls == _iota((HQ, H), 1)).astype(F32)
    causal = _iota((HQ, Ls), 1) <= _iota((HQ, Ls), 0) % Ls

    def new_logf():
        return lax.dot_general(sel, lfn_ref[...], NT, precision=HIGHEST, preferred_element_type=F32)

    @pl.when(p == 0)
    def _():
        q = q_ref[...]
        qh_ref[...] = jnp.concatenate([q[:, h * LANE:(h + 1) * LANE] for h in range(H)], axis=0).astype(BF16)
        a_ref[...] = jnp.sum(jnp.where(causal, new_logf(), 0.0), axis=1, keepdims=True)
        m_ref[...] = jnp.full((HQ, 1), -jnp.inf, F32)
        l_ref[...] = jnp.zeros((HQ, 1), F32)
        acc_ref[...] = jnp.zeros((HQ, LANE), F32)
        rc_ref[...] = jnp.zeros((1, cols), F32)

    def update(scores, pvs):
        m_old = m_ref[...]
        m_new = m_old
        for s in scores:
            m_new = jnp.maximum(m_new, jnp.max(s, axis=1, keepdims=True))
        alpha = jnp.exp(m_old - m_new)
        l_new = alpha * l_ref[...]
        acc = alpha * acc_ref[...]
        for s, pv in zip(scores, pvs):
            pr = jnp.exp(s - m_new)
            l_new = l_new + jnp.sum(pr, axis=1, keepdims=True)
            acc = acc + pv(pr.astype(BF16))
        l_ref[...] = l_new
        acc_ref[...] = acc
        m_ref[...] = m_new

    qh = qh_ref[...]
    own_head = (_iota((HQ, cols), 0) // Ls) == (_iota((HQ, cols), 1) % H)
    rc = rc_ref[...]
    scores, pvs = [], []
    for s_ in range(pps):
        kx = ck_refs[s_][...].reshape(cols, LANE).astype(BF16)
        vx = cv_refs[s_][...].reshape(cols, LANE).astype(BF16)
        bias = a_ref[...] + (rin_refs[s_][...] + rc)
        rc = rc + tot_refs[s_][...]
        sc = lax.dot_general(qh, kx, NT, preferred_element_type=F32) * scale + bias
        scores.append(jnp.where(own_head, sc, -jnp.inf))
        pvs.append(lambda pr, vx=vx: jnp.dot(pr, vx, preferred_element_type=F32))
    rc_ref[...] = rc
    update(scores, pvs)

    @pl.when(p == pl.num_programs(1) - 1)
    def _():
        head = lambda a, h: a[:, h * LANE:(h + 1) * LANE].astype(BF16)
        rows = lambda a, h: a[h * Ls:(h + 1) * Ls, :]
        kn = kn_ref[...]
        vn = vn_ref[...]
        s2 = jnp.concatenate([lax.dot_general(rows(qh, h), head(kn, h), NT, preferred_element_type=F32)
                              for h in range(H)], axis=0)
        upto = (_iota((Ls, Ls), 0) <= _iota((Ls, Ls), 1)).astype(F32)
        cum = jnp.dot(new_logf(), upto, precision=HIGHEST, preferred_element_type=F32)
        s2 = jnp.where(causal, s2 * scale + (a_ref[...] - cum), -jnp.inf)
        update([s2], [lambda pr: jnp.concatenate(
            [jnp.dot(rows(pr, h), head(vn, h), preferred_element_type=F32) for h in range(H)], axis=0)])
        out = acc_ref[...] / l_ref[...]
        for h in range(H):
            o_ref[:, h * LANE:(h + 1) * LANE] = rows(out, h)


def fox_sample(q, k, v, lf, row_start, n_seq, Ls, cache_k, cache_v, rin, tot, layer, page_table, H):
    D = H * LANE
    n_pages = page_table.shape[1]
    n_phys, page = cache_k.shape[1], cache_k.shape[2]
    pps = _tile(n_pages, (8, 4, 2, 1))
    assert Ls % 8 == 0 and row_start % Ls == 0
    r0 = row_start // Ls
    HQ = H * Ls
    rows = lambda b, p, pt: (r0 + b, 0)

    def phys(s_):
        return lambda b, p, pt: pt[b * n_pages + (n_pages - 1 - (p * pps + s_))]

    kv_spec = lambda s_: pl.BlockSpec((None, None, page, H, LANE),
                                      lambda b, p, pt: (layer, phys(s_)(b, p, pt), 0, 0, 0))
    row_spec = lambda s_: pl.BlockSpec((None, 1, page * H),
                                       lambda b, p, pt: (layer * n_phys + phys(s_)(b, p, pt), 0, 0))
    grid_spec = pltpu.PrefetchScalarGridSpec(
        num_scalar_prefetch=1,
        grid=(n_seq, n_pages // pps),
        in_specs=([pl.BlockSpec((Ls, D), rows), pl.BlockSpec((Ls, D), lambda b, p, pt: (b, 0)),
                   pl.BlockSpec((Ls, D), lambda b, p, pt: (b, 0)), pl.BlockSpec((Ls, H), rows)]
                  + [kv_spec(s_) for s_ in range(pps)] + [kv_spec(s_) for s_ in range(pps)]
                  + [row_spec(s_) for s_ in range(pps)] + [row_spec(s_) for s_ in range(pps)]),
        out_specs=pl.BlockSpec((Ls, D), lambda b, p, pt: (b, 0)),
        scratch_shapes=[pltpu.VMEM((HQ, LANE), BF16), pltpu.VMEM((HQ, 1), F32), pltpu.VMEM((HQ, 1), F32),
                        pltpu.VMEM((HQ, 1), F32), pltpu.VMEM((HQ, LANE), F32), pltpu.VMEM((1, page * H), F32)],
    )
    return pl.pallas_call(
        functools.partial(_fox_sample_kernel, H=H, Ls=Ls, scale=LANE ** -0.5, pps=pps),
        grid_spec=grid_spec,
        out_shape=jax.ShapeDtypeStruct((n_seq * Ls, D), F32),
        compiler_params=_params("arbitrary", "arbitrary"),
        name="fox_sample",
    )(page_table.reshape(-1), q, k, v, lf, *([cache_k] * pps), *([cache_v] * pps),
      *([rin] * pps), *([tot] * pps))


def _to_slabs(ref, x):
    for k in range(ref.shape[1]):
        ref[:, k, :] = x[:, k * LANE:(k + 1) * LANE]


def _from_slabs(load, S):
    return jnp.concatenate([load(k) for k in range(S)], axis=1)


def _outproj_ln_kernel(h1p_ref, h1s_ref, h2p_ref, h2s_ref, w_ref, res_ref, g_ref, b_ref, o_ref, oslab_ref,
                       *, alpha, half, n_first):
    first = pl.program_id(0) < n_first
    h1 = jnp.where(first, h1p_ref[...], h1s_ref[...])
    h2 = jnp.where(first, h2p_ref[...], h2s_ref[...])
    y = jnp.dot(h1, w_ref[0:half, :], preferred_element_type=F32)
    y = y + jnp.dot(h2, w_ref[half:2 * half, :], preferred_element_type=F32)
    z = _layer_norm(alpha * res_ref[...] + y, g_ref[...], b_ref[...])
    o_ref[...] = z
    _to_slabs(oslab_ref, z)


def outproj_ln(h1, h2, w_bf, res, g, b, alpha):
    (h1p, h1s), (h2p, h2s) = h1, h2
    half = h1p.shape[1]
    T, D = res.shape
    S = D // LANE
    tm = _tile(h1s.shape[0], (256, 128))
    n1 = h1p.shape[0] // tm
    assert h1p.shape[0] % tm == 0 and h1p.shape[0] + h1s.shape[0] == T
    first = lambda i: (jnp.minimum(i, n1 - 1), 0)
    second = lambda i: (jnp.maximum(i - n1, 0), 0)
    return pl.pallas_call(
        functools.partial(_outproj_ln_kernel, alpha=alpha, half=half, n_first=n1),
        grid=(T // tm,),
        in_specs=[pl.BlockSpec((tm, half), first), pl.BlockSpec((tm, half), second),
                  pl.BlockSpec((tm, half), first), pl.BlockSpec((tm, half), second),
                  pl.BlockSpec((2 * half, D), lambda i: (0, 0)),
                  pl.BlockSpec((tm, D), lambda i: (i, 0)),
                  pl.BlockSpec((1, D), lambda i: (0, 0)),
                  pl.BlockSpec((1, D), lambda i: (0, 0))],
        out_specs=[pl.BlockSpec((tm, D), lambda i: (i, 0)), pl.BlockSpec((tm, S, LANE), lambda i: (i, 0, 0))],
        out_shape=[jax.ShapeDtypeStruct((T, D), F32), jax.ShapeDtypeStruct((T, S, LANE), F32)],
        compiler_params=_params("arbitrary"),
        name="outproj_ln",
    )(h1p, h1s, h2p, h2s, w_bf, res, g.reshape(1, D), b.reshape(1, D))


def _hgrn_kernel(*refs, C, sb, tl, bb, hh, has_s0):
    if has_s0:
        qz_ref, fz_ref, iz_ref, gz_ref, lb_ref, go_ref, s0_ref, o_ref, sn_ref, st_ref = refs
    else:
        qz_ref, fz_ref, iz_ref, gz_ref, lb_ref, go_ref, o_ref, sn_ref, st_ref = refs
    t = pl.program_id(2)
    nsb = C // sb

    @pl.when(t == 0)
    def _():
        for s in range(bb):
            for h in range(hh):
                st_ref[s, h] = s0_ref[s, h].T if has_s0 else jnp.zeros((LANE, LANE), F32)

    lower = (_iota((C, C), 1) <= _iota((C, C), 0))
    lower_f = lower.astype(F32)

    def head_chunk(qz, fz, iz, gz, lb, go, st):
        q = _silu(qz)
        f = lb + (1.0 - lb) * _sigmoid(fz)
        kk = 1.0 - f
        i_bf = iz.astype(BF16)
        b = jnp.dot(lower_f, jnp.log(f), precision=HIGHEST, preferred_element_type=F32)
        b_last = b[C - 1:C, :]
        starts = [jnp.zeros((1, LANE), F32)] + [b[I * sb - 1:I * sb, :] for I in range(1, nsb)]
        lasts = [b[(I + 1) * sb - 1:(I + 1) * sb, :] for I in range(nsb)]
        blk = lambda a, I: a[I * sb:(I + 1) * sb, :]
        kd = [blk(kk, J) * jnp.exp(lasts[J] - blk(b, J)) for J in range(nsb)]
        att_rows = []
        for I in range(nsb):
            qd = blk(q, I) * jnp.exp(blk(b, I) - starts[I])
            parts = [kd[J] * jnp.exp(starts[I] - lasts[J]) for J in range(I)]
            parts.append(blk(kk, I) * jnp.exp(starts[I] - blk(b, I)))
            if I + 1 < nsb:
                parts.append(jnp.zeros(((nsb - I - 1) * sb, LANE), F32))
            kmat = jnp.concatenate(parts, axis=0) if len(parts) > 1 else parts[0]
            att_rows.append(lax.dot_general(qd.astype(BF16), kmat.astype(BF16), NT, preferred_element_type=F32))
        att = jnp.concatenate(att_rows, axis=0) if nsb > 1 else att_rows[0]
        att = jnp.where(lower, att, 0.0)
        o = jnp.dot(att.astype(BF16), i_bf, preferred_element_type=F32)
        o = o + lax.dot_general((q * jnp.exp(b)).astype(BF16), st.astype(BF16), NT, preferred_element_type=F32)
        kst = (kk * jnp.exp(b_last - b)).astype(BF16)
        st_new = st * jnp.exp(b_last) + lax.dot_general(i_bf, kst, TN, preferred_element_type=F32)
        ms = jnp.mean(o * o, axis=-1, keepdims=True)
        return o * lax.rsqrt(ms + LN_EPS) * go * _silu(gz), st_new

    def chunk(s, r0):
        qz, fz, iz, gz = (ref[pl.ds(r0, C), :] for ref in (qz_ref, fz_ref, iz_ref, gz_ref))
        lb, go = lb_ref[...], go_ref[...]
        head = lambda a, h: a[:, h * LANE:(h + 1) * LANE]
        res = [head_chunk(*(head(a, h) for a in (qz, fz, iz, gz, lb, go)), st_ref[s, h]) for h in range(hh)]
        o = jnp.concatenate([r[0] for r in res], axis=1) if hh > 1 else res[0][0]
        return o.astype(o_ref.dtype), jnp.stack([r[1] for r in res])

    if tl == C:
        res = [chunk(s, s * tl) for s in range(bb)]
        o_ref[...] = jnp.concatenate([r[0] for r in res], axis=0) if bb > 1 else res[0][0]
        st_ref[...] = jnp.stack([r[1] for r in res])
    else:
        assert bb == 1

        def body(c, carry):
            r0 = pl.multiple_of(c * C, C)
            o, st_new = chunk(0, r0)
            o_ref[pl.ds(r0, C), :] = o
            st_ref[0] = st_new
            return carry

        lax.fori_loop(0, tl // C, body, 0)

    @pl.when(t == pl.num_programs(2) - 1)
    def _():
        for s in range(bb):
            for h in range(hh):
                sn_ref[s, h] = st_ref[s, h].T


def hgrn(proj, row_start, n_seq, seq_len, lb, g_o, s0, H):
    C = min(HG_CHUNK, seq_len)
    sb = min(HG_SUB, C)
    assert seq_len % C == 0 and C % sb == 0
    if seq_len >= 512:
        tl, bb, hh = _tile(seq_len, (512,)), 1, _tile(H, (4, 2, 1))
    else:
        assert seq_len == C
        tl, bb, hh = seq_len, _tile(n_seq, (8, 4, 2, 1)), 1
    nt = seq_len // tl
    rows = bb * tl
    W = hh * LANE
    assert row_start % rows == 0
    r0 = row_start // rows
    col = lambda k: (lambda b, h, t: (r0 + b * nt + t, k * (H // hh) + h))
    in_specs = [pl.BlockSpec((rows, W), col(0)), pl.BlockSpec((rows, W), col(1)),
                pl.BlockSpec((rows, W), col(2)), pl.BlockSpec((rows, W), col(3)),
                pl.BlockSpec((1, W), lambda b, h, t: (0, h)),
                pl.BlockSpec((1, W), lambda b, h, t: (0, h))]
    args = [proj, proj, proj, proj, lb.reshape(1, H * LANE), g_o.reshape(1, H * LANE)]
    if s0 is not None:
        in_specs.append(pl.BlockSpec((bb, hh, LANE, LANE), lambda b, h, t: (b, h, 0, 0)))
        args.append(s0)
    return pl.pallas_call(
        functools.partial(_hgrn_kernel, C=C, sb=sb, tl=tl, bb=bb, hh=hh, has_s0=s0 is not None),
        grid=(n_seq // bb, H // hh, nt),
        in_specs=in_specs,
        out_specs=[pl.BlockSpec((rows, W), lambda b, h, t: (b * nt + t, h)),
                   pl.BlockSpec((bb, hh, LANE, LANE), lambda b, h, t: (b, h, 0, 0))],
        out_shape=[jax.ShapeDtypeStruct((n_seq * seq_len, H * LANE), BF16),
                   jax.ShapeDtypeStruct((n_seq, H, LANE, LANE), F32)],
        scratch_shapes=[pltpu.VMEM((bb, hh, LANE, LANE), F32)],
        compiler_params=_params("arbitrary", "arbitrary", "arbitrary"),
        name="hgrn",
    )(*args)


def _gmlp_kernel(uz_ref, vz_ref, w_ref, bt_ref, g_ref, b_ref, d_ref, v_ref, *, G, cs):
    n = uz_ref.shape[0]
    v = _layer_norm(_gelu(vz_ref[...]), g_ref[...], b_ref[...])
    v_ref[...] = v
    u = _gelu(uz_ref[...])
    t = _iota((n, n), 0)
    s = _iota((n, n), 1)
    keep = (s <= t) & (t // cs == s // cs)
    for g in range(G):
        w = jnp.where(keep, w_ref[g], 0.0).astype(BF16)
        vg = v[:, g * LANE:(g + 1) * LANE].astype(BF16)
        mixed = jnp.dot(w, vg, preferred_element_type=F32) + bt_ref[:, g:g + 1]
        d_ref[:, g * LANE:(g + 1) * LANE] = (u[:, g * LANE:(g + 1) * LANE] * mixed).astype(d_ref.dtype)


def gmlp(proj, row_start, n_rows, seq_len, w_s, b_s, g_v, b_v):
    G, n, _ = w_s.shape
    Dh = G * LANE
    cs = min(n, seq_len)
    assert n % cs == 0 and seq_len % cs == 0 and n_rows % n == 0 and row_start % n == 0
    if cs < n:
        w_s = jnp.tile(w_s[:, :cs, :cs], (1, n // cs, n // cs))
        b_s = jnp.tile(b_s[:, :cs], (1, n // cs))
    r0 = row_start // n
    return pl.pallas_call(
        functools.partial(_gmlp_kernel, G=G, cs=cs),
        grid=(n_rows // n,),
        in_specs=[pl.BlockSpec((n, Dh), lambda i: (r0 + i, 4)),
                  pl.BlockSpec((n, Dh), lambda i: (r0 + i, 5)),
                  pl.BlockSpec((G, n, n), lambda i: (0, 0, 0)),
                  pl.BlockSpec((n, G), lambda i: (0, 0)),
                  pl.BlockSpec((1, Dh), lambda i: (0, 0)),
                  pl.BlockSpec((1, Dh), lambda i: (0, 0))],
        out_specs=[pl.BlockSpec((n, Dh), lambda i: (i, 0)), pl.BlockSpec((n, Dh), lambda i: (i, 0))],
        out_shape=[jax.ShapeDtypeStruct((n_rows, Dh), BF16), jax.ShapeDtypeStruct((n_rows, Dh), F32)],
        compiler_params=_params("arbitrary"),
        name="gmlp",
    )(proj, proj, w_s, b_s.T, g_v.reshape(1, Dh), b_v.reshape(1, Dh))


def _router_kernel(x_ref, wt_ref, bt_ref, idx_ref, gate_ref, cnt_ref, run_ref, *, E):
    tm = x_ref.shape[0]
    per = E // N_GROUPS
    logits = lax.dot_general(wt_ref[...], x_ref[...], NT, precision=HIGHEST, preferred_element_type=F32)
    z = jnp.exp(logits - jnp.max(logits, axis=0, keepdims=True))
    probs = z / jnp.sum(z, axis=0, keepdims=True)
    sel = probs + bt_ref[...]
    io = _iota((per, tm), 0)

    def top2(sg):
        m1 = jnp.max(sg, axis=0, keepdims=True)
        i1 = jnp.min(jnp.where(sg == m1, io, per), axis=0, keepdims=True)
        rest = jnp.where(io == i1, -jnp.inf, sg)
        m2 = jnp.max(rest, axis=0, keepdims=True)
        i2 = jnp.min(jnp.where(rest == m2, io, per), axis=0, keepdims=True)
        return m1 + m2, i1, i2

    best, e1, e2 = top2(sel[0:per, :])
    for g in range(1, N_GROUPS):
        score, i1, i2 = top2(sel[g * per:(g + 1) * per, :])
        better = score > best
        best = jnp.where(better, score, best)
        e1 = jnp.where(better, i1 + g * per, e1)
        e2 = jnp.where(better, i2 + g * per, e2)
    eo = _iota((E, tm), 0)
    p1 = jnp.sum(jnp.where(eo == e1, probs, 0.0), axis=0, keepdims=True)
    p2 = jnp.sum(jnp.where(eo == e2, probs, 0.0), axis=0, keepdims=True)
    tot = p1 + p2
    gate_ref[...] = jnp.concatenate([p1 / tot, p2 / tot, jnp.zeros((6, tm), F32)], axis=0)

    @pl.when(pl.program_id(0) == 0)
    def _():
        run_ref[...] = jnp.zeros_like(run_ref)

    chosen = jnp.where(eo == e1, 1.0, jnp.where(eo == e2, 1.0, 0.0))
    earlier = (_iota((tm, tm), 0) < _iota((tm, tm), 1)).astype(BF16)
    before = jnp.dot(chosen.astype(BF16), earlier, preferred_element_type=F32) + run_ref[...]
    r1 = jnp.sum(jnp.where(eo == e1, before, 0.0), axis=0, keepdims=True).astype(jnp.int32)
    r2 = jnp.sum(jnp.where(eo == e2, before, 0.0), axis=0, keepdims=True).astype(jnp.int32)
    run_ref[...] = run_ref[...] + jnp.sum(chosen, axis=1, keepdims=True)
    idx_ref[...] = jnp.concatenate([e1, e2, r1, r2, jnp.zeros((4, tm), jnp.int32)], axis=0)
    cnt_ref[...] = jnp.broadcast_to(run_ref[...], cnt_ref.shape).astype(jnp.int32)


def router(x, w_router, b_router):
    T, D = x.shape
    E = w_router.shape[1]
    tm = _tile(T, (512, 256, 128))
    return pl.pallas_call(
        functools.partial(_router_kernel, E=E),
        grid=(T // tm,),
        in_specs=[pl.BlockSpec((tm, D), lambda i: (i, 0)),
                  pl.BlockSpec((E, D), lambda i: (0, 0)),
                  pl.BlockSpec((E, 1), lambda i: (0, 0))],
        out_specs=[pl.BlockSpec((8, tm), lambda i: (0, i)), pl.BlockSpec((8, tm), lambda i: (0, i)),
                   pl.BlockSpec((E, LANE), lambda i: (0, 0))],
        out_shape=[jax.ShapeDtypeStruct((8, T), jnp.int32), jax.ShapeDtypeStruct((8, T), F32),
                   jax.ShapeDtypeStruct((E, LANE), jnp.int32)],
        scratch_shapes=[pltpu.VMEM((E, 1), F32)],
        compiler_params=_params("arbitrary"),
        name="router",
    )(x, w_router.T, b_router.reshape(E, 1))


GATHER_UNROLL = 8


def _slab_copy(src_hbm, row, dst_ref, slot, sem):
    return pltpu.make_async_copy(src_hbm.at[row], dst_ref.at[slot], sem)


def _wait_slabs(src_hbm, dst_ref, sem):
    pltpu.make_async_copy(src_hbm.at[pl.ds(0, dst_ref.shape[0])], dst_ref, sem).wait()


def _start_slab_copies(src_hbm, row_of, dst_ref, sem, n):
    def body(g, carry):
        for u in range(GATHER_UNROLL):
            r = g * GATHER_UNROLL + u
            _slab_copy(src_hbm, row_of(r), dst_ref, r, sem).start(priority=u % 2)
        return carry

    lax.fori_loop(0, n // GATHER_UNROLL, body, 0)


def _gather_kernel(src_ref, x_hbm, o_ref, buf_ref, sems, *, tr):
    i = pl.program_id(0)

    def issue(tile, slot):
        _start_slab_copies(x_hbm, lambda r: src_ref[tile * tr + r], buf_ref.at[slot], sems.at[slot], tr)

    @pl.when(i == 0)
    def _():
        issue(0, 0)

    @pl.when(i + 1 < pl.num_programs(0))
    def _():
        issue(i + 1, (i + 1) % 2)

    slot = i % 2
    _wait_slabs(x_hbm, buf_ref.at[slot], sems.at[slot])
    o_ref[...] = _from_slabs(lambda k: buf_ref[slot, :, k, :], buf_ref.shape[2]).astype(o_ref.dtype)


def gather_rows(x_slabs, src_rows, tr):
    R = src_rows.shape[0]
    _, S, _ = x_slabs.shape
    grid_spec = pltpu.PrefetchScalarGridSpec(
        num_scalar_prefetch=1,
        grid=(R // tr,),
        in_specs=[pl.BlockSpec(memory_space=pl.ANY)],
        out_specs=pl.BlockSpec((tr, S * LANE), lambda i, src: (i, 0)),
        scratch_shapes=[pltpu.VMEM((2, tr, S, LANE), F32), pltpu.SemaphoreType.DMA((2,))],
    )
    assert tr % GATHER_UNROLL == 0
    return pl.pallas_call(
        functools.partial(_gather_kernel, tr=tr),
        grid_spec=grid_spec,
        out_shape=jax.ShapeDtypeStruct((R, S * LANE), BF16),
        compiler_params=_params("arbitrary"),
        name="moe_gather",
    )(src_rows, x_slabs)


def _expert_kernel(te_ref, nv_ref, fresh_ref, xs_ref, wg_ref, wu_ref, wd_ref, rg_ref, y_ref,
                   wg_bf, wu_bf, wd_bf, acc_ref):
    i = pl.program_id(0)
    j = pl.program_id(1)
    valid = i < nv_ref[0]
    last = j == pl.num_programs(1) - 1

    @pl.when(valid & (fresh_ref[i] == 1))
    def _():
        wg_bf[j] = wg_ref[...].astype(BF16)
        wu_bf[j] = wu_ref[...].astype(BF16)
        wd_bf[j] = wd_ref[...].astype(BF16)

    @pl.when(valid)
    def _():
        x = xs_ref[...]
        hg = jnp.dot(x, wg_bf[j], preferred_element_type=F32)
        hu = jnp.dot(x, wu_bf[j], preferred_element_type=F32)
        h = (_silu(hg) * hu).astype(BF16)
        part = jnp.dot(h, wd_bf[j], preferred_element_type=F32)

        @pl.when(j == 0)
        def _():
            acc_ref[...] = part

        @pl.when(j > 0)
        def _():
            acc_ref[...] = acc_ref[...] + part

        @pl.when(last)
        def _():
            _to_slabs(y_ref, acc_ref[...] * rg_ref[...])

    @pl.when(jnp.logical_not(valid) & last)
    def _():
        y_ref[...] = jnp.zeros_like(y_ref)


def expert_ffn(xs, tile_expert, n_valid, fresh, row_gate, w_gate, w_up, w_down, layer, tr):
    R, D = xs.shape
    S = D // LANE
    De = w_gate.shape[-1]
    te = _tile(De, (512, 256, 128))
    nj = De // te
    chunk = lambda i, j, fr: jnp.where(fr[i] == 1, j, nj - 1)
    grid_spec = pltpu.PrefetchScalarGridSpec(
        num_scalar_prefetch=3,
        grid=(R // tr, nj),
        in_specs=[pl.BlockSpec((tr, D), lambda i, j, e, nv, fr: (i, 0)),
                  pl.BlockSpec((None, None, D, te), lambda i, j, e, nv, fr: (layer, e[i], 0, chunk(i, j, fr))),
                  pl.BlockSpec((None, None, D, te), lambda i, j, e, nv, fr: (layer, e[i], 0, chunk(i, j, fr))),
                  pl.BlockSpec((None, None, te, D), lambda i, j, e, nv, fr: (layer, e[i], chunk(i, j, fr), 0)),
                  pl.BlockSpec((tr, 1), lambda i, j, e, nv, fr: (i, 0))],
        out_specs=pl.BlockSpec((tr, S, LANE), lambda i, j, e, nv, fr: (i, 0, 0)),
        scratch_shapes=[pltpu.VMEM((nj, D, te), BF16), pltpu.VMEM((nj, D, te), BF16),
                        pltpu.VMEM((nj, te, D), BF16), pltpu.VMEM((tr, D), F32)],
    )
    return pl.pallas_call(
        _expert_kernel,
        grid_spec=grid_spec,
        out_shape=jax.ShapeDtypeStruct((R, S, LANE), F32),
        compiler_params=_params("arbitrary", "arbitrary"),
        name="moe_experts",
    )(tile_expert, n_valid, fresh, xs, w_gate, w_up, w_down, row_gate)


def _combine_ln_kernel(pos_ref, y_hbm, res_ref, g_ref, b_ref, o_ref, o2_ref, buf_ref, sems, *, tm, alpha, n_first):
    i = pl.program_id(0)

    def issue(tile, slot):
        for k in range(2):
            _start_slab_copies(y_hbm, lambda r: pos_ref[2 * (tile * tm + r) + k], buf_ref.at[slot, k],
                               sems.at[slot], tm)

    @pl.when(i == 0)
    def _():
        issue(0, 0)

    @pl.when(i + 1 < pl.num_programs(0))
    def _():
        issue(i + 1, (i + 1) % 2)

    slot = i % 2
    for k in range(2):
        _wait_slabs(y_hbm, buf_ref.at[slot, k], sems.at[slot])
    moe = _from_slabs(lambda k: buf_ref[slot, 0, :, k, :] + buf_ref[slot, 1, :, k, :], buf_ref.shape[3])
    z = _layer_norm(alpha * res_ref[...] + moe, g_ref[...], b_ref[...])
    if n_first is None:
        o_ref[...] = z
        o2_ref[...] = z.astype(BF16)
    else:
        @pl.when(i < n_first)
        def _():
            o_ref[...] = z

        @pl.when(i >= n_first)
        def _():
            o2_ref[...] = z


def combine_ln(y_slabs, pos, res, g, b, alpha, split_rows=None):
    T, D = res.shape
    S = D // LANE
    tm = _tile(T if split_rows is None else split_rows, (256, 128))
    assert tm % GATHER_UNROLL == 0 and T % tm == 0
    if split_rows is None:
        n1 = None
        out_specs = [pl.BlockSpec((tm, D), lambda i, pos: (i, 0)), pl.BlockSpec((tm, D), lambda i, pos: (i, 0))]
        out_shape = [jax.ShapeDtypeStruct((T, D), F32), jax.ShapeDtypeStruct((T, D), BF16)]
    else:
        n1 = split_rows // tm
        assert 0 < n1 < T // tm
        out_specs = [pl.BlockSpec((tm, D), lambda i, pos: (jnp.minimum(i, n1 - 1), 0)),
                     pl.BlockSpec((tm, D), lambda i, pos: (jnp.maximum(i - n1, 0), 0))]
        out_shape = [jax.ShapeDtypeStruct((split_rows, D), F32), jax.ShapeDtypeStruct((T - split_rows, D), F32)]
    grid_spec = pltpu.PrefetchScalarGridSpec(
        num_scalar_prefetch=1,
        grid=(T // tm,),
        in_specs=[pl.BlockSpec(memory_space=pl.ANY),
                  pl.BlockSpec((tm, D), lambda i, pos: (i, 0)),
                  pl.BlockSpec((1, D), lambda i, pos: (0, 0)),
                  pl.BlockSpec((1, D), lambda i, pos: (0, 0))],
        out_specs=out_specs,
        scratch_shapes=[pltpu.VMEM((2, 2, tm, S, LANE), F32), pltpu.SemaphoreType.DMA((2,))],
    )
    return pl.pallas_call(
        functools.partial(_combine_ln_kernel, tm=tm, alpha=alpha, n_first=n1),
        grid_spec=grid_spec,
        out_shape=out_shape,
        compiler_params=_params("arbitrary"),
        name="moe_combine_ln",
    )(pos, y_slabs, res, g.reshape(1, D), b.reshape(1, D))


def _dispatch_plan(e_idx, gates, counts, E, tr):
    T = e_idx.shape[1]
    pairs = 2 * T
    e_flat = e_idx[0:2, :].T.reshape(pairs)
    rank = e_idx[2:4, :].T.reshape(pairs)
    g_flat = gates[0:2, :].T.reshape(pairs)
    padded = ((counts + tr - 1) // tr) * tr
    pstart = jnp.cumsum(padded) - padded
    pos = (pstart[e_flat] + rank).astype(jnp.int32)
    n_tiles = pairs // tr + E
    R = n_tiles * tr
    src_tok = jnp.zeros((R,), jnp.int32).at[pos].set(jnp.arange(pairs, dtype=jnp.int32) // 2, unique_indices=True)
    row_gate = jnp.zeros((R,), F32).at[pos].set(g_flat, unique_indices=True).reshape(R, 1)
    n_valid = (jnp.sum(padded) // tr).astype(jnp.int32).reshape(1)
    tile_end = (pstart + padded) // tr
    tiles = jnp.arange(n_tiles, dtype=jnp.int32)
    tile_e = jnp.minimum(jnp.sum((tiles[:, None] >= tile_end[None, :]).astype(jnp.int32), axis=1), E - 1)
    last_e = tile_e[jnp.maximum(n_valid[0] - 1, 0)]
    tile_e = jnp.where(tiles < n_valid[0], tile_e, last_e).astype(jnp.int32)
    fresh = jnp.concatenate([jnp.ones((1,), jnp.int32), (tile_e[1:] != tile_e[:-1]).astype(jnp.int32)])
    return src_tok, row_gate, pos, tile_e, n_valid, fresh


def moe_ln(x, x_slabs, x_res_scale, w_router, b_router, w_gate, w_up, w_down, layer, g, b, split_rows=None):
    T, D = x.shape
    E = w_router.shape[1]
    tr = _tile(2 * T, (256, 128))
    e_idx, gates, counts = router(x, w_router, b_router)
    src_tok, row_gate, pos, tile_e, n_valid, fresh = _dispatch_plan(e_idx, gates, counts[:, 0], E, tr)
    xs = gather_rows(x_slabs, src_tok, tr)
    ys = expert_ffn(xs, tile_e, n_valid, fresh, row_gate, w_gate, w_up, w_down, layer, tr)
    return combine_ln(ys, pos, x, g, b, x_res_scale, split_rows)


def kernel(x_prompt, x_sample, cache_k, cache_v, cache_logf, state_conv, state_hgrn, page_table, w_in_even, b_fgate, w_dw, b_dw, g_cnorm, b_cnorm, w_out_even, w_in_odd, lb_logits, g_onorm, g_vnorm, b_vnorm, w_sgu, b_sgu, w_out_odd, ln1_g, ln1_b, ln2_g, ln2_b, w_router, b_router, w_gate, w_up, w_down):
    Bp, Lp, D = x_prompt.shape
    Bs, Ls, _ = x_sample.shape
    Dh = D // 2
    H = b_fgate.shape[1]
    HG = state_hgrn.shape[2]
    assert Dh == H * LANE and Dh == HG * LANE and Dh == w_sgu.shape[1] * LANE
    depth = ln1_g.shape[0]
    alpha = (2 * depth) ** 0.25
    Tp, Ts = Bp * Lp, Bs * Ls
    n_even = cache_k.shape[0]
    n_phys, page = cache_k.shape[1], cache_k.shape[2]

    lb_p = jax.nn.softmax(lb_logits.astype(F32), axis=0)
    lb_all = jnp.cumsum(lb_p, axis=0) - lb_p[0]

    x = jnp.concatenate([x_prompt.reshape(Tp, D), x_sample.reshape(Ts, D)], axis=0)
    x_bf = x.astype(BF16)
    rin, tot = page_suffix(cache_logf.astype(F32).reshape(n_even * n_phys, page * H), H)
    rin = rin.reshape(n_even * n_phys, 1, page * H)
    tot = tot.reshape(n_even * n_phys, 1, page * H)

    out = {k: [] for k in ("kp", "vp", "lfp", "convp", "hgp", "ks", "vs", "lfs", "convs", "hgs", "mlpv")}
    for l in range(depth):
        j = l // 2
        if l % 2 == 0:
            w_in = w_in_even[j]
            ag = matmul_cols(x_bf, w_in, 0, 2 * Dh)
            q = matmul_cols(x_bf, w_in, 2 * Dh, Dh)
            k_p, k_s = matmul_cols(x_bf, w_in, 3 * Dh, Dh, split_rows=Tp)
            v_p, v_s = matmul_cols(x_bf, w_in, 4 * Dh, Dh, split_rows=Tp)
            lf, _, c, ct = logf_project(x_bf, w_in[:, 5 * Dh:], b_fgate[j], Lp)
            conv_args = (w_dw[j], b_dw[j], g_cnorm[j], b_cnorm[j])
            ca_p, cst_p = conv_prompt(ag, Bp, Lp, *conv_args)
            ca_s, cst_s = conv_sample(ag, Tp, Bs, Ls, state_conv[j], *conv_args)
            att_p = fox_prompt(q, k_p, v_p, c, ct, Bp, Lp, H)
            att_s = fox_sample(q, k_s, v_s, lf, Tp, Bs, Ls, cache_k, cache_v, rin, tot, j, page_table, H)
            h1 = (ca_p, ca_s)
            h2 = (att_p, att_s.astype(BF16))
            w_out = w_out_even[j]
            out["kp"].append(k_p.reshape(Bp, Lp, H, LANE))
            out["vp"].append(v_p.reshape(Bp, Lp, H, LANE))
            out["lfp"].append(lf[:Tp].reshape(Bp, Lp, H))
            out["convp"].append(cst_p)
            out["ks"].append(k_s.reshape(Bs, Ls, H, LANE))
            out["vs"].append(v_s.reshape(Bs, Ls, H, LANE))
            out["lfs"].append(lf[Tp:].reshape(Bs, Ls, H))
            out["convs"].append(cst_s)
        else:
            proj = matmul_cols(x_bf, w_in_odd[j], 0, 6 * Dh)
            o_p, s_p = hgrn(proj, 0, Bp, Lp, lb_all[l], g_onorm[j], None, HG)
            o_s, s_s = hgrn(proj, Tp, Bs, Ls, lb_all[l], g_onorm[j], state_hgrn[j].astype(F32), HG)
            mlp_args = (w_sgu[j], b_sgu[j], g_vnorm[j], b_vnorm[j])
            d_p, _ = gmlp(proj, 0, Tp, Lp, *mlp_args)
            d_s, v_s = gmlp(proj, Tp, Ts, Ls, *mlp_args)
            h1 = (o_p, o_s)
            h2 = (d_p, d_s)
            w_out = w_out_odd[j]
            out["hgp"].append(s_p)
            out["hgs"].append(s_s)
            out["mlpv"].append(v_s.reshape(Bs, Ls, Dh))
        x, x_slabs = outproj_ln(h1, h2, w_out.astype(BF16), x, ln1_g[l], ln1_b[l], alpha)
        moe_args = (x, x_slabs, alpha, w_router, b_router, w_gate, w_up, w_down, l, ln2_g[l], ln2_b[l])
        if l + 1 < depth:
            x, x_bf = moe_ln(*moe_args)
        else:
            y_p, y_s = moe_ln(*moe_args, split_rows=Tp)

    stack = lambda name: jnp.stack(out[name])
    return (y_p.reshape(Bp, Lp, D), y_s.reshape(Bs, Ls, D),
            stack("kp"), stack("vp"), stack("lfp"), stack("convp"), stack("hgp"),
            stack("ks"), stack("vs"), stack("lfs"), stack("convs"), stack("hgs"), stack("mlpv"))
```

```python
import functools
import math

import jax
import jax.numpy as jnp
from jax import lax
from jax.experimental import pallas as pl
from jax.experimental.pallas import tpu as pltpu

F32 = jnp.float32
BF16 = jnp.bfloat16
HIGHEST = lax.Precision.HIGHEST

LANE = 128
VMEM_LIMIT = 56 * 1024 * 1024
LN_EPS = 1e-5
HG_CHUNK = 64
HG_SUB = 16
CONV_HALO = 32
N_GROUPS = 4
NT = (((1,), (1,)), ((), ()))
TN = (((0,), (0,)), ((), ()))


def _params(*sem):
    return pltpu.CompilerParams(dimension_semantics=sem, vmem_limit_bytes=VMEM_LIMIT)


def _tile(n, prefs):
    for t in prefs:
        if n % t == 0:
            return t
    raise ValueError(f"no tile in {prefs} divides {n}")


def _sigmoid(x):
    return 1.0 / (1.0 + jnp.exp(-x))


def _silu(x):
    return x * _sigmoid(x)


def _gelu(x):
    return 0.5 * x * (1.0 + jnp.tanh(0.7978845608028654 * (x + 0.044715 * (x * x * x))))


def _log_sigmoid(x):
    return -(jnp.maximum(-x, 0.0) + jnp.log1p(jnp.exp(-jnp.abs(x))))


def _layer_norm(x, g, b):
    mu = jnp.mean(x, axis=-1, keepdims=True)
    xc = x - mu
    var = jnp.mean(xc * xc, axis=-1, keepdims=True)
    return xc * lax.rsqrt(var + LN_EPS) * g + b


def _iota(shape, dim):
    return lax.broadcasted_iota(jnp.int32, shape, dim)


def _matmul_kernel(x_ref, w_ref, o_ref, wbf_ref):
    @pl.when(pl.program_id(1) == 0)
    def _():
        wbf_ref[...] = w_ref[...].astype(BF16)

    o_ref[...] = jnp.dot(x_ref[...], wbf_ref[...], preferred_element_type=F32).astype(o_ref.dtype)


def _matmul_split_kernel(x_ref, w_ref, o1_ref, o2_ref, wbf_ref, *, n_first):
    i = pl.program_id(1)

    @pl.when(i == 0)
    def _():
        wbf_ref[...] = w_ref[...].astype(BF16)

    y = jnp.dot(x_ref[...], wbf_ref[...], preferred_element_type=F32)

    @pl.when(i < n_first)
    def _():
        o1_ref[...] = y

    @pl.when(i >= n_first)
    def _():
        o2_ref[...] = y


def matmul_cols(x_bf, w, col_start, n_cols, split_rows=None):
    M, K = x_bf.shape
    tm = _tile(M if split_rows is None else math.gcd(split_rows, M - split_rows), (1024, 512, 256, 128))
    tn = _tile(n_cols, (512, 256, 128))
    assert col_start % tn == 0 and M % tm == 0
    off = col_start // tn
    common = dict(
        grid=(n_cols // tn, M // tm),
        in_specs=[pl.BlockSpec((tm, K), lambda j, i: (i, 0)),
                  pl.BlockSpec((K, tn), lambda j, i: (0, j + off))],
        scratch_shapes=[pltpu.VMEM((K, tn), BF16)],
        compiler_params=_params("arbitrary", "arbitrary"),
    )
    if split_rows is None:
        return pl.pallas_call(
            _matmul_kernel,
            out_specs=pl.BlockSpec((tm, tn), lambda j, i: (i, j)),
            out_shape=jax.ShapeDtypeStruct((M, n_cols), F32),
            name="matmul_cols", **common,
        )(x_bf, w)
    n1 = split_rows // tm
    assert (M - split_rows) % tm == 0 and 0 < n1 < M // tm
    return pl.pallas_call(
        functools.partial(_matmul_split_kernel, n_first=n1),
        out_specs=[pl.BlockSpec((tm, tn), lambda j, i: (jnp.minimum(i, n1 - 1), j)),
                   pl.BlockSpec((tm, tn), lambda j, i: (jnp.maximum(i - n1, 0), j))],
        out_shape=[jax.ShapeDtypeStruct((split_rows, n_cols), F32),
                   jax.ShapeDtypeStruct((M - split_rows, n_cols), F32)],
        name="matmul_cols_split", **common,
    )(x_bf, w)


def _logf_kernel(x_ref, w_ref, wt_ref, b_ref, bt_ref, lf_ref, lft_ref, c_ref, ct_ref,
                 carry_ref, carryt_ref, *, tiles_per_seq):
    i = pl.program_id(0)
    x = x_ref[...]
    tm = x.shape[0]
    fz = jnp.dot(x, w_ref[...].astype(BF16), preferred_element_type=F32) + b_ref[...]
    fzt = lax.dot_general(wt_ref[...].astype(BF16), x, NT, preferred_element_type=F32) + bt_ref[...]
    lf = _log_sigmoid(fz)
    lft = _log_sigmoid(fzt)
    lf_ref[...] = lf
    lft_ref[...] = lft

    @pl.when(i % tiles_per_seq == 0)
    def _():
        carry_ref[...] = jnp.zeros_like(carry_ref)
        carryt_ref[...] = jnp.zeros_like(carryt_ref)

    row = _iota((tm, tm), 0)
    col = _iota((tm, tm), 1)
    lower = (col <= row).astype(F32)
    upper = (row <= col).astype(F32)
    c = jnp.dot(lower, lf, precision=HIGHEST, preferred_element_type=F32) + carry_ref[...]
    ct = jnp.dot(lft, upper, precision=HIGHEST, preferred_element_type=F32) + carryt_ref[...]
    c_ref[...] = c
    ct_ref[...] = ct
    carry_ref[...] = c[tm - 1:tm, :]
    carryt_ref[...] = ct[:, tm - 1:tm]


def logf_project(x_bf, w_f, b_f, seq_len):
    T, D = x_bf.shape
    H = w_f.shape[1]
    tm = _tile(seq_len, (512, 256, 128))
    assert T % tm == 0
    outs = pl.pallas_call(
        functools.partial(_logf_kernel, tiles_per_seq=seq_len // tm),
        grid=(T // tm,),
        in_specs=[pl.BlockSpec((tm, D), lambda i: (i, 0)),
                  pl.BlockSpec((D, H), lambda i: (0, 0)),
                  pl.BlockSpec((H, D), lambda i: (0, 0)),
                  pl.BlockSpec((1, H), lambda i: (0, 0)),
                  pl.BlockSpec((H, 1), lambda i: (0, 0))],
        out_specs=[pl.BlockSpec((tm, H), lambda i: (i, 0)),
                   pl.BlockSpec((H, tm), lambda i: (0, i)),
                   pl.BlockSpec((tm, H), lambda i: (i, 0)),
                   pl.BlockSpec((H, tm), lambda i: (0, i))],
        out_shape=[jax.ShapeDtypeStruct((T, H), F32), jax.ShapeDtypeStruct((H, T), F32),
                   jax.ShapeDtypeStruct((T, H), F32), jax.ShapeDtypeStruct((H, T), F32)],
        scratch_shapes=[pltpu.VMEM((1, H), F32), pltpu.VMEM((H, 1), F32)],
        compiler_params=_params("arbitrary"),
        name="logf_project",
    )(x_bf, w_f, w_f.T, b_f.reshape(1, H), b_f.reshape(H, 1))
    return outs


def _conv_tail(acc, bdw, g, b):
    return _silu(_layer_norm(acc + bdw, g, b))


def _conv_prompt_kernel(ag_ref, wdw_ref, bdw_ref, g_ref, b_ref, y_ref, st_ref, ext_ref, *, W, tt, C):
    t = pl.program_id(1)

    @pl.when(t == 0)
    def _():
        ext_ref[0:CONV_HALO, :] = jnp.zeros((CONV_HALO, C), F32)

    @pl.when(t > 0)
    def _():
        ext_ref[0:CONV_HALO, :] = ext_ref[tt:tt + CONV_HALO, :]

    u = ag_ref[:, 0:C] * _sigmoid(ag_ref[:, C:2 * C])
    ext_ref[CONV_HALO:CONV_HALO + tt, :] = u
    base = CONV_HALO - (W - 1)
    acc = ext_ref[base:base + tt, :] * wdw_ref[0:1, :]
    for w in range(1, W):
        acc = acc + ext_ref[base + w:base + w + tt, :] * wdw_ref[w:w + 1, :]
    y_ref[...] = _conv_tail(acc, bdw_ref[...], g_ref[...], b_ref[...]).astype(y_ref.dtype)

    @pl.when(t == pl.num_programs(1) - 1)
    def _():
        st_ref[...] = ext_ref[CONV_HALO + tt - (W - 1):CONV_HALO + tt, :]


def conv_prompt(ag, n_seq, seq_len, w_dw, b_dw, g_n, b_n):
    W, C = w_dw.shape
    assert W - 1 <= CONV_HALO
    tt = _tile(seq_len, (256, 128))
    nt = seq_len // tt
    vec = lambda a: a.reshape(1, C)
    return pl.pallas_call(
        functools.partial(_conv_prompt_kernel, W=W, tt=tt, C=C),
        grid=(n_seq, nt),
        in_specs=[pl.BlockSpec((tt, 2 * C), lambda b, t: (b * nt + t, 0)),
                  pl.BlockSpec((W, C), lambda b, t: (0, 0)),
                  pl.BlockSpec((1, C), lambda b, t: (0, 0)),
                  pl.BlockSpec((1, C), lambda b, t: (0, 0)),
                  pl.BlockSpec((1, C), lambda b, t: (0, 0))],
        out_specs=[pl.BlockSpec((tt, C), lambda b, t: (b * nt + t, 0)),
                   pl.BlockSpec((None, W - 1, C), lambda b, t: (b, 0, 0))],
        out_shape=[jax.ShapeDtypeStruct((n_seq * seq_len, C), BF16),
                   jax.ShapeDtypeStruct((n_seq, W - 1, C), F32)],
        scratch_shapes=[pltpu.VMEM((CONV_HALO + tt, C), F32)],
        compiler_params=_params("arbitrary", "arbitrary"),
        name="conv_prompt",
    )(ag, w_dw, vec(b_dw), vec(g_n), vec(b_n))


def _conv_sample_kernel(ag_ref, st_ref, wdw_ref, bdw_ref, g_ref, b_ref, y_ref, nst_ref, ext_ref, *, W, Ls, C, bb):
    u = ag_ref[:, 0:C] * _sigmoid(ag_ref[:, C:2 * C])
    ext_ref[:, 0:W - 1, :] = st_ref[...]
    ext_ref[:, W - 1:W - 1 + Ls, :] = u.reshape(bb, Ls, C)
    acc = ext_ref[:, 0:Ls, :] * wdw_ref[0:1, :]
    for w in range(1, W):
        acc = acc + ext_ref[:, w:w + Ls, :] * wdw_ref[w:w + 1, :]
    y = _conv_tail(acc.reshape(bb * Ls, C), bdw_ref[...], g_ref[...], b_ref[...])
    y_ref[...] = y.astype(y_ref.dtype)
    nst_ref[...] = ext_ref[:, Ls:Ls + W - 1, :]


def conv_sample(ag, row_start, n_seq, Ls, state, w_dw, b_dw, g_n, b_n):
    W, C = w_dw.shape
    bb = _tile(n_seq, (8, 4, 2, 1))
    rows = bb * Ls
    assert Ls % 8 == 0 and row_start % rows == 0
    r0 = row_start // rows
    vec = lambda a: a.reshape(1, C)
    return pl.pallas_call(
        functools.partial(_conv_sample_kernel, W=W, Ls=Ls, C=C, bb=bb),
        grid=(n_seq // bb,),
        in_specs=[pl.BlockSpec((rows, 2 * C), lambda i: (r0 + i, 0)),
                  pl.BlockSpec((bb, W - 1, C), lambda i: (i, 0, 0)),
                  pl.BlockSpec((W, C), lambda i: (0, 0)),
                  pl.BlockSpec((1, C), lambda i: (0, 0)),
                  pl.BlockSpec((1, C), lambda i: (0, 0)),
                  pl.BlockSpec((1, C), lambda i: (0, 0))],
        out_specs=[pl.BlockSpec((rows, C), lambda i: (i, 0)),
                   pl.BlockSpec((bb, W - 1, C), lambda i: (i, 0, 0))],
        out_shape=[jax.ShapeDtypeStruct((n_seq * Ls, C), BF16),
                   jax.ShapeDtypeStruct((n_seq, W - 1, C), F32)],
        scratch_shapes=[pltpu.VMEM((bb, W - 1 + Ls, C), F32)],
        compiler_params=_params("arbitrary"),
        name="conv_sample",
    )(ag, state, w_dw, vec(b_dw), vec(g_n), vec(b_n))


def _fox_prompt_kernel(q_ref, k_ref, v_ref, c_ref, ct_ref, o_ref, *, tq, scale, H):
    h = pl.program_id(1)
    qi = pl.program_id(2)
    q = (q_ref[...] * scale).astype(BF16)
    cq = jnp.sum(jnp.where(_iota((tq, H), 1) == h, c_ref[...], 0.0), axis=1, keepdims=True)

    def step(ki, carry, on_diagonal):
        m, l, acc = carry
        start = pl.multiple_of(ki * tq, tq)
        k = k_ref[pl.ds(start, tq), :].astype(BF16)
        v = v_ref[pl.ds(start, tq), :].astype(BF16)
        ck = ct_ref[pl.ds(h, 1), pl.ds(start, tq)]
        s = lax.dot_general(q, k, NT, preferred_element_type=F32) + cq - ck
        if on_diagonal:
            s = jnp.where(_iota((tq, tq), 0) >= _iota((tq, tq), 1), s, -jnp.inf)
        m_new = jnp.maximum(m, jnp.max(s, axis=1, keepdims=True))
        alpha = jnp.exp(m - m_new)
        p = jnp.exp(s - m_new)
        l = alpha * l + jnp.sum(p, axis=1, keepdims=True)
        acc = alpha * acc + jnp.dot(p.astype(BF16), v, preferred_element_type=F32)
        return m_new, l, acc

    init = (jnp.full((tq, 1), -jnp.inf, F32), jnp.zeros((tq, 1), F32), jnp.zeros((tq, LANE), F32))
    carry = lax.fori_loop(0, qi, lambda ki, c: step(ki, c, False), init)
    _, l, acc = step(qi, carry, True)
    o_ref[...] = (acc / l).astype(o_ref.dtype)


def fox_prompt(q, k, v, c, ct, n_seq, seq_len, H):
    tq = _tile(seq_len, (512, 256, 128))
    nq = seq_len // tq
    return pl.pallas_call(
        functools.partial(_fox_prompt_kernel, tq=tq, scale=LANE ** -0.5, H=H),
        grid=(n_seq, H, nq),
        in_specs=[pl.BlockSpec((tq, LANE), lambda b, h, i: (b * nq + i, h)),
                  pl.BlockSpec((seq_len, LANE), lambda b, h, i: (b, h)),
                  pl.BlockSpec((seq_len, LANE), lambda b, h, i: (b, h)),
                  pl.BlockSpec((tq, H), lambda b, h, i: (b * nq + i, 0)),
                  pl.BlockSpec((H, seq_len), lambda b, h, i: (0, b))],
        out_specs=pl.BlockSpec((tq, LANE), lambda b, h, i: (b * nq + i, h)),
        out_shape=jax.ShapeDtypeStruct((n_seq * seq_len, H * LANE), BF16),
        compiler_params=_params("arbitrary", "arbitrary", "arbitrary"),
        name="fox_prompt",
    )(q, k, v, c, ct)


def _split3_dot(x, w_bf):
    x1 = x.astype(BF16)
    r1 = x - x1.astype(F32)
    x2 = r1.astype(BF16)
    x3 = (r1 - x2.astype(F32)).astype(BF16)
    dot = lambda a: jnp.dot(a, w_bf, preferred_element_type=F32)
    return dot(x1) + dot(x2) + dot(x3)


def _page_suffix_kernel(lf_ref, rin_ref, tot_ref, later_ref, same_ref, *, H):
    n = lf_ref.shape[1]

    @pl.when(pl.program_id(0) == 0)
    def _():
        r = _iota((n, n), 0)
        c = _iota((n, n), 1)
        same = (r % H) == (c % H)
        later_ref[...] = jnp.where(same & (r > c), 1.0, 0.0).astype(BF16)
        same_ref[...] = jnp.where(same, 1.0, 0.0).astype(BF16)

    x = lf_ref[...]
    rin_ref[...] = _split3_dot(x, later_ref[...])
    tot_ref[...] = _split3_dot(x, same_ref[...])


def page_suffix(lf_pages, H):
    P, n = lf_pages.shape
    tp = _tile(P, (256, 128, 64, 32, 16, 8))
    return pl.pallas_call(
        functools.partial(_page_suffix_kernel, H=H),
        grid=(P // tp,),
        in_specs=[pl.BlockSpec((tp, n), lambda i: (i, 0))],
        out_specs=[pl.BlockSpec((tp, n), lambda i: (i, 0)), pl.BlockSpec((tp, n), lambda i: (i, 0))],
        out_shape=[jax.ShapeDtypeStruct((P, n), F32), jax.ShapeDtypeStruct((P, n), F32)],
        scratch_shapes=[pltpu.VMEM((n, n), BF16), pltpu.VMEM((n, n), BF16)],
        compiler_params=_params("arbitrary"),
        name="page_suffix",
    )(lf_pages)


def _fox_sample_kernel(*refs, H, Ls, scale, pps):
    pt_ref, q_ref, kn_ref, vn_ref, lfn_ref = refs[0:5]
    ck_refs = refs[5:5 + pps]
    cv_refs = refs[5 + pps:5 + 2 * pps]
    rin_refs = refs[5 + 2 * pps:5 + 3 * pps]
    tot_refs = refs[5 + 3 * pps:5 + 4 * pps]
    o_ref, qh_ref, a_ref, m_ref, l_ref, acc_ref, rc_ref = refs[5 + 4 * pps:]
    p = pl.program_id(1)
    HQ = H * Ls
    cols = ck_refs[0].shape[0] * H
    sel = (_iota((HQ, H), 0) // Ls == _iota((HQ, H), 1)).astype(F32)
    causal = _iota((HQ, Ls), 1) <= _iota((HQ, Ls), 0) % Ls

    def new_logf():
        return lax.dot_general(sel, lfn_ref[...], NT, precision=HIGHEST, preferred_element_type=F32)

    @pl.when(p == 0)
    def _():
        q = q_ref[...]
        qh_ref[...] = jnp.concatenate([q[:, h * LANE:(h + 1) * LANE] for h in range(H)], axis=0).astype(BF16)
        a_ref[...] = jnp.sum(jnp.where(causal, new_logf(), 0.0), axis=1, keepdims=True)
        m_ref[...] = jnp.full((HQ, 1), -jnp.inf, F32)
        l_ref[...] = jnp.zeros((HQ, 1), F32)
        acc_ref[...] = jnp.zeros((HQ, LANE), F32)
        rc_ref[...] = jnp.zeros((1, cols), F32)

    def update(scores, pvs):
        m_old = m_ref[...]
        m_new = m_old
        for s in scores:
            m_new = jnp.maximum(m_new, jnp.max(s, axis=1, keepdims=True))
        alpha = jnp.exp(m_old - m_new)
        l_new = alpha * l_ref[...]
        acc = alpha * acc_ref[...]
        for s, pv in zip(scores, pvs):
            pr = jnp.exp(s - m_new)
            l_new = l_new + jnp.sum(pr, axis=1, keepdims=True)
            acc = acc + pv(pr.astype(BF16))
        l_ref[...] = l_new
        acc_ref[...] = acc
        m_ref[...] = m_new

    qh = qh_ref[...]
    own_head = (_iota((HQ, cols), 0) // Ls) == (_iota((HQ, cols), 1) % H)
    rc = rc_ref[...]
    scores, pvs = [], []
    for s_ in range(pps):
        kx = ck_refs[s_][...].reshape(cols, LANE).astype(BF16)
        vx = cv_refs[s_][...].reshape(cols, LANE).astype(BF16)
        bias = a_ref[...] + (rin_refs[s_][...] + rc)
        rc = rc + tot_refs[s_][...]
        sc = lax.dot_general(qh, kx, NT, preferred_element_type=F32) * scale + bias
        scores.append(jnp.where(own_head, sc, -jnp.inf))
        pvs.append(lambda pr, vx=vx: jnp.dot(pr, vx, preferred_element_type=F32))
    rc_ref[...] = rc
    update(scores, pvs)

    @pl.when(p == pl.num_programs(1) - 1)
    def _():
        head = lambda a, h: a[:, h * LANE:(h + 1) * LANE].astype(BF16)
        rows = lambda a, h: a[h * Ls:(h + 1) * Ls, :]
        kn = kn_ref[...]
        vn = vn_ref[...]
        s2 = jnp.concatenate([lax.dot_general(rows(qh, h), head(kn, h), NT, preferred_element_type=F32)
                              for h in range(H)], axis=0)
        upto = (_iota((Ls, Ls), 0) <= _iota((Ls, Ls), 1)).astype(F32)
        cum = jnp.dot(new_logf(), upto, precision=HIGHEST, preferred_element_type=F32)
        s2 = jnp.where(causal, s2 * scale + (a_ref[...] - cum), -jnp.inf)
        update([s2], [lambda pr: jnp.concatenate(
            [jnp.dot(rows(pr, h), head(vn, h), preferred_element_type=F32) for h in range(H)], axis=0)])
        out = acc_ref[...] / l_ref[...]
        for h in range(H):
            o_ref[:, h * LANE:(h + 1) * LANE] = rows(out, h)


def fox_sample(q, k, v, lf, row_start, n_seq, Ls, cache_k, cache_v, rin, tot, layer, page_table, H):
    D = H * LANE
    n_pages = page_table.shape[1]
    n_phys, page = cache_k.shape[1], cache_k.shape[2]
    pps = _tile(n_pages, (8, 4, 2, 1))
    assert Ls % 8 == 0 and row_start % Ls == 0
    r0 = row_start // Ls
    HQ = H * Ls
    rows = lambda b, p, pt: (r0 + b, 0)

    def phys(s_):
        return lambda b, p, pt: pt[b * n_pages + (n_pages - 1 - (p * pps + s_))]

    kv_spec = lambda s_: pl.BlockSpec((None, None, page, H, LANE),
                                      lambda b, p, pt: (layer, phys(s_)(b, p, pt), 0, 0, 0))
    row_spec = lambda s_: pl.BlockSpec((None, 1, page * H),
                                       lambda b, p, pt: (layer * n_phys + phys(s_)(b, p, pt), 0, 0))
    grid_spec = pltpu.PrefetchScalarGridSpec(
        num_scalar_prefetch=1,
        grid=(n_seq, n_pages // pps),
        in_specs=([pl.BlockSpec((Ls, D), rows), pl.BlockSpec((Ls, D), lambda b, p, pt: (b, 0)),
                   pl.BlockSpec((Ls, D), lambda b, p, pt: (b, 0)), pl.BlockSpec((Ls, H), rows)]
                  + [kv_spec(s_) for s_ in range(pps)] + [kv_spec(s_) for s_ in range(pps)]
                  + [row_spec(s_) for s_ in range(pps)] + [row_spec(s_) for s_ in range(pps)]),
        out_specs=pl.BlockSpec((Ls, D), lambda b, p, pt: (b, 0)),
        scratch_shapes=[pltpu.VMEM((HQ, LANE), BF16), pltpu.VMEM((HQ, 1), F32), pltpu.VMEM((HQ, 1), F32),
                        pltpu.VMEM((HQ, 1), F32), pltpu.VMEM((HQ, LANE), F32), pltpu.VMEM((1, page * H), F32)],
    )
    return pl.pallas_call(
        functools.partial(_fox_sample_kernel, H=H, Ls=Ls, scale=LANE ** -0.5, pps=pps),
        grid_spec=grid_spec,
        out_shape=jax.ShapeDtypeStruct((n_seq * Ls, D), F32),
        compiler_params=_params("arbitrary", "arbitrary"),
        name="fox_sample",
    )(page_table.reshape(-1), q, k, v, lf, *([cache_k] * pps), *([cache_v] * pps),
      *([rin] * pps), *([tot] * pps))


def _to_slabs(ref, x):
    for k in range(ref.shape[1]):
        ref[:, k, :] = x[:, k * LANE:(k + 1) * LANE]


def _from_slabs(load, S):
    return jnp.concatenate([load(k) for k in range(S)], axis=1)


def _outproj_ln_kernel(h1p_ref, h1s_ref, h2p_ref, h2s_ref, w_ref, resp_ref, ress_ref, g_ref, b_ref, o_ref,
                       oslab_ref, *, alpha, half, n_first):
    first = pl.program_id(0) < n_first
    h1 = jnp.where(first, h1p_ref[...], h1s_ref[...])
    h2 = jnp.where(first, h2p_ref[...], h2s_ref[...])
    res = jnp.where(first, resp_ref[...], ress_ref[...])
    y = jnp.dot(h1, w_ref[0:half, :], preferred_element_type=F32)
    y = y + jnp.dot(h2, w_ref[half:2 * half, :], preferred_element_type=F32)
    z = _layer_norm(alpha * res + y, g_ref[...], b_ref[...])
    o_ref[...] = z
    _to_slabs(oslab_ref, z)


def outproj_ln(h1, h2, w_bf, res, g, b, alpha):
    (h1p, h1s), (h2p, h2s), (resp, ress) = h1, h2, res
    half = h1p.shape[1]
    D = resp.shape[1]
    T = resp.shape[0] + ress.shape[0]
    S = D // LANE
    tm = _tile(math.gcd(resp.shape[0], ress.shape[0]), (256, 128))
    n1 = resp.shape[0] // tm
    assert h1p.shape[0] == resp.shape[0] and h1s.shape[0] == ress.shape[0]
    first = lambda i: (jnp.minimum(i, n1 - 1), 0)
    second = lambda i: (jnp.maximum(i - n1, 0), 0)
    return pl.pallas_call(
        functools.partial(_outproj_ln_kernel, alpha=alpha, half=half, n_first=n1),
        grid=(T // tm,),
        in_specs=[pl.BlockSpec((tm, half), first), pl.BlockSpec((tm, half), second),
                  pl.BlockSpec((tm, half), first), pl.BlockSpec((tm, half), second),
                  pl.BlockSpec((2 * half, D), lambda i: (0, 0)),
                  pl.BlockSpec((tm, D), first), pl.BlockSpec((tm, D), second),
                  pl.BlockSpec((1, D), lambda i: (0, 0)),
                  pl.BlockSpec((1, D), lambda i: (0, 0))],
        out_specs=[pl.BlockSpec((tm, D), lambda i: (i, 0)), pl.BlockSpec((tm, S, LANE), lambda i: (i, 0, 0))],
        out_shape=[jax.ShapeDtypeStruct((T, D), F32), jax.ShapeDtypeStruct((T, S, LANE), F32)],
        compiler_params=_params("arbitrary"),
        name="outproj_ln",
    )(h1p, h1s, h2p, h2s, w_bf, resp, ress, g.reshape(1, D), b.reshape(1, D))


def _hgrn_kernel(*refs, C, sb, tl, bb, hh, has_s0):
    if has_s0:
        qz_ref, fz_ref, iz_ref, gz_ref, lb_ref, go_ref, s0_ref, o_ref, sn_ref, st_ref = refs
    else:
        qz_ref, fz_ref, iz_ref, gz_ref, lb_ref, go_ref, o_ref, sn_ref, st_ref = refs
    t = pl.program_id(2)
    nsb = C // sb

    @pl.when(t == 0)
    def _():
        for s in range(bb):
            for h in range(hh):
                st_ref[s, h] = s0_ref[s, h].T if has_s0 else jnp.zeros((LANE, LANE), F32)

    lower = (_iota((C, C), 1) <= _iota((C, C), 0))
    lower_f = lower.astype(F32)

    def head_chunk(qz, fz, iz, gz, lb, go, st):
        q = _silu(qz)
        f = lb + (1.0 - lb) * _sigmoid(fz)
        kk = 1.0 - f
        i_bf = iz.astype(BF16)
        b = jnp.dot(lower_f, jnp.log(f), precision=HIGHEST, preferred_element_type=F32)
        b_last = b[C - 1:C, :]
        starts = [jnp.zeros((1, LANE), F32)] + [b[I * sb - 1:I * sb, :] for I in range(1, nsb)]
        lasts = [b[(I + 1) * sb - 1:(I + 1) * sb, :] for I in range(nsb)]
        blk = lambda a, I: a[I * sb:(I + 1) * sb, :]
        kd = [blk(kk, J) * jnp.exp(lasts[J] - blk(b, J)) for J in range(nsb)]
        att_rows = []
        for I in range(nsb):
            qd = blk(q, I) * jnp.exp(blk(b, I) - starts[I])
            parts = [kd[J] * jnp.exp(starts[I] - lasts[J]) for J in range(I)]
            parts.append(blk(kk, I) * jnp.exp(starts[I] - blk(b, I)))
            if I + 1 < nsb:
                parts.append(jnp.zeros(((nsb - I - 1) * sb, LANE), F32))
            kmat = jnp.concatenate(parts, axis=0) if len(parts) > 1 else parts[0]
            att_rows.append(lax.dot_general(qd.astype(BF16), kmat.astype(BF16), NT, preferred_element_type=F32))
        att = jnp.concatenate(att_rows, axis=0) if nsb > 1 else att_rows[0]
        att = jnp.where(lower, att, 0.0)
        o = jnp.dot(att.astype(BF16), i_bf, preferred_element_type=F32)
        o = o + lax.dot_general((q * jnp.exp(b)).astype(BF16), st.astype(BF16), NT, preferred_element_type=F32)
        kst = (kk * jnp.exp(b_last - b)).astype(BF16)
        st_new = st * jnp.exp(b_last) + lax.dot_general(i_bf, kst, TN, preferred_element_type=F32)
        ms = jnp.mean(o * o, axis=-1, keepdims=True)
        return o * lax.rsqrt(ms + LN_EPS) * go * _silu(gz), st_new

    def chunk(s, r0):
        qz, fz, iz, gz = (ref[pl.ds(r0, C), :] for ref in (qz_ref, fz_ref, iz_ref, gz_ref))
        lb, go = lb_ref[...], go_ref[...]
        head = lambda a, h: a[:, h * LANE:(h + 1) * LANE]
        res = [head_chunk(*(head(a, h) for a in (qz, fz, iz, gz, lb, go)), st_ref[s, h]) for h in range(hh)]
        o = jnp.concatenate([r[0] for r in res], axis=1) if hh > 1 else res[0][0]
        return o.astype(o_ref.dtype), jnp.stack([r[1] for r in res])

    if tl == C:
        res = [chunk(s, s * tl) for s in range(bb)]
        o_ref[...] = jnp.concatenate([r[0] for r in res], axis=0) if bb > 1 else res[0][0]
        st_ref[...] = jnp.stack([r[1] for r in res])
    else:
        assert bb == 1

        def body(c, carry):
            r0 = pl.multiple_of(c * C, C)
            o, st_new = chunk(0, r0)
            o_ref[pl.ds(r0, C), :] = o
            st_ref[0] = st_new
            return carry

        lax.fori_loop(0, tl // C, body, 0)

    @pl.when(t == pl.num_programs(2) - 1)
    def _():
        for s in range(bb):
            for h in range(hh):
                sn_ref[s, h] = st_ref[s, h].T


def hgrn(proj, row_start, n_seq, seq_len, lb, g_o, s0, H):
    C = min(HG_CHUNK, seq_len)
    sb = min(HG_SUB, C)
    assert seq_len % C == 0 and C % sb == 0
    if seq_len >= 512:
        tl, bb, hh = _tile(seq_len, (512,)), 1, _tile(H, (8, 4, 2, 1))
    else:
        assert seq_len == C
        tl, bb, hh = seq_len, _tile(n_seq, (16, 8, 4, 2, 1)), 1
    nt = seq_len // tl
    rows = bb * tl
    W = hh * LANE
    assert row_start % rows == 0
    r0 = row_start // rows
    col = lambda k: (lambda b, h, t: (r0 + b * nt + t, k * (H // hh) + h))
    in_specs = [pl.BlockSpec((rows, W), col(0)), pl.BlockSpec((rows, W), col(1)),
                pl.BlockSpec((rows, W), col(2)), pl.BlockSpec((rows, W), col(3)),
                pl.BlockSpec((1, W), lambda b, h, t: (0, h)),
                pl.BlockSpec((1, W), lambda b, h, t: (0, h))]
    args = [proj, proj, proj, proj, lb.reshape(1, H * LANE), g_o.reshape(1, H * LANE)]
    if s0 is not None:
        in_specs.append(pl.BlockSpec((bb, hh, LANE, LANE), lambda b, h, t: (b, h, 0, 0)))
        args.append(s0)
    return pl.pallas_call(
        functools.partial(_hgrn_kernel, C=C, sb=sb, tl=tl, bb=bb, hh=hh, has_s0=s0 is not None),
        grid=(n_seq // bb, H // hh, nt),
        in_specs=in_specs,
        out_specs=[pl.BlockSpec((rows, W), lambda b, h, t: (b * nt + t, h)),
                   pl.BlockSpec((bb, hh, LANE, LANE), lambda b, h, t: (b, h, 0, 0))],
        out_shape=[jax.ShapeDtypeStruct((n_seq * seq_len, H * LANE), BF16),
                   jax.ShapeDtypeStruct((n_seq, H, LANE, LANE), F32)],
        scratch_shapes=[pltpu.VMEM((bb, hh, LANE, LANE), F32)],
        compiler_params=_params("arbitrary", "arbitrary", "arbitrary"),
        name="hgrn",
    )(*args)


def _gmlp_kernel(uz_ref, vz_ref, w_ref, bt_ref, g_ref, b_ref, d_ref, v_ref, *, G, cs):
    n = uz_ref.shape[0]
    v = _layer_norm(_gelu(vz_ref[...]), g_ref[...], b_ref[...])
    v_ref[...] = v
    u = _gelu(uz_ref[...])
    t = _iota((n, n), 0)
    s = _iota((n, n), 1)
    keep = (s <= t) & (t // cs == s // cs)
    for g in range(G):
        w = jnp.where(keep, w_ref[g], 0.0).astype(BF16)
        vg = v[:, g * LANE:(g + 1) * LANE].astype(BF16)
        mixed = jnp.dot(w, vg, preferred_element_type=F32) + bt_ref[:, g:g + 1]
        d_ref[:, g * LANE:(g + 1) * LANE] = (u[:, g * LANE:(g + 1) * LANE] * mixed).astype(d_ref.dtype)


def gmlp(proj, row_start, n_rows, seq_len, w_s, b_s, g_v, b_v):
    G, n, _ = w_s.shape
    Dh = G * LANE
    cs = min(n, seq_len)
    assert n % cs == 0 and seq_len % cs == 0 and n_rows % n == 0 and row_start % n == 0
    if cs < n:
        w_s = jnp.tile(w_s[:, :cs, :cs], (1, n // cs, n // cs))
        b_s = jnp.tile(b_s[:, :cs], (1, n // cs))
    r0 = row_start // n
    return pl.pallas_call(
        functools.partial(_gmlp_kernel, G=G, cs=cs),
        grid=(n_rows // n,),
        in_specs=[pl.BlockSpec((n, Dh), lambda i: (r0 + i, 4)),
                  pl.BlockSpec((n, Dh), lambda i: (r0 + i, 5)),
                  pl.BlockSpec((G, n, n), lambda i: (0, 0, 0)),
                  pl.BlockSpec((n, G), lambda i: (0, 0)),
                  pl.BlockSpec((1, Dh), lambda i: (0, 0)),
                  pl.BlockSpec((1, Dh), lambda i: (0, 0))],
        out_specs=[pl.BlockSpec((n, Dh), lambda i: (i, 0)), pl.BlockSpec((n, Dh), lambda i: (i, 0))],
        out_shape=[jax.ShapeDtypeStruct((n_rows, Dh), BF16), jax.ShapeDtypeStruct((n_rows, Dh), F32)],
        compiler_params=_params("arbitrary"),
        name="gmlp",
    )(proj, proj, w_s, b_s.T, g_v.reshape(1, Dh), b_v.reshape(1, Dh))


def _router_kernel(x_ref, wt_ref, bt_ref, idx_ref, gate_ref, cnt_ref, run_ref, *, E):
    tm = x_ref.shape[0]
    per = E // N_GROUPS
    logits = lax.dot_general(wt_ref[...], x_ref[...], NT, precision=HIGHEST, preferred_element_type=F32)
    z = jnp.exp(logits - jnp.max(logits, axis=0, keepdims=True))
    probs = z / jnp.sum(z, axis=0, keepdims=True)
    sel = probs + bt_ref[...]
    io = _iota((per, tm), 0)

    def top2(sg):
        m1 = jnp.max(sg, axis=0, keepdims=True)
        i1 = jnp.min(jnp.where(sg == m1, io, per), axis=0, keepdims=True)
        rest = jnp.where(io == i1, -jnp.inf, sg)
        m2 = jnp.max(rest, axis=0, keepdims=True)
        i2 = jnp.min(jnp.where(rest == m2, io, per), axis=0, keepdims=True)
        return m1 + m2, i1, i2

    best, e1, e2 = top2(sel[0:per, :])
    for g in range(1, N_GROUPS):
        score, i1, i2 = top2(sel[g * per:(g + 1) * per, :])
        better = score > best
        best = jnp.where(better, score, best)
        e1 = jnp.where(better, i1 + g * per, e1)
        e2 = jnp.where(better, i2 + g * per, e2)
    eo = _iota((E, tm), 0)
    p1 = jnp.sum(jnp.where(eo == e1, probs, 0.0), axis=0, keepdims=True)
    p2 = jnp.sum(jnp.where(eo == e2, probs, 0.0), axis=0, keepdims=True)
    tot = p1 + p2
    gate_ref[...] = jnp.concatenate([p1 / tot, p2 / tot, jnp.zeros((6, tm), F32)], axis=0)

    @pl.when(pl.program_id(0) == 0)
    def _():
        run_ref[...] = jnp.zeros_like(run_ref)

    chosen = jnp.where(eo == e1, 1.0, jnp.where(eo == e2, 1.0, 0.0))
    earlier = (_iota((tm, tm), 0) < _iota((tm, tm), 1)).astype(BF16)
    before = jnp.dot(chosen.astype(BF16), earlier, preferred_element_type=F32) + run_ref[...]
    r1 = jnp.sum(jnp.where(eo == e1, before, 0.0), axis=0, keepdims=True).astype(jnp.int32)
    r2 = jnp.sum(jnp.where(eo == e2, before, 0.0), axis=0, keepdims=True).astype(jnp.int32)
    run_ref[...] = run_ref[...] + jnp.sum(chosen, axis=1, keepdims=True)
    idx_ref[...] = jnp.concatenate([e1, e2, r1, r2, jnp.zeros((4, tm), jnp.int32)], axis=0)
    cnt_ref[...] = jnp.broadcast_to(run_ref[...], cnt_ref.shape).astype(jnp.int32)


def router(x, w_router, b_router):
    T, D = x.shape
    E = w_router.shape[1]
    tm = _tile(T, (512, 256, 128))
    return pl.pallas_call(
        functools.partial(_router_kernel, E=E),
        grid=(T // tm,),
        in_specs=[pl.BlockSpec((tm, D), lambda i: (i, 0)),
                  pl.BlockSpec((E, D), lambda i: (0, 0)),
                  pl.BlockSpec((E, 1), lambda i: (0, 0))],
        out_specs=[pl.BlockSpec((8, tm), lambda i: (0, i)), pl.BlockSpec((8, tm), lambda i: (0, i)),
                   pl.BlockSpec((E, LANE), lambda i: (0, 0))],
        out_shape=[jax.ShapeDtypeStruct((8, T), jnp.int32), jax.ShapeDtypeStruct((8, T), F32),
                   jax.ShapeDtypeStruct((E, LANE), jnp.int32)],
        scratch_shapes=[pltpu.VMEM((E, 1), F32)],
        compiler_params=_params("arbitrary"),
        name="router",
    )(x, w_router.T, b_router.reshape(E, 1))


ROW_DMA_UNROLL = 8


def _start_row_copies(n, copy_of):
    def body(g, carry):
        for u in range(ROW_DMA_UNROLL):
            copy_of(g * ROW_DMA_UNROLL + u).start(priority=u % 2)
        return carry

    lax.fori_loop(0, n // ROW_DMA_UNROLL, body, 0)


def _wait_row_copies(hbm_ref, vmem_ref, sem, to_hbm):
    rows = hbm_ref.at[pl.ds(0, vmem_ref.shape[0])]
    (pltpu.make_async_copy(vmem_ref, rows, sem) if to_hbm else pltpu.make_async_copy(rows, vmem_ref, sem)).wait()


def _expert_kernel(te_ref, nv_ref, fresh_ref, src_ref, dst_ref, x_hbm, wg_ref, wu_ref, wd_ref, y_hbm,
                   wg_bf, wu_bf, wd_bf, xbuf, xs_bf, acc_ref, ybuf, gsem, ssem, zsem, *, tr, n_pairs):
    i = pl.program_id(0)
    j = pl.program_id(1)
    nv = nv_ref[0]
    valid = i < nv
    last = j == pl.num_programs(1) - 1
    slot = i % 2

    def gather(tile, s):
        _start_row_copies(tr, lambda r: pltpu.make_async_copy(x_hbm.at[src_ref[tile * tr + r]], xbuf.at[s, r],
                                                              gsem.at[s]))

    def scatter(tile, s):
        _start_row_copies(tr, lambda r: pltpu.make_async_copy(ybuf.at[s, r], y_hbm.at[dst_ref[tile * tr + r]],
                                                              ssem.at[s]))

    @pl.when((i == 0) & (j == 0))
    def _():
        ybuf[1] = jnp.zeros(ybuf.shape[1:], F32)
        spare = [pltpu.make_async_copy(ybuf.at[1], y_hbm.at[pl.ds(n_pairs + c * tr, tr)], zsem)
                 for c in range((y_hbm.shape[0] - n_pairs) // tr)]
        for cp in spare:
            cp.start()
        for cp in spare:
            cp.wait()

    @pl.when(valid & (j == 0))
    def _():
        @pl.when(i == 0)
        def _():
            gather(0, 0)

        @pl.when(i + 1 < nv)
        def _():
            gather(i + 1, 1 - slot)

        _wait_row_copies(x_hbm, xbuf.at[slot], gsem.at[slot], to_hbm=False)
        xs_bf[...] = _from_slabs(lambda k: xbuf[slot, :, k, :], xbuf.shape[2]).astype(BF16)

    @pl.when(valid & (fresh_ref[i] == 1))
    def _():
        wg_bf[j] = wg_ref[...].astype(BF16)
        wu_bf[j] = wu_ref[...].astype(BF16)
        wd_bf[j] = wd_ref[...].astype(BF16)

    @pl.when(valid)
    def _():
        x = xs_bf[...]
        hg = jnp.dot(x, wg_bf[j], preferred_element_type=F32)
        hu = jnp.dot(x, wu_bf[j], preferred_element_type=F32)
        h = (_silu(hg) * hu).astype(BF16)
        part = jnp.dot(h, wd_bf[j], preferred_element_type=F32)

        @pl.when(j == 0)
        def _():
            acc_ref[...] = part

        @pl.when(j > 0)
        def _():
            acc_ref[...] = acc_ref[...] + part

    @pl.when(valid & last)
    def _():
        _to_slabs(ybuf.at[slot], acc_ref[...])
        scatter(i, slot)

        @pl.when(i > 0)
        def _():
            _wait_row_copies(y_hbm, ybuf.at[1 - slot], ssem.at[1 - slot], to_hbm=True)

        @pl.when(i == nv - 1)
        def _():
            _wait_row_copies(y_hbm, ybuf.at[slot], ssem.at[slot], to_hbm=True)


def expert_ffn(x_slabs, src_rows, dst_rows, tile_expert, n_valid, fresh, w_gate, w_up, w_down, layer, tr, n_out):
    T, S, _ = x_slabs.shape
    n_pairs = 2 * T
    D = S * LANE
    R = src_rows.shape[0]
    De = w_gate.shape[-1]
    te = _tile(De, (512, 256, 128))
    nj = De // te
    assert tr % ROW_DMA_UNROLL == 0
    chunk = lambda i, j, fr: jnp.where(fr[i] == 1, j, nj - 1)
    grid_spec = pltpu.PrefetchScalarGridSpec(
        num_scalar_prefetch=5,
        grid=(R // tr, nj),
        in_specs=[pl.BlockSpec(memory_space=pl.ANY),
                  pl.BlockSpec((None, None, D, te), lambda i, j, e, nv, fr, s, d: (layer, e[i], 0, chunk(i, j, fr))),
                  pl.BlockSpec((None, None, D, te), lambda i, j, e, nv, fr, s, d: (layer, e[i], 0, chunk(i, j, fr))),
                  pl.BlockSpec((None, None, te, D), lambda i, j, e, nv, fr, s, d: (layer, e[i], chunk(i, j, fr), 0))],
        out_specs=pl.BlockSpec(memory_space=pl.ANY),
        scratch_shapes=[pltpu.VMEM((nj, D, te), BF16), pltpu.VMEM((nj, D, te), BF16), pltpu.VMEM((nj, te, D), BF16),
                        pltpu.VMEM((2, tr, S, LANE), F32), pltpu.VMEM((tr, D), BF16), pltpu.VMEM((tr, D), F32),
                        pltpu.VMEM((2, tr, S, LANE), F32),
                        pltpu.SemaphoreType.DMA((2,)), pltpu.SemaphoreType.DMA((2,)), pltpu.SemaphoreType.DMA(())],
    )
    assert (n_out - n_pairs) % tr == 0
    return pl.pallas_call(
        functools.partial(_expert_kernel, tr=tr, n_pairs=n_pairs),
        grid_spec=grid_spec,
        out_shape=jax.ShapeDtypeStruct((n_out, S, LANE), F32),
        compiler_params=_params("arbitrary", "arbitrary"),
        name="moe_experts",
    )(tile_expert, n_valid, fresh, src_rows, dst_rows, x_slabs, w_gate, w_up, w_down)


def _combine_ln_kernel(y_ref, gate_ref, res_ref, g_ref, b_ref, o1_ref, o2_ref, *obf_ref, alpha, n_first):
    i = pl.program_id(0)
    S = y_ref.shape[1] // 2
    g0 = gate_ref[:, 0:1]
    g1 = gate_ref[:, 1:2]
    moe = _from_slabs(lambda k: g0 * y_ref[:, k, :] + g1 * y_ref[:, S + k, :], S)
    z = _layer_norm(alpha * res_ref[...] + moe, g_ref[...], b_ref[...])

    @pl.when(i < n_first)
    def _():
        o1_ref[...] = z

    @pl.when(i >= n_first)
    def _():
        o2_ref[...] = z

    if obf_ref:
        obf_ref[0][...] = z.astype(BF16)


def combine_ln(y_pairs, gates_t, res, g, b, alpha, split_rows, with_bf16):
    T, D = res.shape
    S2 = y_pairs.shape[1]
    tm = _tile(math.gcd(split_rows, T - split_rows), (256, 128))
    row = lambda i: (i, 0)
    n1 = split_rows // tm
    assert 0 < n1 < T // tm
    out_specs = [pl.BlockSpec((tm, D), lambda i: (jnp.minimum(i, n1 - 1), 0)),
                 pl.BlockSpec((tm, D), lambda i: (jnp.maximum(i - n1, 0), 0))]
    out_shape = [jax.ShapeDtypeStruct((split_rows, D), F32), jax.ShapeDtypeStruct((T - split_rows, D), F32)]
    if with_bf16:
        out_specs.append(pl.BlockSpec((tm, D), row))
        out_shape.append(jax.ShapeDtypeStruct((T, D), BF16))
    return pl.pallas_call(
        functools.partial(_combine_ln_kernel, alpha=alpha, n_first=n1),
        grid=(T // tm,),
        in_specs=[pl.BlockSpec((tm, S2, LANE), lambda i: (i, 0, 0)),
                  pl.BlockSpec((tm, 8), row),
                  pl.BlockSpec((tm, D), row),
                  pl.BlockSpec((1, D), lambda i: (0, 0)),
                  pl.BlockSpec((1, D), lambda i: (0, 0))],
        out_specs=out_specs,
        out_shape=out_shape,
        compiler_params=_params("arbitrary"),
        name="moe_combine_ln",
    )(y_pairs, gates_t, res, g.reshape(1, D), b.reshape(1, D))


def _dispatch_plan(e_idx, counts, E, tr):
    T = e_idx.shape[1]
    pairs = 2 * T
    e_flat = e_idx[0:2, :].T.reshape(pairs)
    rank = e_idx[2:4, :].T.reshape(pairs)
    padded = ((counts + tr - 1) // tr) * tr
    pstart = jnp.cumsum(padded) - padded
    pos = (pstart[e_flat] + rank).astype(jnp.int32)
    n_tiles = pairs // tr + E
    R = n_tiles * tr
    pair_of_row = jnp.full((R,), -1, jnp.int32).at[pos].set(jnp.arange(pairs, dtype=jnp.int32), unique_indices=True)
    n_valid = (jnp.sum(padded) // tr).astype(jnp.int32).reshape(1)
    tile_end = (pstart + padded) // tr
    tiles = jnp.arange(n_tiles, dtype=jnp.int32)
    tile_e = jnp.minimum(jnp.sum((tiles[:, None] >= tile_end[None, :]).astype(jnp.int32), axis=1), E - 1)
    spare = pairs + tile_e * tr - pstart[tile_e] - counts[tile_e]
    spare_row = jnp.repeat(spare, tr) + jnp.arange(R, dtype=jnp.int32)
    is_pair = pair_of_row >= 0
    src_rows = jnp.where(is_pair, pair_of_row // 2, 0).astype(jnp.int32)
    dst_rows = jnp.where(is_pair, pair_of_row, spare_row).astype(jnp.int32)
    last_e = tile_e[jnp.maximum(n_valid[0] - 1, 0)]
    tile_e = jnp.where(tiles < n_valid[0], tile_e, last_e).astype(jnp.int32)
    fresh = jnp.concatenate([jnp.ones((1,), jnp.int32), (tile_e[1:] != tile_e[:-1]).astype(jnp.int32)])
    return src_rows, dst_rows, tile_e, n_valid, fresh, R


def moe_ln(x, x_slabs, x_res_scale, w_router, b_router, w_gate, w_up, w_down, layer, g, b, split_rows, with_bf16):
    T, D = x.shape
    S = D // LANE
    E = w_router.shape[1]
    tr = _tile(2 * T, (256, 128))
    e_idx, gates, counts = router(x, w_router, b_router)
    src_rows, dst_rows, tile_e, n_valid, fresh, R = _dispatch_plan(e_idx, counts[:, 0], E, tr)
    ys = expert_ffn(x_slabs, src_rows, dst_rows, tile_e, n_valid, fresh, w_gate, w_up, w_down, layer, tr, R)
    return combine_ln(ys.reshape(R // 2, 2 * S, LANE), gates.T, x, g, b, x_res_scale, split_rows, with_bf16)


def kernel(x_prompt, x_sample, cache_k, cache_v, cache_logf, state_conv, state_hgrn, page_table, w_in_even, b_fgate, w_dw, b_dw, g_cnorm, b_cnorm, w_out_even, w_in_odd, lb_logits, g_onorm, g_vnorm, b_vnorm, w_sgu, b_sgu, w_out_odd, ln1_g, ln1_b, ln2_g, ln2_b, w_router, b_router, w_gate, w_up, w_down):
    Bp, Lp, D = x_prompt.shape
    Bs, Ls, _ = x_sample.shape
    Dh = D // 2
    H = b_fgate.shape[1]
    HG = state_hgrn.shape[2]
    assert Dh == H * LANE and Dh == HG * LANE and Dh == w_sgu.shape[1] * LANE
    depth = ln1_g.shape[0]
    alpha = (2 * depth) ** 0.25
    Tp, Ts = Bp * Lp, Bs * Ls
    n_even = cache_k.shape[0]
    n_phys, page = cache_k.shape[1], cache_k.shape[2]

    lb_p = jax.nn.softmax(lb_logits.astype(F32), axis=0)
    lb_all = jnp.cumsum(lb_p, axis=0) - lb_p[0]

    res = (x_prompt.reshape(Tp, D), x_sample.reshape(Ts, D))
    x_bf = jnp.concatenate([r.astype(BF16) for r in res], axis=0)
    rin, tot = page_suffix(cache_logf.astype(F32).reshape(n_even * n_phys, page * H), H)
    rin = rin.reshape(n_even * n_phys, 1, page * H)
    tot = tot.reshape(n_even * n_phys, 1, page * H)

    out = {k: [] for k in ("kp", "vp", "lfp", "convp", "hgp", "ks", "vs", "lfs", "convs", "hgs", "mlpv")}
    for l in range(depth):
        j = l // 2
        if l % 2 == 0:
            w_in = w_in_even[j]
            ag = matmul_cols(x_bf, w_in, 0, 2 * Dh)
            q = matmul_cols(x_bf, w_in, 2 * Dh, Dh)
            k_p, k_s = matmul_cols(x_bf, w_in, 3 * Dh, Dh, split_rows=Tp)
            v_p, v_s = matmul_cols(x_bf, w_in, 4 * Dh, Dh, split_rows=Tp)
            lf, _, c, ct = logf_project(x_bf, w_in[:, 5 * Dh:], b_fgate[j], Lp)
            conv_args = (w_dw[j], b_dw[j], g_cnorm[j], b_cnorm[j])
            ca_p, cst_p = conv_prompt(ag, Bp, Lp, *conv_args)
            ca_s, cst_s = conv_sample(ag, Tp, Bs, Ls, state_conv[j], *conv_args)
            att_p = fox_prompt(q, k_p, v_p, c, ct, Bp, Lp, H)
            att_s = fox_sample(q, k_s, v_s, lf, Tp, Bs, Ls, cache_k, cache_v, rin, tot, j, page_table, H)
            h1 = (ca_p, ca_s)
            h2 = (att_p, att_s.astype(BF16))
            w_out = w_out_even[j]
            out["kp"].append(k_p.reshape(Bp, Lp, H, LANE))
            out["vp"].append(v_p.reshape(Bp, Lp, H, LANE))
            out["lfp"].append(lf[:Tp].reshape(Bp, Lp, H))
            out["convp"].append(cst_p)
            out["ks"].append(k_s.reshape(Bs, Ls, H, LANE))
            out["vs"].append(v_s.reshape(Bs, Ls, H, LANE))
            out["lfs"].append(lf[Tp:].reshape(Bs, Ls, H))
            out["convs"].append(cst_s)
        else:
            proj = matmul_cols(x_bf, w_in_odd[j], 0, 6 * Dh)
            o_p, s_p = hgrn(proj, 0, Bp, Lp, lb_all[l], g_onorm[j], None, HG)
            o_s, s_s = hgrn(proj, Tp, Bs, Ls, lb_all[l], g_onorm[j], state_hgrn[j].astype(F32), HG)
            mlp_args = (w_sgu[j], b_sgu[j], g_vnorm[j], b_vnorm[j])
            d_p, _ = gmlp(proj, 0, Tp, Lp, *mlp_args)
            d_s, v_s = gmlp(proj, Tp, Ts, Ls, *mlp_args)
            h1 = (o_p, o_s)
            h2 = (d_p, d_s)
            w_out = w_out_odd[j]
            out["hgp"].append(s_p)
            out["hgs"].append(s_s)
            out["mlpv"].append(v_s.reshape(Bs, Ls, Dh))
        x, x_slabs = outproj_ln(h1, h2, w_out.astype(BF16), res, ln1_g[l], ln1_b[l], alpha)
        more = l + 1 < depth
        res_p, res_s, *x_next = moe_ln(x, x_slabs, alpha, w_router, b_router, w_gate, w_up, w_down, l,
                                       ln2_g[l], ln2_b[l], Tp, more)
        res = (res_p, res_s)
        if more:
            x_bf = x_next[0]

    stack = lambda name: jnp.stack(out[name])
    return (res[0].reshape(Bp, Lp, D), res[1].reshape(Bs, Ls, D),
            stack("kp"), stack("vp"), stack("lfp"), stack("convp"), stack("hgp"),
            stack("ks"), stack("vs"), stack("lfs"), stack("convs"), stack("hgs"), stack("mlpv"))
```

```python
import functools
import math

import jax
import jax.numpy as jnp
from jax import lax
from jax.experimental import pallas as pl
from jax.experimental.pallas import tpu as pltpu

F32 = jnp.float32
BF16 = jnp.bfloat16
HIGHEST = lax.Precision.HIGHEST

LANE = 128
SUBLANES = 8
VMEM_LIMIT = 56 * 1024 * 1024
LN_EPS = 1e-5
HG_CHUNK = 64
HG_SUB = 16
CONV_HALO = 32
N_GROUPS = 4
NT = (((1,), (1,)), ((), ()))
TN = (((0,), (0,)), ((), ()))


def _params(*sem):
    return pltpu.CompilerParams(dimension_semantics=sem, vmem_limit_bytes=VMEM_LIMIT)


def _tile(n, prefs):
    for t in prefs:
        if n % t == 0:
            return t
    raise ValueError(f"no tile in {prefs} divides {n}")


def _sigmoid(x):
    return 1.0 / (1.0 + jnp.exp(-x))


def _silu(x):
    return x * _sigmoid(x)


def _gelu(x):
    return 0.5 * x * (1.0 + jnp.tanh(0.7978845608028654 * (x + 0.044715 * (x * x * x))))


def _log_sigmoid(x):
    return -(jnp.maximum(-x, 0.0) + jnp.log1p(jnp.exp(-jnp.abs(x))))


def _layer_norm(x, g, b):
    mu = jnp.mean(x, axis=-1, keepdims=True)
    xc = x - mu
    var = jnp.mean(xc * xc, axis=-1, keepdims=True)
    return xc * lax.rsqrt(var + LN_EPS) * g + b


def _iota(shape, dim):
    return lax.broadcasted_iota(jnp.int32, shape, dim)


def _matmul_kernel(x_ref, w_ref, o_ref, wbf_ref):
    @pl.when(pl.program_id(1) == 0)
    def _():
        wbf_ref[...] = w_ref[...].astype(BF16)

    o_ref[...] = jnp.dot(x_ref[...], wbf_ref[...], preferred_element_type=F32).astype(o_ref.dtype)


def _matmul_split_kernel(x_ref, w_ref, o1_ref, o2_ref, wbf_ref, *, n_first):
    i = pl.program_id(1)

    @pl.when(i == 0)
    def _():
        wbf_ref[...] = w_ref[...].astype(BF16)

    y = jnp.dot(x_ref[...], wbf_ref[...], preferred_element_type=F32)

    @pl.when(i < n_first)
    def _():
        o1_ref[...] = y

    @pl.when(i >= n_first)
    def _():
        o2_ref[...] = y


def matmul_cols(x_bf, w, col_start, n_cols, split_rows=None):
    M, K = x_bf.shape
    tm = _tile(M if split_rows is None else math.gcd(split_rows, M - split_rows), (1024, 512, 256, 128))
    tn = _tile(n_cols, (1024, 512, 256, 128))
    assert col_start % tn == 0 and M % tm == 0
    off = col_start // tn
    common = dict(
        grid=(n_cols // tn, M // tm),
        in_specs=[pl.BlockSpec((tm, K), lambda j, i: (i, 0)),
                  pl.BlockSpec((K, tn), lambda j, i: (0, j + off))],
        scratch_shapes=[pltpu.VMEM((K, tn), BF16)],
        compiler_params=_params("arbitrary", "arbitrary"),
    )
    if split_rows is None:
        return pl.pallas_call(
            _matmul_kernel,
            out_specs=pl.BlockSpec((tm, tn), lambda j, i: (i, j)),
            out_shape=jax.ShapeDtypeStruct((M, n_cols), F32),
            name="matmul_cols", **common,
        )(x_bf, w)
    n1 = split_rows // tm
    assert (M - split_rows) % tm == 0 and 0 < n1 < M // tm
    return pl.pallas_call(
        functools.partial(_matmul_split_kernel, n_first=n1),
        out_specs=[pl.BlockSpec((tm, tn), lambda j, i: (jnp.minimum(i, n1 - 1), j)),
                   pl.BlockSpec((tm, tn), lambda j, i: (jnp.maximum(i - n1, 0), j))],
        out_shape=[jax.ShapeDtypeStruct((split_rows, n_cols), F32),
                   jax.ShapeDtypeStruct((M - split_rows, n_cols), F32)],
        name="matmul_cols_split", **common,
    )(x_bf, w)


def _logf_kernel(x_ref, w_ref, wt_ref, b_ref, bt_ref, lf_ref, lft_ref, c_ref, ct_ref,
                 carry_ref, carryt_ref, *, tiles_per_seq):
    i = pl.program_id(0)
    x = x_ref[...]
    tm = x.shape[0]
    fz = jnp.dot(x, w_ref[...].astype(BF16), preferred_element_type=F32) + b_ref[...]
    fzt = lax.dot_general(wt_ref[...].astype(BF16), x, NT, preferred_element_type=F32) + bt_ref[...]
    lf = _log_sigmoid(fz)
    lft = _log_sigmoid(fzt)
    lf_ref[...] = lf
    lft_ref[...] = lft

    @pl.when(i % tiles_per_seq == 0)
    def _():
        carry_ref[...] = jnp.zeros_like(carry_ref)
        carryt_ref[...] = jnp.zeros_like(carryt_ref)

    row = _iota((tm, tm), 0)
    col = _iota((tm, tm), 1)
    lower = (col <= row).astype(F32)
    upper = (row <= col).astype(F32)
    c = jnp.dot(lower, lf, precision=HIGHEST, preferred_element_type=F32) + carry_ref[...]
    ct = jnp.dot(lft, upper, precision=HIGHEST, preferred_element_type=F32) + carryt_ref[...]
    c_ref[...] = c
    ct_ref[...] = ct
    carry_ref[...] = c[tm - 1:tm, :]
    carryt_ref[...] = ct[:, tm - 1:tm]


def logf_project(x_bf, w_f, b_f, seq_len):
    T, D = x_bf.shape
    H = w_f.shape[1]
    tm = _tile(seq_len, (512, 256, 128))
    assert T % tm == 0
    outs = pl.pallas_call(
        functools.partial(_logf_kernel, tiles_per_seq=seq_len // tm),
        grid=(T // tm,),
        in_specs=[pl.BlockSpec((tm, D), lambda i: (i, 0)),
                  pl.BlockSpec((D, H), lambda i: (0, 0)),
                  pl.BlockSpec((H, D), lambda i: (0, 0)),
                  pl.BlockSpec((1, H), lambda i: (0, 0)),
                  pl.BlockSpec((H, 1), lambda i: (0, 0))],
        out_specs=[pl.BlockSpec((tm, H), lambda i: (i, 0)),
                   pl.BlockSpec((H, tm), lambda i: (0, i)),
                   pl.BlockSpec((tm, H), lambda i: (i, 0)),
                   pl.BlockSpec((H, tm), lambda i: (0, i))],
        out_shape=[jax.ShapeDtypeStruct((T, H), F32), jax.ShapeDtypeStruct((H, T), F32),
                   jax.ShapeDtypeStruct((T, H), F32), jax.ShapeDtypeStruct((H, T), F32)],
        scratch_shapes=[pltpu.VMEM((1, H), F32), pltpu.VMEM((H, 1), F32)],
        compiler_params=_params("arbitrary"),
        name="logf_project",
    )(x_bf, w_f, w_f.T, b_f.reshape(1, H), b_f.reshape(H, 1))
    return outs


def _conv_tail(acc, bdw, g, b):
    return _silu(_layer_norm(acc + bdw, g, b))


def _conv_prompt_kernel(ag_ref, wdw_ref, bdw_ref, g_ref, b_ref, y_ref, st_ref, ext_ref, sh_ref, *, W, tt, C):
    t = pl.program_id(1)

    @pl.when(t == 0)
    def _():
        ext_ref[0:CONV_HALO, :] = jnp.zeros((CONV_HALO, C), F32)
        ext_ref[CONV_HALO + tt:CONV_HALO + tt + SUBLANES, :] = jnp.zeros((SUBLANES, C), F32)

    @pl.when(t > 0)
    def _():
        ext_ref[0:CONV_HALO, :] = ext_ref[tt:tt + CONV_HALO, :]

    u = ag_ref[:, 0:C] * _sigmoid(ag_ref[:, C:2 * C])
    ext_ref[CONV_HALO:CONV_HALO + tt, :] = u
    base = CONV_HALO - (W - 1)
    acc = None
    for r in range(SUBLANES):
        taps = [w for w in range(W) if (base + w) % SUBLANES == r]
        if not taps:
            continue
        sh_ref[...] = ext_ref[r:r + CONV_HALO + tt, :]
        for w in taps:
            a = base + w - r
            term = sh_ref[a:a + tt, :] * wdw_ref[w:w + 1, :]
            acc = term if acc is None else acc + term
    y_ref[...] = _conv_tail(acc, bdw_ref[...], g_ref[...], b_ref[...]).astype(y_ref.dtype)

    @pl.when(t == pl.num_programs(1) - 1)
    def _():
        st_ref[...] = ext_ref[CONV_HALO + tt - (W - 1):CONV_HALO + tt, :]


def conv_prompt(ag, n_seq, seq_len, w_dw, b_dw, g_n, b_n):
    W, C = w_dw.shape
    assert W - 1 <= CONV_HALO
    tt = _tile(seq_len, (256, 128))
    nt = seq_len // tt
    vec = lambda a: a.reshape(1, C)
    return pl.pallas_call(
        functools.partial(_conv_prompt_kernel, W=W, tt=tt, C=C),
        grid=(n_seq, nt),
        in_specs=[pl.BlockSpec((tt, 2 * C), lambda b, t: (b * nt + t, 0)),
                  pl.BlockSpec((W, C), lambda b, t: (0, 0)),
                  pl.BlockSpec((1, C), lambda b, t: (0, 0)),
                  pl.BlockSpec((1, C), lambda b, t: (0, 0)),
                  pl.BlockSpec((1, C), lambda b, t: (0, 0))],
        out_specs=[pl.BlockSpec((tt, C), lambda b, t: (b * nt + t, 0)),
                   pl.BlockSpec((None, W - 1, C), lambda b, t: (b, 0, 0))],
        out_shape=[jax.ShapeDtypeStruct((n_seq * seq_len, C), BF16),
                   jax.ShapeDtypeStruct((n_seq, W - 1, C), F32)],
        scratch_shapes=[pltpu.VMEM((CONV_HALO + tt + SUBLANES, C), F32), pltpu.VMEM((CONV_HALO + tt, C), F32)],
        compiler_params=_params("arbitrary", "arbitrary"),
        name="conv_prompt",
    )(ag, w_dw, vec(b_dw), vec(g_n), vec(b_n))


def _conv_sample_kernel(ag_ref, st_ref, wdw_ref, bdw_ref, g_ref, b_ref, y_ref, nst_ref, ext_ref, *, W, Ls, C, bb):
    u = ag_ref[:, 0:C] * _sigmoid(ag_ref[:, C:2 * C])
    ext_ref[:, 0:W - 1, :] = st_ref[...]
    ext_ref[:, W - 1:W - 1 + Ls, :] = u.reshape(bb, Ls, C)
    acc = ext_ref[:, 0:Ls, :] * wdw_ref[0:1, :]
    for w in range(1, W):
        acc = acc + ext_ref[:, w:w + Ls, :] * wdw_ref[w:w + 1, :]
    y = _conv_tail(acc.reshape(bb * Ls, C), bdw_ref[...], g_ref[...], b_ref[...])
    y_ref[...] = y.astype(y_ref.dtype)
    nst_ref[...] = ext_ref[:, Ls:Ls + W - 1, :]


def conv_sample(ag, row_start, n_seq, Ls, state, w_dw, b_dw, g_n, b_n):
    W, C = w_dw.shape
    bb = _tile(n_seq, (8, 4, 2, 1))
    rows = bb * Ls
    assert Ls % 8 == 0 and row_start % rows == 0
    r0 = row_start // rows
    vec = lambda a: a.reshape(1, C)
    return pl.pallas_call(
        functools.partial(_conv_sample_kernel, W=W, Ls=Ls, C=C, bb=bb),
        grid=(n_seq // bb,),
        in_specs=[pl.BlockSpec((rows, 2 * C), lambda i: (r0 + i, 0)),
                  pl.BlockSpec((bb, W - 1, C), lambda i: (i, 0, 0)),
                  pl.BlockSpec((W, C), lambda i: (0, 0)),
                  pl.BlockSpec((1, C), lambda i: (0, 0)),
                  pl.BlockSpec((1, C), lambda i: (0, 0)),
                  pl.BlockSpec((1, C), lambda i: (0, 0))],
        out_specs=[pl.BlockSpec((rows, C), lambda i: (i, 0)),
                   pl.BlockSpec((bb, W - 1, C), lambda i: (i, 0, 0))],
        out_shape=[jax.ShapeDtypeStruct((n_seq * Ls, C), BF16),
                   jax.ShapeDtypeStruct((n_seq, W - 1, C), F32)],
        scratch_shapes=[pltpu.VMEM((bb, W - 1 + Ls, C), F32)],
        compiler_params=_params("arbitrary"),
        name="conv_sample",
    )(ag, state, w_dw, vec(b_dw), vec(g_n), vec(b_n))


def _fox_prompt_kernel(q_ref, k_ref, v_ref, c_ref, ct_ref, o_ref, *, tq, scale, H):
    h = pl.program_id(1)
    qi = pl.program_id(2)
    q = (q_ref[...] * scale).astype(BF16)
    cq = jnp.sum(jnp.where(_iota((tq, H), 1) == h, c_ref[...], 0.0), axis=1, keepdims=True)

    def step(ki, carry, on_diagonal):
        m, l, acc = carry
        start = pl.multiple_of(ki * tq, tq)
        k = k_ref[pl.ds(start, tq), :].astype(BF16)
        v = v_ref[pl.ds(start, tq), :].astype(BF16)
        ck = ct_ref[pl.ds(h, 1), pl.ds(start, tq)]
        s = lax.dot_general(q, k, NT, preferred_element_type=F32) + cq - ck
        if on_diagonal:
            s = jnp.where(_iota((tq, tq), 0) >= _iota((tq, tq), 1), s, -jnp.inf)
        m_new = jnp.maximum(m, jnp.max(s, axis=1, keepdims=True))
        alpha = jnp.exp(m - m_new)
        p = jnp.exp(s - m_new)
        l = alpha * l + jnp.sum(p, axis=1, keepdims=True)
        acc = alpha * acc + jnp.dot(p.astype(BF16), v, preferred_element_type=F32)
        return m_new, l, acc

    init = (jnp.full((tq, 1), -jnp.inf, F32), jnp.zeros((tq, 1), F32), jnp.zeros((tq, LANE), F32))
    carry = lax.fori_loop(0, qi, lambda ki, c: step(ki, c, False), init)
    _, l, acc = step(qi, carry, True)
    o_ref[...] = (acc / l).astype(o_ref.dtype)


def fox_prompt(q, k, v, c, ct, n_seq, seq_len, H):
    tq = _tile(seq_len, (512, 256, 128))
    nq = seq_len // tq
    return pl.pallas_call(
        functools.partial(_fox_prompt_kernel, tq=tq, scale=LANE ** -0.5, H=H),
        grid=(n_seq, H, nq),
        in_specs=[pl.BlockSpec((tq, LANE), lambda b, h, i: (b * nq + i, h)),
                  pl.BlockSpec((seq_len, LANE), lambda b, h, i: (b, h)),
                  pl.BlockSpec((seq_len, LANE), lambda b, h, i: (b, h)),
                  pl.BlockSpec((tq, H), lambda b, h, i: (b * nq + i, 0)),
                  pl.BlockSpec((H, seq_len), lambda b, h, i: (0, b))],
        out_specs=pl.BlockSpec((tq, LANE), lambda b, h, i: (b * nq + i, h)),
        out_shape=jax.ShapeDtypeStruct((n_seq * seq_len, H * LANE), BF16),
        compiler_params=_params("arbitrary", "arbitrary", "arbitrary"),
        name="fox_prompt",
    )(q, k, v, c, ct)


def _split3_dot(x, w_bf):
    x1 = x.astype(BF16)
    r1 = x - x1.astype(F32)
    x2 = r1.astype(BF16)
    x3 = (r1 - x2.astype(F32)).astype(BF16)
    dot = lambda a: jnp.dot(a, w_bf, preferred_element_type=F32)
    return dot(x1) + dot(x2) + dot(x3)


def _page_suffix_kernel(lf_ref, rin_ref, tot_ref, later_ref, same_ref, *, H):
    n = lf_ref.shape[1]

    @pl.when(pl.program_id(0) == 0)
    def _():
        r = _iota((n, n), 0)
        c = _iota((n, n), 1)
        same = (r % H) == (c % H)
        later_ref[...] = jnp.where(same & (r > c), 1.0, 0.0).astype(BF16)
        same_ref[...] = jnp.where(same, 1.0, 0.0).astype(BF16)

    x = lf_ref[...]
    rin_ref[...] = _split3_dot(x, later_ref[...])
    tot_ref[...] = _split3_dot(x, same_ref[...])


def page_suffix(lf_pages, H):
    P, n = lf_pages.shape
    tp = _tile(P, (256, 128, 64, 32, 16, 8))
    return pl.pallas_call(
        functools.partial(_page_suffix_kernel, H=H),
        grid=(P // tp,),
        in_specs=[pl.BlockSpec((tp, n), lambda i: (i, 0))],
        out_specs=[pl.BlockSpec((tp, n), lambda i: (i, 0)), pl.BlockSpec((tp, n), lambda i: (i, 0))],
        out_shape=[jax.ShapeDtypeStruct((P, n), F32), jax.ShapeDtypeStruct((P, n), F32)],
        scratch_shapes=[pltpu.VMEM((n, n), BF16), pltpu.VMEM((n, n), BF16)],
        compiler_params=_params("arbitrary"),
        name="page_suffix",
    )(lf_pages)


def _fox_sample_kernel(*refs, H, Ls, scale, pps):
    pt_ref, q_ref, kn_ref, vn_ref, lfn_ref = refs[0:5]
    ck_refs = refs[5:5 + pps]
    cv_refs = refs[5 + pps:5 + 2 * pps]
    rin_refs = refs[5 + 2 * pps:5 + 3 * pps]
    tot_refs = refs[5 + 3 * pps:5 + 4 * pps]
    o_ref, qh_ref, a_ref, m_ref, l_ref, acc_ref, rc_ref = refs[5 + 4 * pps:]
    p = pl.program_id(1)
    HQ = H * Ls
    cols = ck_refs[0].shape[0] * H
    sel = (_iota((HQ, H), 0) // Ls == _iota((HQ, H), 1)).astype(F32)
    causal = _iota((HQ, Ls), 1) <= _iota((HQ, Ls), 0) % Ls

    def new_logf():
        return lax.dot_general(sel, lfn_ref[...], NT, precision=HIGHEST, preferred_element_type=F32)

    @pl.when(p == 0)
    def _():
        q = q_ref[...]
        qh_ref[...] = jnp.concatenate([q[:, h * LANE:(h + 1) * LANE] for h in range(H)], axis=0).astype(BF16)
        a_ref[...] = jnp.sum(jnp.where(causal, new_logf(), 0.0), axis=1, keepdims=True)
        m_ref[...] = jnp.full((HQ, 1), -jnp.inf, F32)
        l_ref[...] = jnp.zeros((HQ, 1), F32)
        acc_ref[...] = jnp.zeros((HQ, LANE), F32)
        rc_ref[...] = jnp.zeros((1, cols), F32)

    def update(scores, pvs):
        m_old = m_ref[...]
        m_new = m_old
        for s in scores:
            m_new = jnp.maximum(m_new, jnp.max(s, axis=1, keepdims=True))
        alpha = jnp.exp(m_old - m_new)
        l_new = alpha * l_ref[...]
        acc = alpha * acc_ref[...]
        for s, pv in zip(scores, pvs):
            pr = jnp.exp(s - m_new)
            l_new = l_new + jnp.sum(pr, axis=1, keepdims=True)
            acc = acc + pv(pr.astype(BF16))
        l_ref[...] = l_new
        acc_ref[...] = acc
        m_ref[...] = m_new

    qh = qh_ref[...]
    own_head = (_iota((HQ, cols), 0) // Ls) == (_iota((HQ, cols), 1) % H)
    rc = rc_ref[...]
    scores, pvs = [], []
    for s_ in range(pps):
        kx = ck_refs[s_][...].reshape(cols, LANE).astype(BF16)
        vx = cv_refs[s_][...].reshape(cols, LANE).astype(BF16)
        bias = a_ref[...] + (rin_refs[s_][...] + rc)
        rc = rc + tot_refs[s_][...]
        sc = lax.dot_general(qh, kx, NT, preferred_element_type=F32) * scale + bias
        scores.append(jnp.where(own_head, sc, -jnp.inf))
        pvs.append(lambda pr, vx=vx: jnp.dot(pr, vx, preferred_element_type=F32))
    rc_ref[...] = rc
    update(scores, pvs)

    @pl.when(p == pl.num_programs(1) - 1)
    def _():
        head = lambda a, h: a[:, h * LANE:(h + 1) * LANE].astype(BF16)
        rows = lambda a, h: a[h * Ls:(h + 1) * Ls, :]
        kn = kn_ref[...]
        vn = vn_ref[...]
        s2 = jnp.concatenate([lax.dot_general(rows(qh, h), head(kn, h), NT, preferred_element_type=F32)
                              for h in range(H)], axis=0)
        upto = (_iota((Ls, Ls), 0) <= _iota((Ls, Ls), 1)).astype(F32)
        cum = jnp.dot(new_logf(), upto, precision=HIGHEST, preferred_element_type=F32)
        s2 = jnp.where(causal, s2 * scale + (a_ref[...] - cum), -jnp.inf)
        update([s2], [lambda pr: jnp.concatenate(
            [jnp.dot(rows(pr, h), head(vn, h), preferred_element_type=F32) for h in range(H)], axis=0)])
        out = acc_ref[...] / l_ref[...]
        for h in range(H):
            o_ref[:, h * LANE:(h + 1) * LANE] = rows(out, h)


def fox_sample(q, k, v, lf, row_start, n_seq, Ls, cache_k, cache_v, rin, tot, layer, page_table, H):
    D = H * LANE
    n_pages = page_table.shape[1]
    n_phys, page = cache_k.shape[1], cache_k.shape[2]
    pps = _tile(n_pages, (8, 4, 2, 1))
    assert Ls % 8 == 0 and row_start % Ls == 0
    r0 = row_start // Ls
    HQ = H * Ls
    rows = lambda b, p, pt: (r0 + b, 0)

    def phys(s_):
        return lambda b, p, pt: pt[b * n_pages + (n_pages - 1 - (p * pps + s_))]

    kv_spec = lambda s_: pl.BlockSpec((None, None, page, H, LANE),
                                      lambda b, p, pt: (layer, phys(s_)(b, p, pt), 0, 0, 0))
    row_spec = lambda s_: pl.BlockSpec((None, 1, page * H),
                                       lambda b, p, pt: (layer * n_phys + phys(s_)(b, p, pt), 0, 0))
    grid_spec = pltpu.PrefetchScalarGridSpec(
        num_scalar_prefetch=1,
        grid=(n_seq, n_pages // pps),
        in_specs=([pl.BlockSpec((Ls, D), rows), pl.BlockSpec((Ls, D), lambda b, p, pt: (b, 0)),
                   pl.BlockSpec((Ls, D), lambda b, p, pt: (b, 0)), pl.BlockSpec((Ls, H), rows)]
                  + [kv_spec(s_) for s_ in range(pps)] + [kv_spec(s_) for s_ in range(pps)]
                  + [row_spec(s_) for s_ in range(pps)] + [row_spec(s_) for s_ in range(pps)]),
        out_specs=pl.BlockSpec((Ls, D), lambda b, p, pt: (b, 0)),
        scratch_shapes=[pltpu.VMEM((HQ, LANE), BF16), pltpu.VMEM((HQ, 1), F32), pltpu.VMEM((HQ, 1), F32),
                        pltpu.VMEM((HQ, 1), F32), pltpu.VMEM((HQ, LANE), F32), pltpu.VMEM((1, page * H), F32)],
    )
    return pl.pallas_call(
        functools.partial(_fox_sample_kernel, H=H, Ls=Ls, scale=LANE ** -0.5, pps=pps),
        grid_spec=grid_spec,
        out_shape=jax.ShapeDtypeStruct((n_seq * Ls, D), F32),
        compiler_params=_params("arbitrary", "arbitrary"),
        name="fox_sample",
    )(page_table.reshape(-1), q, k, v, lf, *([cache_k] * pps), *([cache_v] * pps),
      *([rin] * pps), *([tot] * pps))


def _transpose8(v):
    v = list(v)
    sub = _iota(v[0].shape, 1)
    for s in (4, 2, 1):
        keep = (sub & s) == 0
        for k in range(SUBLANES):
            if k & s:
                continue
            lo, hi = v[k], v[k + s]
            v[k] = jnp.where(keep, lo, pltpu.roll(hi, s, axis=1))
            v[k + s] = jnp.where(keep, pltpu.roll(lo, SUBLANES - s, axis=1), hi)
    return v


def _every8(t, n):
    return pl.ds(t, n // SUBLANES, stride=SUBLANES)


def _to_slabs(ref, x):
    n, S, _ = ref.shape
    if S % SUBLANES or n % SUBLANES:
        for k in range(S):
            ref[:, k, :] = x[:, k * LANE:(k + 1) * LANE]
        return
    for h in range(S // SUBLANES):
        v = [x[:, (SUBLANES * h + k) * LANE:(SUBLANES * h + k + 1) * LANE].reshape(n // SUBLANES, SUBLANES, LANE)
             for k in range(SUBLANES)]
        for t, w in enumerate(_transpose8(v)):
            ref[_every8(t, n), SUBLANES * h:SUBLANES * (h + 1), :] = w


def _from_slabs(load, n, S):
    if S % SUBLANES or n % SUBLANES:
        return jnp.concatenate([load(slice(None), k) for k in range(S)], axis=1)
    tiles = []
    for h in range(S // SUBLANES):
        w = [load(_every8(t, n), slice(SUBLANES * h, SUBLANES * (h + 1))) for t in range(SUBLANES)]
        tiles += [a.reshape(n, LANE) for a in _transpose8(w)]
    return jnp.concatenate(tiles, axis=1)


def _outproj_ln_kernel(h1p_ref, h1s_ref, h2p_ref, h2s_ref, w_ref, resp_ref, ress_ref, g_ref, b_ref, o_ref,
                       oslab_ref, *, alpha, half, n_first):
    first = pl.program_id(0) < n_first
    h1 = jnp.where(first, h1p_ref[...], h1s_ref[...])
    h2 = jnp.where(first, h2p_ref[...], h2s_ref[...])
    res = jnp.where(first, resp_ref[...], ress_ref[...])
    y = jnp.dot(h1, w_ref[0:half, :], preferred_element_type=F32)
    y = y + jnp.dot(h2, w_ref[half:2 * half, :], preferred_element_type=F32)
    z = _layer_norm(alpha * res + y, g_ref[...], b_ref[...])
    o_ref[...] = z
    _to_slabs(oslab_ref, z)


def outproj_ln(h1, h2, w_bf, res, g, b, alpha):
    (h1p, h1s), (h2p, h2s), (resp, ress) = h1, h2, res
    half = h1p.shape[1]
    D = resp.shape[1]
    T = resp.shape[0] + ress.shape[0]
    S = D // LANE
    tm = _tile(math.gcd(resp.shape[0], ress.shape[0]), (256, 128))
    n1 = resp.shape[0] // tm
    assert h1p.shape[0] == resp.shape[0] and h1s.shape[0] == ress.shape[0]
    first = lambda i: (jnp.minimum(i, n1 - 1), 0)
    second = lambda i: (jnp.maximum(i - n1, 0), 0)
    return pl.pallas_call(
        functools.partial(_outproj_ln_kernel, alpha=alpha, half=half, n_first=n1),
        grid=(T // tm,),
        in_specs=[pl.BlockSpec((tm, half), first), pl.BlockSpec((tm, half), second),
                  pl.BlockSpec((tm, half), first), pl.BlockSpec((tm, half), second),
                  pl.BlockSpec((2 * half, D), lambda i: (0, 0)),
                  pl.BlockSpec((tm, D), first), pl.BlockSpec((tm, D), second),
                  pl.BlockSpec((1, D), lambda i: (0, 0)),
                  pl.BlockSpec((1, D), lambda i: (0, 0))],
        out_specs=[pl.BlockSpec((tm, D), lambda i: (i, 0)), pl.BlockSpec((tm, S, LANE), lambda i: (i, 0, 0))],
        out_shape=[jax.ShapeDtypeStruct((T, D), F32), jax.ShapeDtypeStruct((T, S, LANE), F32)],
        compiler_params=_params("arbitrary"),
        name="outproj_ln",
    )(h1p, h1s, h2p, h2s, w_bf, resp, ress, g.reshape(1, D), b.reshape(1, D))


def _hgrn_kernel(*refs, C, sb, tl, bb, hh, has_s0):
    if has_s0:
        qz_ref, fz_ref, iz_ref, gz_ref, lb_ref, go_ref, s0_ref, o_ref, sn_ref, st_ref = refs
    else:
        qz_ref, fz_ref, iz_ref, gz_ref, lb_ref, go_ref, o_ref, sn_ref, st_ref = refs
    t = pl.program_id(2)
    nsb = C // sb

    @pl.when(t == 0)
    def _():
        for s in range(bb):
            for h in range(hh):
                st_ref[s, h] = s0_ref[s, h].T if has_s0 else jnp.zeros((LANE, LANE), F32)

    lower = (_iota((C, C), 1) <= _iota((C, C), 0))
    lower_f = lower.astype(F32)

    def head_chunk(qz, fz, iz, gz, lb, go, st):
        q = _silu(qz)
        f = lb + (1.0 - lb) * _sigmoid(fz)
        kk = 1.0 - f
        i_bf = iz.astype(BF16)
        b = jnp.dot(lower_f, jnp.log(f), precision=HIGHEST, preferred_element_type=F32)
        b_last = b[C - 1:C, :]
        starts = [jnp.zeros((1, LANE), F32)] + [b[I * sb - 1:I * sb, :] for I in range(1, nsb)]
        lasts = [b[(I + 1) * sb - 1:(I + 1) * sb, :] for I in range(nsb)]
        blk = lambda a, I: a[I * sb:(I + 1) * sb, :]
        kd = [blk(kk, J) * jnp.exp(lasts[J] - blk(b, J)) for J in range(nsb)]
        att_rows = []
        for I in range(nsb):
            qd = blk(q, I) * jnp.exp(blk(b, I) - starts[I])
            parts = [kd[J] * jnp.exp(starts[I] - lasts[J]) for J in range(I)]
            parts.append(blk(kk, I) * jnp.exp(starts[I] - blk(b, I)))
            if I + 1 < nsb:
                parts.append(jnp.zeros(((nsb - I - 1) * sb, LANE), F32))
            kmat = jnp.concatenate(parts, axis=0) if len(parts) > 1 else parts[0]
            att_rows.append(lax.dot_general(qd.astype(BF16), kmat.astype(BF16), NT, preferred_element_type=F32))
        att = jnp.concatenate(att_rows, axis=0) if nsb > 1 else att_rows[0]
        att = jnp.where(lower, att, 0.0)
        o = jnp.dot(att.astype(BF16), i_bf, preferred_element_type=F32)
        o = o + lax.dot_general((q * jnp.exp(b)).astype(BF16), st.astype(BF16), NT, preferred_element_type=F32)
        kst = (kk * jnp.exp(b_last - b)).astype(BF16)
        st_new = st * jnp.exp(b_last) + lax.dot_general(i_bf, kst, TN, preferred_element_type=F32)
        ms = jnp.mean(o * o, axis=-1, keepdims=True)
        return o * lax.rsqrt(ms + LN_EPS) * go * _silu(gz), st_new

    def chunk(s, r0):
        qz, fz, iz, gz = (ref[pl.ds(r0, C), :] for ref in (qz_ref, fz_ref, iz_ref, gz_ref))
        lb, go = lb_ref[...], go_ref[...]
        head = lambda a, h: a[:, h * LANE:(h + 1) * LANE]
        res = [head_chunk(*(head(a, h) for a in (qz, fz, iz, gz, lb, go)), st_ref[s, h]) for h in range(hh)]
        o = jnp.concatenate([r[0] for r in res], axis=1) if hh > 1 else res[0][0]
        return o.astype(o_ref.dtype), jnp.stack([r[1] for r in res])

    if tl == C:
        res = [chunk(s, s * tl) for s in range(bb)]
        o_ref[...] = jnp.concatenate([r[0] for r in res], axis=0) if bb > 1 else res[0][0]
        st_ref[...] = jnp.stack([r[1] for r in res])
    else:
        assert bb == 1

        def body(c, carry):
            r0 = pl.multiple_of(c * C, C)
            o, st_new = chunk(0, r0)
            o_ref[pl.ds(r0, C), :] = o
            st_ref[0] = st_new
            return carry

        lax.fori_loop(0, tl // C, body, 0)

    @pl.when(t == pl.num_programs(2) - 1)
    def _():
        for s in range(bb):
            for h in range(hh):
                sn_ref[s, h] = st_ref[s, h].T


def hgrn(proj, row_start, n_seq, seq_len, lb, g_o, s0, H):
    C = min(HG_CHUNK, seq_len)
    sb = min(HG_SUB, C)
    assert seq_len % C == 0 and C % sb == 0
    if seq_len >= 512:
        tl, bb, hh = _tile(seq_len, (512,)), 1, _tile(H, (8, 4, 2, 1))
    else:
        assert seq_len == C
        tl, bb, hh = seq_len, _tile(n_seq, (16, 8, 4, 2, 1)), 1
    nt = seq_len // tl
    rows = bb * tl
    W = hh * LANE
    assert row_start % rows == 0
    r0 = row_start // rows
    col = lambda k: (lambda b, h, t: (r0 + b * nt + t, k * (H // hh) + h))
    in_specs = [pl.BlockSpec((rows, W), col(0)), pl.BlockSpec((rows, W), col(1)),
                pl.BlockSpec((rows, W), col(2)), pl.BlockSpec((rows, W), col(3)),
                pl.BlockSpec((1, W), lambda b, h, t: (0, h)),
                pl.BlockSpec((1, W), lambda b, h, t: (0, h))]
    args = [proj, proj, proj, proj, lb.reshape(1, H * LANE), g_o.reshape(1, H * LANE)]
    if s0 is not None:
        in_specs.append(pl.BlockSpec((bb, hh, LANE, LANE), lambda b, h, t: (b, h, 0, 0)))
        args.append(s0)
    return pl.pallas_call(
        functools.partial(_hgrn_kernel, C=C, sb=sb, tl=tl, bb=bb, hh=hh, has_s0=s0 is not None),
        grid=(n_seq // bb, H // hh, nt),
        in_specs=in_specs,
        out_specs=[pl.BlockSpec((rows, W), lambda b, h, t: (b * nt + t, h)),
                   pl.BlockSpec((bb, hh, LANE, LANE), lambda b, h, t: (b, h, 0, 0))],
        out_shape=[jax.ShapeDtypeStruct((n_seq * seq_len, H * LANE), BF16),
                   jax.ShapeDtypeStruct((n_seq, H, LANE, LANE), F32)],
        scratch_shapes=[pltpu.VMEM((bb, hh, LANE, LANE), F32)],
        compiler_params=_params("arbitrary", "arbitrary", "arbitrary"),
        name="hgrn",
    )(*args)


def _gmlp_kernel(uz_ref, vz_ref, w_ref, bt_ref, g_ref, b_ref, d_ref, v_ref, *, G, cs):
    n = uz_ref.shape[0]
    v = _layer_norm(_gelu(vz_ref[...]), g_ref[...], b_ref[...])
    v_ref[...] = v
    u = _gelu(uz_ref[...])
    t = _iota((n, n), 0)
    s = _iota((n, n), 1)
    keep = (s <= t) & (t // cs == s // cs)
    for g in range(G):
        w = jnp.where(keep, w_ref[g], 0.0).astype(BF16)
        vg = v[:, g * LANE:(g + 1) * LANE].astype(BF16)
        mixed = jnp.dot(w, vg, preferred_element_type=F32) + bt_ref[:, g:g + 1]
        d_ref[:, g * LANE:(g + 1) * LANE] = (u[:, g * LANE:(g + 1) * LANE] * mixed).astype(d_ref.dtype)


def gmlp(proj, row_start, n_rows, seq_len, w_s, b_s, g_v, b_v):
    G, n, _ = w_s.shape
    Dh = G * LANE
    cs = min(n, seq_len)
    assert n % cs == 0 and seq_len % cs == 0 and n_rows % n == 0 and row_start % n == 0
    if cs < n:
        w_s = jnp.tile(w_s[:, :cs, :cs], (1, n // cs, n // cs))
        b_s = jnp.tile(b_s[:, :cs], (1, n // cs))
    r0 = row_start // n
    return pl.pallas_call(
        functools.partial(_gmlp_kernel, G=G, cs=cs),
        grid=(n_rows // n,),
        in_specs=[pl.BlockSpec((n, Dh), lambda i: (r0 + i, 4)),
                  pl.BlockSpec((n, Dh), lambda i: (r0 + i, 5)),
                  pl.BlockSpec((G, n, n), lambda i: (0, 0, 0)),
                  pl.BlockSpec((n, G), lambda i: (0, 0)),
                  pl.BlockSpec((1, Dh), lambda i: (0, 0)),
                  pl.BlockSpec((1, Dh), lambda i: (0, 0))],
        out_specs=[pl.BlockSpec((n, Dh), lambda i: (i, 0)), pl.BlockSpec((n, Dh), lambda i: (i, 0))],
        out_shape=[jax.ShapeDtypeStruct((n_rows, Dh), BF16), jax.ShapeDtypeStruct((n_rows, Dh), F32)],
        compiler_params=_params("arbitrary"),
        name="gmlp",
    )(proj, proj, w_s, b_s.T, g_v.reshape(1, Dh), b_v.reshape(1, Dh))


def _router_kernel(x_ref, wt_ref, bt_ref, idx_ref, gate_ref, cnt_ref, run_ref, *, E):
    tm = x_ref.shape[0]
    per = E // N_GROUPS
    logits = lax.dot_general(wt_ref[...], x_ref[...], NT, precision=HIGHEST, preferred_element_type=F32)
    z = jnp.exp(logits - jnp.max(logits, axis=0, keepdims=True))
    probs = z / jnp.sum(z, axis=0, keepdims=True)
    sel = probs + bt_ref[...]
    io = _iota((per, tm), 0)

    def top2(sg):
        m1 = jnp.max(sg, axis=0, keepdims=True)
        i1 = jnp.min(jnp.where(sg == m1, io, per), axis=0, keepdims=True)
        rest = jnp.where(io == i1, -jnp.inf, sg)
        m2 = jnp.max(rest, axis=0, keepdims=True)
        i2 = jnp.min(jnp.where(rest == m2, io, per), axis=0, keepdims=True)
        return m1 + m2, i1, i2

    best, e1, e2 = top2(sel[0:per, :])
    for g in range(1, N_GROUPS):
        score, i1, i2 = top2(sel[g * per:(g + 1) * per, :])
        better = score > best
        best = jnp.where(better, score, best)
        e1 = jnp.where(better, i1 + g * per, e1)
        e2 = jnp.where(better, i2 + g * per, e2)
    eo = _iota((E, tm), 0)
    p1 = jnp.sum(jnp.where(eo == e1, probs, 0.0), axis=0, keepdims=True)
    p2 = jnp.sum(jnp.where(eo == e2, probs, 0.0), axis=0, keepdims=True)
    tot = p1 + p2
    gate_ref[...] = jnp.concatenate([p1 / tot, p2 / tot, jnp.zeros((6, tm), F32)], axis=0)

    @pl.when(pl.program_id(0) == 0)
    def _():
        run_ref[...] = jnp.zeros_like(run_ref)

    chosen = jnp.where(eo == e1, 1.0, jnp.where(eo == e2, 1.0, 0.0))
    earlier = (_iota((tm, tm), 0) < _iota((tm, tm), 1)).astype(BF16)
    before = jnp.dot(chosen.astype(BF16), earlier, preferred_element_type=F32) + run_ref[...]
    r1 = jnp.sum(jnp.where(eo == e1, before, 0.0), axis=0, keepdims=True).astype(jnp.int32)
    r2 = jnp.sum(jnp.where(eo == e2, before, 0.0), axis=0, keepdims=True).astype(jnp.int32)
    run_ref[...] = run_ref[...] + jnp.sum(chosen, axis=1, keepdims=True)
    idx_ref[...] = jnp.concatenate([e1, e2, r1, r2, jnp.zeros((4, tm), jnp.int32)], axis=0)
    cnt_ref[...] = jnp.broadcast_to(run_ref[...], cnt_ref.shape).astype(jnp.int32)


def router(x, w_router, b_router):
    T, D = x.shape
    E = w_router.shape[1]
    tm = _tile(T, (512, 256, 128))
    return pl.pallas_call(
        functools.partial(_router_kernel, E=E),
        grid=(T // tm,),
        in_specs=[pl.BlockSpec((tm, D), lambda i: (i, 0)),
                  pl.BlockSpec((E, D), lambda i: (0, 0)),
                  pl.BlockSpec((E, 1), lambda i: (0, 0))],
        out_specs=[pl.BlockSpec((8, tm), lambda i: (0, i)), pl.BlockSpec((8, tm), lambda i: (0, i)),
                   pl.BlockSpec((E, LANE), lambda i: (0, 0))],
        out_shape=[jax.ShapeDtypeStruct((8, T), jnp.int32), jax.ShapeDtypeStruct((8, T), F32),
                   jax.ShapeDtypeStruct((E, LANE), jnp.int32)],
        scratch_shapes=[pltpu.VMEM((E, 1), F32)],
        compiler_params=_params("arbitrary"),
        name="router",
    )(x, w_router.T, b_router.reshape(E, 1))


ROW_DMA_UNROLL = 8


def _start_row_copies(n, copy_of):
    def body(g, carry):
        for u in range(ROW_DMA_UNROLL):
            copy_of(g * ROW_DMA_UNROLL + u).start(priority=u % 2)
        return carry

    lax.fori_loop(0, n // ROW_DMA_UNROLL, body, 0)


def _wait_row_copies(hbm_ref, vmem_ref, sem, to_hbm):
    rows = hbm_ref.at[pl.ds(0, vmem_ref.shape[0])]
    (pltpu.make_async_copy(vmem_ref, rows, sem) if to_hbm else pltpu.make_async_copy(rows, vmem_ref, sem)).wait()


def _expert_kernel(te_ref, nv_ref, fresh_ref, src_ref, dst_ref, x_hbm, wg_ref, wu_ref, wd_ref, y_hbm,
                   wg_bf, wu_bf, wd_bf, xbuf, xs_bf, acc_ref, ybuf, gsem, ssem, zsem, *, tr, n_pairs):
    i = pl.program_id(0)
    j = pl.program_id(1)
    nv = nv_ref[0]
    valid = i < nv
    last = j == pl.num_programs(1) - 1
    slot = i % 2

    def gather(tile, s):
        _start_row_copies(tr, lambda r: pltpu.make_async_copy(x_hbm.at[src_ref[tile * tr + r]], xbuf.at[s, r],
                                                              gsem.at[s]))

    def scatter(tile, s):
        _start_row_copies(tr, lambda r: pltpu.make_async_copy(ybuf.at[s, r], y_hbm.at[dst_ref[tile * tr + r]],
                                                              ssem.at[s]))

    @pl.when((i == 0) & (j == 0))
    def _():
        ybuf[1] = jnp.zeros(ybuf.shape[1:], F32)
        spare = [pltpu.make_async_copy(ybuf.at[1], y_hbm.at[pl.ds(n_pairs + c * tr, tr)], zsem)
                 for c in range((y_hbm.shape[0] - n_pairs) // tr)]
        for cp in spare:
            cp.start()
        for cp in spare:
            cp.wait()

    @pl.when(valid & (j == 0))
    def _():
        @pl.when(i == 0)
        def _():
            gather(0, 0)

        @pl.when(i + 1 < nv)
        def _():
            gather(i + 1, 1 - slot)

        _wait_row_copies(x_hbm, xbuf.at[slot], gsem.at[slot], to_hbm=False)
        xs_bf[...] = _from_slabs(lambda rows, cols: xbuf[slot, rows, cols, :], tr, xbuf.shape[2]).astype(BF16)

    @pl.when(valid & (fresh_ref[i] == 1))
    def _():
        wg_bf[j] = wg_ref[...].astype(BF16)
        wu_bf[j] = wu_ref[...].astype(BF16)
        wd_bf[j] = wd_ref[...].astype(BF16)

    @pl.when(valid)
    def _():
        x = xs_bf[...]
        hg = jnp.dot(x, wg_bf[j], preferred_element_type=F32)
        hu = jnp.dot(x, wu_bf[j], preferred_element_type=F32)
        h = (_silu(hg) * hu).astype(BF16)
        part = jnp.dot(h, wd_bf[j], preferred_element_type=F32)

        @pl.when(j == 0)
        def _():
            acc_ref[...] = part

        @pl.when(j > 0)
        def _():
            acc_ref[...] = acc_ref[...] + part

    @pl.when(valid & last)
    def _():
        _to_slabs(ybuf.at[slot], acc_ref[...])
        scatter(i, slot)

        @pl.when(i > 0)
        def _():
            _wait_row_copies(y_hbm, ybuf.at[1 - slot], ssem.at[1 - slot], to_hbm=True)

        @pl.when(i == nv - 1)
        def _():
            _wait_row_copies(y_hbm, ybuf.at[slot], ssem.at[slot], to_hbm=True)


def expert_ffn(x_slabs, src_rows, dst_rows, tile_expert, n_valid, fresh, w_gate, w_up, w_down, layer, tr, n_out):
    T, S, _ = x_slabs.shape
    n_pairs = 2 * T
    D = S * LANE
    R = src_rows.shape[0]
    De = w_gate.shape[-1]
    te = _tile(De, (512, 256, 128))
    nj = De // te
    assert tr % ROW_DMA_UNROLL == 0
    chunk = lambda i, j, fr: jnp.where(fr[i] == 1, j, nj - 1)
    grid_spec = pltpu.PrefetchScalarGridSpec(
        num_scalar_prefetch=5,
        grid=(R // tr, nj),
        in_specs=[pl.BlockSpec(memory_space=pl.ANY),
                  pl.BlockSpec((None, None, D, te), lambda i, j, e, nv, fr, s, d: (layer, e[i], 0, chunk(i, j, fr))),
                  pl.BlockSpec((None, None, D, te), lambda i, j, e, nv, fr, s, d: (layer, e[i], 0, chunk(i, j, fr))),
                  pl.BlockSpec((None, None, te, D), lambda i, j, e, nv, fr, s, d: (layer, e[i], chunk(i, j, fr), 0))],
        out_specs=pl.BlockSpec(memory_space=pl.ANY),
        scratch_shapes=[pltpu.VMEM((nj, D, te), BF16), pltpu.VMEM((nj, D, te), BF16), pltpu.VMEM((nj, te, D), BF16),
                        pltpu.VMEM((2, tr, S, LANE), F32), pltpu.VMEM((tr, D), BF16), pltpu.VMEM((tr, D), F32),
                        pltpu.VMEM((2, tr, S, LANE), F32),
                        pltpu.SemaphoreType.DMA((2,)), pltpu.SemaphoreType.DMA((2,)), pltpu.SemaphoreType.DMA(())],
    )
    assert (n_out - n_pairs) % tr == 0
    return pl.pallas_call(
        functools.partial(_expert_kernel, tr=tr, n_pairs=n_pairs),
        grid_spec=grid_spec,
        out_shape=jax.ShapeDtypeStruct((n_out, S, LANE), F32),
        compiler_params=_params("arbitrary", "arbitrary"),
        name="moe_experts",
    )(tile_expert, n_valid, fresh, src_rows, dst_rows, x_slabs, w_gate, w_up, w_down)


def _combine_ln_kernel(y_ref, gate_ref, res_ref, g_ref, b_ref, o1_ref, o2_ref, *obf_ref, alpha, n_first):
    i = pl.program_id(0)
    tm, S2, _ = y_ref.shape
    S = S2 // 2
    shift = lambda cols, by: cols + by if isinstance(cols, int) else slice(cols.start + by, cols.stop + by)
    y0 = _from_slabs(lambda rows, cols: y_ref[rows, cols, :], tm, S)
    y1 = _from_slabs(lambda rows, cols: y_ref[rows, shift(cols, S), :], tm, S)
    moe = gate_ref[:, 0:1] * y0 + gate_ref[:, 1:2] * y1
    z = _layer_norm(alpha * res_ref[...] + moe, g_ref[...], b_ref[...])

    @pl.when(i < n_first)
    def _():
        o1_ref[...] = z

    @pl.when(i >= n_first)
    def _():
        o2_ref[...] = z

    if obf_ref:
        obf_ref[0][...] = z.astype(BF16)


def combine_ln(y_pairs, gates_t, res, g, b, alpha, split_rows, with_bf16):
    T, D = res.shape
    S2 = y_pairs.shape[1]
    tm = _tile(math.gcd(split_rows, T - split_rows), (256, 128))
    row = lambda i: (i, 0)
    n1 = split_rows // tm
    assert 0 < n1 < T // tm
    out_specs = [pl.BlockSpec((tm, D), lambda i: (jnp.minimum(i, n1 - 1), 0)),
                 pl.BlockSpec((tm, D), lambda i: (jnp.maximum(i - n1, 0), 0))]
    out_shape = [jax.ShapeDtypeStruct((split_rows, D), F32), jax.ShapeDtypeStruct((T - split_rows, D), F32)]
    if with_bf16:
        out_specs.append(pl.BlockSpec((tm, D), row))
        out_shape.append(jax.ShapeDtypeStruct((T, D), BF16))
    return pl.pallas_call(
        functools.partial(_combine_ln_kernel, alpha=alpha, n_first=n1),
        grid=(T // tm,),
        in_specs=[pl.BlockSpec((tm, S2, LANE), lambda i: (i, 0, 0)),
                  pl.BlockSpec((tm, 8), row),
                  pl.BlockSpec((tm, D), row),
                  pl.BlockSpec((1, D), lambda i: (0, 0)),
                  pl.BlockSpec((1, D), lambda i: (0, 0))],
        out_specs=out_specs,
        out_shape=out_shape,
        compiler_params=_params("arbitrary"),
        name="moe_combine_ln",
    )(y_pairs, gates_t, res, g.reshape(1, D), b.reshape(1, D))


def _dispatch_plan(e_idx, counts, E, tr):
    T = e_idx.shape[1]
    pairs = 2 * T
    e_flat = e_idx[0:2, :].T.reshape(pairs)
    rank = e_idx[2:4, :].T.reshape(pairs)
    padded = ((counts + tr - 1) // tr) * tr
    pstart = jnp.cumsum(padded) - padded
    pos = (pstart[e_flat] + rank).astype(jnp.int32)
    n_tiles = pairs // tr + E
    R = n_tiles * tr
    pair_of_row = jnp.full((R,), -1, jnp.int32).at[pos].set(jnp.arange(pairs, dtype=jnp.int32), unique_indices=True)
    n_valid = (jnp.sum(padded) // tr).astype(jnp.int32).reshape(1)
    tile_end = (pstart + padded) // tr
    tiles = jnp.arange(n_tiles, dtype=jnp.int32)
    tile_e = jnp.minimum(jnp.sum((tiles[:, None] >= tile_end[None, :]).astype(jnp.int32), axis=1), E - 1)
    spare = pairs + tile_e * tr - pstart[tile_e] - counts[tile_e]
    spare_row = jnp.repeat(spare, tr) + jnp.arange(R, dtype=jnp.int32)
    is_pair = pair_of_row >= 0
    src_rows = jnp.where(is_pair, pair_of_row // 2, 0).astype(jnp.int32)
    dst_rows = jnp.where(is_pair, pair_of_row, spare_row).astype(jnp.int32)
    last_e = tile_e[jnp.maximum(n_valid[0] - 1, 0)]
    tile_e = jnp.where(tiles < n_valid[0], tile_e, last_e).astype(jnp.int32)
    fresh = jnp.concatenate([jnp.ones((1,), jnp.int32), (tile_e[1:] != tile_e[:-1]).astype(jnp.int32)])
    return src_rows, dst_rows, tile_e, n_valid, fresh, R


def moe_ln(x, x_slabs, x_res_scale, w_router, b_router, w_gate, w_up, w_down, layer, g, b, split_rows, with_bf16):
    T, D = x.shape
    S = D // LANE
    E = w_router.shape[1]
    tr = _tile(2 * T, (256, 128))
    e_idx, gates, counts = router(x, w_router, b_router)
    src_rows, dst_rows, tile_e, n_valid, fresh, R = _dispatch_plan(e_idx, counts[:, 0], E, tr)
    ys = expert_ffn(x_slabs, src_rows, dst_rows, tile_e, n_valid, fresh, w_gate, w_up, w_down, layer, tr, R)
    return combine_ln(ys.reshape(R // 2, 2 * S, LANE), gates.T, x, g, b, x_res_scale, split_rows, with_bf16)


def kernel(x_prompt, x_sample, cache_k, cache_v, cache_logf, state_conv, state_hgrn, page_table, w_in_even, b_fgate, w_dw, b_dw, g_cnorm, b_cnorm, w_out_even, w_in_odd, lb_logits, g_onorm, g_vnorm, b_vnorm, w_sgu, b_sgu, w_out_odd, ln1_g, ln1_b, ln2_g, ln2_b, w_router, b_router, w_gate, w_up, w_down):
    Bp, Lp, D = x_prompt.shape
    Bs, Ls, _ = x_sample.shape
    Dh = D // 2
    H = b_fgate.shape[1]
    HG = state_hgrn.shape[2]
    assert Dh == H * LANE and Dh == HG * LANE and Dh == w_sgu.shape[1] * LANE
    depth = ln1_g.shape[0]
    alpha = (2 * depth) ** 0.25
    Tp, Ts = Bp * Lp, Bs * Ls
    n_even = cache_k.shape[0]
    n_phys, page = cache_k.shape[1], cache_k.shape[2]

    lb_p = jax.nn.softmax(lb_logits.astype(F32), axis=0)
    lb_all = jnp.cumsum(lb_p, axis=0) - lb_p[0]

    res = (x_prompt.reshape(Tp, D), x_sample.reshape(Ts, D))
    x_bf = jnp.concatenate([r.astype(BF16) for r in res], axis=0)
    rin, tot = page_suffix(cache_logf.astype(F32).reshape(n_even * n_phys, page * H), H)
    rin = rin.reshape(n_even * n_phys, 1, page * H)
    tot = tot.reshape(n_even * n_phys, 1, page * H)

    out = {k: [] for k in ("kp", "vp", "lfp", "convp", "hgp", "ks", "vs", "lfs", "convs", "hgs", "mlpv")}
    for l in range(depth):
        j = l // 2
        if l % 2 == 0:
            w_in = w_in_even[j]
            ag = matmul_cols(x_bf, w_in, 0, 2 * Dh)
            q = matmul_cols(x_bf, w_in, 2 * Dh, Dh)
            k_p, k_s = matmul_cols(x_bf, w_in, 3 * Dh, Dh, split_rows=Tp)
            v_p, v_s = matmul_cols(x_bf, w_in, 4 * Dh, Dh, split_rows=Tp)
            lf, _, c, ct = logf_project(x_bf, w_in[:, 5 * Dh:], b_fgate[j], Lp)
            conv_args = (w_dw[j], b_dw[j], g_cnorm[j], b_cnorm[j])
            ca_p, cst_p = conv_prompt(ag, Bp, Lp, *conv_args)
            ca_s, cst_s = conv_sample(ag, Tp, Bs, Ls, state_conv[j], *conv_args)
            att_p = fox_prompt(q, k_p, v_p, c, ct, Bp, Lp, H)
            att_s = fox_sample(q, k_s, v_s, lf, Tp, Bs, Ls, cache_k, cache_v, rin, tot, j, page_table, H)
            h1 = (ca_p, ca_s)
            h2 = (att_p, att_s.astype(BF16))
            w_out = w_out_even[j]
            out["kp"].append(k_p.reshape(Bp, Lp, H, LANE))
            out["vp"].append(v_p.reshape(Bp, Lp, H, LANE))
            out["lfp"].append(lf[:Tp].reshape(Bp, Lp, H))
            out["convp"].append(cst_p)
            out["ks"].append(k_s.reshape(Bs, Ls, H, LANE))
            out["vs"].append(v_s.reshape(Bs, Ls, H, LANE))
            out["lfs"].append(lf[Tp:].reshape(Bs, Ls, H))
            out["convs"].append(cst_s)
        else:
            proj = matmul_cols(x_bf, w_in_odd[j], 0, 6 * Dh)
            o_p, s_p = hgrn(proj, 0, Bp, Lp, lb_all[l], g_onorm[j], None, HG)
            o_s, s_s = hgrn(proj, Tp, Bs, Ls, lb_all[l], g_onorm[j], state_hgrn[j].astype(F32), HG)
            mlp_args = (w_sgu[j], b_sgu[j], g_vnorm[j], b_vnorm[j])
            d_p, _ = gmlp(proj, 0, Tp, Lp, *mlp_args)
            d_s, v_s = gmlp(proj, Tp, Ts, Ls, *mlp_args)
            h1 = (o_p, o_s)
            h2 = (d_p, d_s)
            w_out = w_out_odd[j]
            out["hgp"].append(s_p)
            out["hgs"].append(s_s)
            out["mlpv"].append(v_s.reshape(Bs, Ls, Dh))
        x, x_slabs = outproj_ln(h1, h2, w_out.astype(BF16), res, ln1_g[l], ln1_b[l], alpha)
        more = l + 1 < depth
        res_p, res_s, *x_next = moe_ln(x, x_slabs, alpha, w_router, b_router, w_gate, w_up, w_down, l,
                                       ln2_g[l], ln2_b[l], Tp, more)
        res = (res_p, res_s)
        if more:
            x_bf = x_next[0]

    stack = lambda name: jnp.stack(out[name])
    return (res[0].reshape(Bp, Lp, D), res[1].reshape(Bs, Ls, D),
            stack("kp"), stack("vp"), stack("lfp"), stack("convp"), stack("hgp"),
            stack("ks"), stack("vs"), stack("lfs"), stack("convs"), stack("hgs"), stack("mlpv"))
```

```python
import functools
import math

import jax
import jax.numpy as jnp
from jax import lax
from jax.experimental import pallas as pl
from jax.experimental.pallas import tpu as pltpu

F32 = jnp.float32
BF16 = jnp.bfloat16
HIGHEST = lax.Precision.HIGHEST

LANE = 128
SUBLANES = 8
VMEM_LIMIT = 56 * 1024 * 1024
LN_EPS = 1e-5
HG_CHUNK = 64
HG_SUB = 16
CONV_HALO = 32
N_GROUPS = 4
NT = (((1,), (1,)), ((), ()))
TN = (((0,), (0,)), ((), ()))


def _params(*sem):
    return pltpu.CompilerParams(dimension_semantics=sem, vmem_limit_bytes=VMEM_LIMIT)


def _tile(n, prefs):
    for t in prefs:
        if n % t == 0:
            return t
    raise ValueError(f"no tile in {prefs} divides {n}")


def _sigmoid(x):
    return 1.0 / (1.0 + jnp.exp(-x))


def _silu(x):
    return x * _sigmoid(x)


def _gelu(x):
    return 0.5 * x * (1.0 + jnp.tanh(0.7978845608028654 * (x + 0.044715 * (x * x * x))))


def _log_sigmoid(x):
    return -(jnp.maximum(-x, 0.0) + jnp.log1p(jnp.exp(-jnp.abs(x))))


def _layer_norm(x, g, b):
    mu = jnp.mean(x, axis=-1, keepdims=True)
    xc = x - mu
    var = jnp.mean(xc * xc, axis=-1, keepdims=True)
    return xc * lax.rsqrt(var + LN_EPS) * g + b


def _iota(shape, dim):
    return lax.broadcasted_iota(jnp.int32, shape, dim)


def _matmul_kernel(x_ref, w_ref, o_ref, wbf_ref):
    @pl.when(pl.program_id(1) == 0)
    def _():
        wbf_ref[...] = w_ref[...].astype(BF16)

    o_ref[...] = jnp.dot(x_ref[...], wbf_ref[...], preferred_element_type=F32).astype(o_ref.dtype)


def _matmul_split_kernel(x_ref, w_ref, o1_ref, o2_ref, wbf_ref, *, n_first):
    i = pl.program_id(1)

    @pl.when(i == 0)
    def _():
        wbf_ref[...] = w_ref[...].astype(BF16)

    y = jnp.dot(x_ref[...], wbf_ref[...], preferred_element_type=F32)

    @pl.when(i < n_first)
    def _():
        o1_ref[...] = y

    @pl.when(i >= n_first)
    def _():
        o2_ref[...] = y


def matmul_cols(x_bf, w, layer, col_start, n_cols, split_rows=None):
    M, K = x_bf.shape
    tm = _tile(M if split_rows is None else math.gcd(split_rows, M - split_rows), (1024, 512, 256, 128))
    tn = _tile(n_cols, (1024, 512, 256, 128))
    assert col_start % tn == 0 and M % tm == 0
    off = col_start // tn
    common = dict(
        grid=(n_cols // tn, M // tm),
        in_specs=[pl.BlockSpec((tm, K), lambda j, i: (i, 0)),
                  pl.BlockSpec((None, K, tn), lambda j, i: (layer, 0, j + off))],
        scratch_shapes=[pltpu.VMEM((K, tn), BF16)],
        compiler_params=_params("arbitrary", "arbitrary"),
    )
    if split_rows is None:
        return pl.pallas_call(
            _matmul_kernel,
            out_specs=pl.BlockSpec((tm, tn), lambda j, i: (i, j)),
            out_shape=jax.ShapeDtypeStruct((M, n_cols), F32),
            name="matmul_cols", **common,
        )(x_bf, w)
    n1 = split_rows // tm
    assert (M - split_rows) % tm == 0 and 0 < n1 < M // tm
    return pl.pallas_call(
        functools.partial(_matmul_split_kernel, n_first=n1),
        out_specs=[pl.BlockSpec((tm, tn), lambda j, i: (jnp.minimum(i, n1 - 1), j)),
                   pl.BlockSpec((tm, tn), lambda j, i: (jnp.maximum(i - n1, 0), j))],
        out_shape=[jax.ShapeDtypeStruct((split_rows, n_cols), F32),
                   jax.ShapeDtypeStruct((M - split_rows, n_cols), F32)],
        name="matmul_cols_split", **common,
    )(x_bf, w)


def _logf_kernel(x_ref, w_ref, wt_ref, b_ref, bt_ref, lf_ref, lft_ref, c_ref, ct_ref,
                 carry_ref, carryt_ref, *, tiles_per_seq):
    i = pl.program_id(0)
    x = x_ref[...]
    tm = x.shape[0]
    fz = jnp.dot(x, w_ref[...].astype(BF16), preferred_element_type=F32) + b_ref[...]
    fzt = lax.dot_general(wt_ref[...].astype(BF16), x, NT, preferred_element_type=F32) + bt_ref[...]
    lf = _log_sigmoid(fz)
    lft = _log_sigmoid(fzt)
    lf_ref[...] = lf
    lft_ref[...] = lft

    @pl.when(i % tiles_per_seq == 0)
    def _():
        carry_ref[...] = jnp.zeros_like(carry_ref)
        carryt_ref[...] = jnp.zeros_like(carryt_ref)

    row = _iota((tm, tm), 0)
    col = _iota((tm, tm), 1)
    lower = (col <= row).astype(F32)
    upper = (row <= col).astype(F32)
    c = jnp.dot(lower, lf, precision=HIGHEST, preferred_element_type=F32) + carry_ref[...]
    ct = jnp.dot(lft, upper, precision=HIGHEST, preferred_element_type=F32) + carryt_ref[...]
    c_ref[...] = c
    ct_ref[...] = ct
    carry_ref[...] = c[tm - 1:tm, :]
    carryt_ref[...] = ct[:, tm - 1:tm]


def logf_project(x_bf, w_f, b_f, seq_len):
    T, D = x_bf.shape
    H = w_f.shape[1]
    tm = _tile(seq_len, (512, 256, 128))
    assert T % tm == 0
    outs = pl.pallas_call(
        functools.partial(_logf_kernel, tiles_per_seq=seq_len // tm),
        grid=(T // tm,),
        in_specs=[pl.BlockSpec((tm, D), lambda i: (i, 0)),
                  pl.BlockSpec((D, H), lambda i: (0, 0)),
                  pl.BlockSpec((H, D), lambda i: (0, 0)),
                  pl.BlockSpec((1, H), lambda i: (0, 0)),
                  pl.BlockSpec((H, 1), lambda i: (0, 0))],
        out_specs=[pl.BlockSpec((tm, H), lambda i: (i, 0)),
                   pl.BlockSpec((H, tm), lambda i: (0, i)),
                   pl.BlockSpec((tm, H), lambda i: (i, 0)),
                   pl.BlockSpec((H, tm), lambda i: (0, i))],
        out_shape=[jax.ShapeDtypeStruct((T, H), F32), jax.ShapeDtypeStruct((H, T), F32),
                   jax.ShapeDtypeStruct((T, H), F32), jax.ShapeDtypeStruct((H, T), F32)],
        scratch_shapes=[pltpu.VMEM((1, H), F32), pltpu.VMEM((H, 1), F32)],
        compiler_params=_params("arbitrary"),
        name="logf_project",
    )(x_bf, w_f, w_f.T, b_f.reshape(1, H), b_f.reshape(H, 1))
    return outs


def _conv_tail(acc, bdw, g, b):
    return _silu(_layer_norm(acc + bdw, g, b))


def _conv_prompt_kernel(ag_ref, wdw_ref, bdw_ref, g_ref, b_ref, y_ref, st_ref, ext_ref, sh_ref, *, W, tt, C):
    t = pl.program_id(1)

    @pl.when(t == 0)
    def _():
        ext_ref[0:CONV_HALO, :] = jnp.zeros((CONV_HALO, C), F32)
        ext_ref[CONV_HALO + tt:CONV_HALO + tt + SUBLANES, :] = jnp.zeros((SUBLANES, C), F32)

    @pl.when(t > 0)
    def _():
        ext_ref[0:CONV_HALO, :] = ext_ref[tt:tt + CONV_HALO, :]

    u = ag_ref[:, 0:C] * _sigmoid(ag_ref[:, C:2 * C])
    ext_ref[CONV_HALO:CONV_HALO + tt, :] = u
    base = CONV_HALO - (W - 1)
    acc = None
    for r in range(SUBLANES):
        taps = [w for w in range(W) if (base + w) % SUBLANES == r]
        if not taps:
            continue
        sh_ref[...] = ext_ref[r:r + CONV_HALO + tt, :]
        for w in taps:
            a = base + w - r
            term = sh_ref[a:a + tt, :] * wdw_ref[w:w + 1, :]
            acc = term if acc is None else acc + term
    y_ref[...] = _conv_tail(acc, bdw_ref[...], g_ref[...], b_ref[...]).astype(y_ref.dtype)

    @pl.when(t == pl.num_programs(1) - 1)
    def _():
        st_ref[...] = ext_ref[CONV_HALO + tt - (W - 1):CONV_HALO + tt, :]


def conv_prompt(ag, n_seq, seq_len, w_dw, b_dw, g_n, b_n):
    W, C = w_dw.shape
    assert W - 1 <= CONV_HALO
    tt = _tile(seq_len, (256, 128))
    nt = seq_len // tt
    vec = lambda a: a.reshape(1, C)
    return pl.pallas_call(
        functools.partial(_conv_prompt_kernel, W=W, tt=tt, C=C),
        grid=(n_seq, nt),
        in_specs=[pl.BlockSpec((tt, 2 * C), lambda b, t: (b * nt + t, 0)),
                  pl.BlockSpec((W, C), lambda b, t: (0, 0)),
                  pl.BlockSpec((1, C), lambda b, t: (0, 0)),
                  pl.BlockSpec((1, C), lambda b, t: (0, 0)),
                  pl.BlockSpec((1, C), lambda b, t: (0, 0))],
        out_specs=[pl.BlockSpec((tt, C), lambda b, t: (b * nt + t, 0)),
                   pl.BlockSpec((None, W - 1, C), lambda b, t: (b, 0, 0))],
        out_shape=[jax.ShapeDtypeStruct((n_seq * seq_len, C), BF16),
                   jax.ShapeDtypeStruct((n_seq, W - 1, C), F32)],
        scratch_shapes=[pltpu.VMEM((CONV_HALO + tt + SUBLANES, C), F32), pltpu.VMEM((CONV_HALO + tt, C), F32)],
        compiler_params=_params("arbitrary", "arbitrary"),
        name="conv_prompt",
    )(ag, w_dw, vec(b_dw), vec(g_n), vec(b_n))


def _conv_sample_kernel(ag_ref, st_ref, wdw_ref, bdw_ref, g_ref, b_ref, y_ref, nst_ref, ext_ref, *, W, Ls, C, bb):
    u = ag_ref[:, 0:C] * _sigmoid(ag_ref[:, C:2 * C])
    ext_ref[:, 0:W - 1, :] = st_ref[...]
    ext_ref[:, W - 1:W - 1 + Ls, :] = u.reshape(bb, Ls, C)
    acc = ext_ref[:, 0:Ls, :] * wdw_ref[0:1, :]
    for w in range(1, W):
        acc = acc + ext_ref[:, w:w + Ls, :] * wdw_ref[w:w + 1, :]
    y = _conv_tail(acc.reshape(bb * Ls, C), bdw_ref[...], g_ref[...], b_ref[...])
    y_ref[...] = y.astype(y_ref.dtype)
    nst_ref[...] = ext_ref[:, Ls:Ls + W - 1, :]


def conv_sample(ag, row_start, n_seq, Ls, state, w_dw, b_dw, g_n, b_n):
    W, C = w_dw.shape
    bb = _tile(n_seq, (8, 4, 2, 1))
    rows = bb * Ls
    assert Ls % 8 == 0 and row_start % rows == 0
    r0 = row_start // rows
    vec = lambda a: a.reshape(1, C)
    return pl.pallas_call(
        functools.partial(_conv_sample_kernel, W=W, Ls=Ls, C=C, bb=bb),
        grid=(n_seq // bb,),
        in_specs=[pl.BlockSpec((rows, 2 * C), lambda i: (r0 + i, 0)),
                  pl.BlockSpec((bb, W - 1, C), lambda i: (i, 0, 0)),
                  pl.BlockSpec((W, C), lambda i: (0, 0)),
                  pl.BlockSpec((1, C), lambda i: (0, 0)),
                  pl.BlockSpec((1, C), lambda i: (0, 0)),
                  pl.BlockSpec((1, C), lambda i: (0, 0))],
        out_specs=[pl.BlockSpec((rows, C), lambda i: (i, 0)),
                   pl.BlockSpec((bb, W - 1, C), lambda i: (i, 0, 0))],
        out_shape=[jax.ShapeDtypeStruct((n_seq * Ls, C), BF16),
                   jax.ShapeDtypeStruct((n_seq, W - 1, C), F32)],
        scratch_shapes=[pltpu.VMEM((bb, W - 1 + Ls, C), F32)],
        compiler_params=_params("arbitrary"),
        name="conv_sample",
    )(ag, state, w_dw, vec(b_dw), vec(g_n), vec(b_n))


def _fox_prompt_kernel(q_ref, k_ref, v_ref, c_ref, ct_ref, o_ref, *, tq, scale, H):
    h = pl.program_id(1)
    qi = pl.program_id(2)
    q = (q_ref[...] * scale).astype(BF16)
    cq = jnp.sum(jnp.where(_iota((tq, H), 1) == h, c_ref[...], 0.0), axis=1, keepdims=True)

    def step(ki, carry, on_diagonal):
        m, l, acc = carry
        start = pl.multiple_of(ki * tq, tq)
        k = k_ref[pl.ds(start, tq), :].astype(BF16)
        v = v_ref[pl.ds(start, tq), :].astype(BF16)
        ck = ct_ref[pl.ds(h, 1), pl.ds(start, tq)]
        s = lax.dot_general(q, k, NT, preferred_element_type=F32) + cq - ck
        if on_diagonal:
            s = jnp.where(_iota((tq, tq), 0) >= _iota((tq, tq), 1), s, -jnp.inf)
        m_new = jnp.maximum(m, jnp.max(s, axis=1, keepdims=True))
        alpha = jnp.exp(m - m_new)
        p = jnp.exp(s - m_new)
        l = alpha * l + jnp.sum(p, axis=1, keepdims=True)
        acc = alpha * acc + jnp.dot(p.astype(BF16), v, preferred_element_type=F32)
        return m_new, l, acc

    init = (jnp.full((tq, 1), -jnp.inf, F32), jnp.zeros((tq, 1), F32), jnp.zeros((tq, LANE), F32))
    carry = lax.fori_loop(0, qi, lambda ki, c: step(ki, c, False), init)
    _, l, acc = step(qi, carry, True)
    o_ref[...] = (acc / l).astype(o_ref.dtype)


def fox_prompt(q, k, v, c, ct, n_seq, seq_len, H):
    tq = _tile(seq_len, (512, 256, 128))
    nq = seq_len // tq
    return pl.pallas_call(
        functools.partial(_fox_prompt_kernel, tq=tq, scale=LANE ** -0.5, H=H),
        grid=(n_seq, H, nq),
        in_specs=[pl.BlockSpec((tq, LANE), lambda b, h, i: (b * nq + i, h)),
                  pl.BlockSpec((seq_len, LANE), lambda b, h, i: (b, h)),
                  pl.BlockSpec((seq_len, LANE), lambda b, h, i: (b, h)),
                  pl.BlockSpec((tq, H), lambda b, h, i: (b * nq + i, 0)),
                  pl.BlockSpec((H, seq_len), lambda b, h, i: (0, b))],
        out_specs=pl.BlockSpec((tq, LANE), lambda b, h, i: (b * nq + i, h)),
        out_shape=jax.ShapeDtypeStruct((n_seq * seq_len, H * LANE), BF16),
        compiler_params=_params("arbitrary", "arbitrary", "arbitrary"),
        name="fox_prompt",
    )(q, k, v, c, ct)


def _split3_dot(x, w_bf, w_left=False):
    x1 = x.astype(BF16)
    r1 = x - x1.astype(F32)
    x2 = r1.astype(BF16)
    x3 = (r1 - x2.astype(F32)).astype(BF16)
    if w_left:
        dot = lambda a: jnp.dot(w_bf, a, preferred_element_type=F32)
    else:
        dot = lambda a: jnp.dot(a, w_bf, preferred_element_type=F32)
    return dot(x1) + dot(x2) + dot(x3)


def _page_suffix_kernel(lf_ref, rin_ref, tot_ref, later_ref, same_ref, *, H):
    n = lf_ref.shape[1]

    @pl.when(pl.program_id(0) == 0)
    def _():
        r = _iota((n, n), 0)
        c = _iota((n, n), 1)
        same = (r % H) == (c % H)
        later_ref[...] = jnp.where(same & (r > c), 1.0, 0.0).astype(BF16)
        same_ref[...] = jnp.where(same, 1.0, 0.0).astype(BF16)

    x = lf_ref[...]
    rin_ref[...] = _split3_dot(x, later_ref[...])
    tot_ref[...] = _split3_dot(x, same_ref[...])


def page_suffix(lf_pages, H):
    P, n = lf_pages.shape
    tp = _tile(P, (256, 128, 64, 32, 16, 8))
    return pl.pallas_call(
        functools.partial(_page_suffix_kernel, H=H),
        grid=(P // tp,),
        in_specs=[pl.BlockSpec((tp, n), lambda i: (i, 0))],
        out_specs=[pl.BlockSpec((tp, n), lambda i: (i, 0)), pl.BlockSpec((tp, n), lambda i: (i, 0))],
        out_shape=[jax.ShapeDtypeStruct((P, n), F32), jax.ShapeDtypeStruct((P, n), F32)],
        scratch_shapes=[pltpu.VMEM((n, n), BF16), pltpu.VMEM((n, n), BF16)],
        compiler_params=_params("arbitrary"),
        name="page_suffix",
    )(lf_pages)


def _fox_sample_kernel(*refs, H, Ls, scale, pps):
    pt_ref, q_ref, kn_ref, vn_ref, lfn_ref = refs[0:5]
    ck_refs = refs[5:5 + pps]
    cv_refs = refs[5 + pps:5 + 2 * pps]
    rin_refs = refs[5 + 2 * pps:5 + 3 * pps]
    tot_refs = refs[5 + 3 * pps:5 + 4 * pps]
    o_ref, qh_ref, a_ref, m_ref, l_ref, acc_ref, rc_ref = refs[5 + 4 * pps:]
    p = pl.program_id(1)
    HQ = H * Ls
    cols = ck_refs[0].shape[0] * H
    sel = (_iota((HQ, H), 0) // Ls == _iota((HQ, H), 1)).astype(F32)
    causal = _iota((HQ, Ls), 1) <= _iota((HQ, Ls), 0) % Ls

    def new_logf():
        return lax.dot_general(sel, lfn_ref[...], NT, precision=HIGHEST, preferred_element_type=F32)

    @pl.when(p == 0)
    def _():
        q = q_ref[...]
        qh_ref[...] = jnp.concatenate([q[:, h * LANE:(h + 1) * LANE] for h in range(H)], axis=0).astype(BF16)
        a_ref[...] = jnp.sum(jnp.where(causal, new_logf(), 0.0), axis=1, keepdims=True)
        m_ref[...] = jnp.full((HQ, 1), -jnp.inf, F32)
        l_ref[...] = jnp.zeros((HQ, 1), F32)
        acc_ref[...] = jnp.zeros((HQ, LANE), F32)
        rc_ref[...] = jnp.zeros((1, cols), F32)

    def update(scores, pvs):
        m_old = m_ref[...]
        m_new = m_old
        for s in scores:
            m_new = jnp.maximum(m_new, jnp.max(s, axis=1, keepdims=True))
        alpha = jnp.exp(m_old - m_new)
        l_new = alpha * l_ref[...]
        acc = alpha * acc_ref[...]
        for s, pv in zip(scores, pvs):
            pr = jnp.exp(s - m_new)
            l_new = l_new + jnp.sum(pr, axis=1, keepdims=True)
            acc = acc + pv(pr.astype(BF16))
        l_ref[...] = l_new
        acc_ref[...] = acc
        m_ref[...] = m_new

    qh = qh_ref[...]
    own_head = (_iota((HQ, cols), 0) // Ls) == (_iota((HQ, cols), 1) % H)
    rc = rc_ref[...]
    scores, pvs = [], []
    for s_ in range(pps):
        kx = ck_refs[s_][...].reshape(cols, LANE).astype(BF16)
        vx = cv_refs[s_][...].reshape(cols, LANE).astype(BF16)
        bias = a_ref[...] + (rin_refs[s_][...] + rc)
        rc = rc + tot_refs[s_][...]
        sc = lax.dot_general(qh, kx, NT, preferred_element_type=F32) * scale + bias
        scores.append(jnp.where(own_head, sc, -jnp.inf))
        pvs.append(lambda pr, vx=vx: jnp.dot(pr, vx, preferred_element_type=F32))
    rc_ref[...] = rc
    update(scores, pvs)

    @pl.when(p == pl.num_programs(1) - 1)
    def _():
        head = lambda a, h: a[:, h * LANE:(h + 1) * LANE].astype(BF16)
        rows = lambda a, h: a[h * Ls:(h + 1) * Ls, :]
        kn = kn_ref[...]
        vn = vn_ref[...]
        s2 = jnp.concatenate([lax.dot_general(rows(qh, h), head(kn, h), NT, preferred_element_type=F32)
                              for h in range(H)], axis=0)
        upto = (_iota((Ls, Ls), 0) <= _iota((Ls, Ls), 1)).astype(F32)
        cum = jnp.dot(new_logf(), upto, precision=HIGHEST, preferred_element_type=F32)
        s2 = jnp.where(causal, s2 * scale + (a_ref[...] - cum), -jnp.inf)
        update([s2], [lambda pr: jnp.concatenate(
            [jnp.dot(rows(pr, h), head(vn, h), preferred_element_type=F32) for h in range(H)], axis=0)])
        out = acc_ref[...] / l_ref[...]
        for h in range(H):
            o_ref[:, h * LANE:(h + 1) * LANE] = rows(out, h)


def fox_sample(q, k, v, lf, row_start, n_seq, Ls, cache_k, cache_v, rin, tot, layer, page_table, H):
    D = H * LANE
    n_pages = page_table.shape[1]
    n_phys, page = cache_k.shape[1], cache_k.shape[2]
    pps = _tile(n_pages, (16, 8, 4, 2, 1))
    assert Ls % 8 == 0 and row_start % Ls == 0
    r0 = row_start // Ls
    HQ = H * Ls
    rows = lambda b, p, pt: (r0 + b, 0)

    def phys(s_):
        return lambda b, p, pt: pt[b * n_pages + (n_pages - 1 - (p * pps + s_))]

    kv_spec = lambda s_: pl.BlockSpec((None, None, page, H, LANE),
                                      lambda b, p, pt: (layer, phys(s_)(b, p, pt), 0, 0, 0))
    row_spec = lambda s_: pl.BlockSpec((None, 1, page * H),
                                       lambda b, p, pt: (layer * n_phys + phys(s_)(b, p, pt), 0, 0))
    grid_spec = pltpu.PrefetchScalarGridSpec(
        num_scalar_prefetch=1,
        grid=(n_seq, n_pages // pps),
        in_specs=([pl.BlockSpec((Ls, D), rows), pl.BlockSpec((Ls, D), lambda b, p, pt: (b, 0)),
                   pl.BlockSpec((Ls, D), lambda b, p, pt: (b, 0)), pl.BlockSpec((Ls, H), rows)]
                  + [kv_spec(s_) for s_ in range(pps)] + [kv_spec(s_) for s_ in range(pps)]
                  + [row_spec(s_) for s_ in range(pps)] + [row_spec(s_) for s_ in range(pps)]),
        out_specs=pl.BlockSpec((Ls, D), lambda b, p, pt: (b, 0)),
        scratch_shapes=[pltpu.VMEM((HQ, LANE), BF16), pltpu.VMEM((HQ, 1), F32), pltpu.VMEM((HQ, 1), F32),
                        pltpu.VMEM((HQ, 1), F32), pltpu.VMEM((HQ, LANE), F32), pltpu.VMEM((1, page * H), F32)],
    )
    return pl.pallas_call(
        functools.partial(_fox_sample_kernel, H=H, Ls=Ls, scale=LANE ** -0.5, pps=pps),
        grid_spec=grid_spec,
        out_shape=jax.ShapeDtypeStruct((n_seq * Ls, D), F32),
        compiler_params=_params("arbitrary", "arbitrary"),
        name="fox_sample",
    )(page_table.reshape(-1), q, k, v, lf, *([cache_k] * pps), *([cache_v] * pps),
      *([rin] * pps), *([tot] * pps))


def _transpose8(v):
    v = list(v)
    sub = _iota(v[0].shape, 1)
    for s in (4, 2, 1):
        keep = (sub & s) == 0
        for k in range(SUBLANES):
            if k & s:
                continue
            lo, hi = v[k], v[k + s]
            v[k] = jnp.where(keep, lo, pltpu.roll(hi, s, axis=1))
            v[k + s] = jnp.where(keep, pltpu.roll(lo, SUBLANES - s, axis=1), hi)
    return v


def _every8(t, n):
    return pl.ds(t, n // SUBLANES, stride=SUBLANES)


def _to_slabs(ref, x):
    n, S, _ = ref.shape
    if S % SUBLANES or n % SUBLANES:
        for k in range(S):
            ref[:, k, :] = x[:, k * LANE:(k + 1) * LANE]
        return
    for h in range(S // SUBLANES):
        v = [x[:, (SUBLANES * h + k) * LANE:(SUBLANES * h + k + 1) * LANE].reshape(n // SUBLANES, SUBLANES, LANE)
             for k in range(SUBLANES)]
        for t, w in enumerate(_transpose8(v)):
            ref[_every8(t, n), SUBLANES * h:SUBLANES * (h + 1), :] = w


def _from_slabs(load, n, S):
    if S % SUBLANES or n % SUBLANES:
        return jnp.concatenate([load(slice(None), k) for k in range(S)], axis=1)
    tiles = []
    for h in range(S // SUBLANES):
        w = [load(_every8(t, n), slice(SUBLANES * h, SUBLANES * (h + 1))) for t in range(SUBLANES)]
        tiles += [a.reshape(n, LANE) for a in _transpose8(w)]
    return jnp.concatenate(tiles, axis=1)


def _outproj_ln_kernel(h1p_ref, h1s_ref, h2p_ref, h2s_ref, w_ref, resp_ref, ress_ref, g_ref, b_ref, o_ref,
                       oslab_ref, *, alpha, half, n_first):
    first = pl.program_id(0) < n_first
    h1 = jnp.where(first, h1p_ref[...], h1s_ref[...])
    h2 = jnp.where(first, h2p_ref[...], h2s_ref[...])
    res = jnp.where(first, resp_ref[...], ress_ref[...])
    y = jnp.dot(h1, w_ref[0:half, :], preferred_element_type=F32)
    y = y + jnp.dot(h2, w_ref[half:2 * half, :], preferred_element_type=F32)
    z = _layer_norm(alpha * res + y, g_ref[...], b_ref[...])
    o_ref[...] = z
    _to_slabs(oslab_ref, z)


def outproj_ln(h1, h2, w_bf, res, g, b, alpha):
    (h1p, h1s), (h2p, h2s), (resp, ress) = h1, h2, res
    half = h1p.shape[1]
    D = resp.shape[1]
    T = resp.shape[0] + ress.shape[0]
    S = D // LANE
    tm = _tile(math.gcd(resp.shape[0], ress.shape[0]), (256, 128))
    n1 = resp.shape[0] // tm
    assert h1p.shape[0] == resp.shape[0] and h1s.shape[0] == ress.shape[0]
    first = lambda i: (jnp.minimum(i, n1 - 1), 0)
    second = lambda i: (jnp.maximum(i - n1, 0), 0)
    return pl.pallas_call(
        functools.partial(_outproj_ln_kernel, alpha=alpha, half=half, n_first=n1),
        grid=(T // tm,),
        in_specs=[pl.BlockSpec((tm, half), first), pl.BlockSpec((tm, half), second),
                  pl.BlockSpec((tm, half), first), pl.BlockSpec((tm, half), second),
                  pl.BlockSpec((2 * half, D), lambda i: (0, 0)),
                  pl.BlockSpec((tm, D), first), pl.BlockSpec((tm, D), second),
                  pl.BlockSpec((1, D), lambda i: (0, 0)),
                  pl.BlockSpec((1, D), lambda i: (0, 0))],
        out_specs=[pl.BlockSpec((tm, D), lambda i: (i, 0)), pl.BlockSpec((tm, S, LANE), lambda i: (i, 0, 0))],
        out_shape=[jax.ShapeDtypeStruct((T, D), F32), jax.ShapeDtypeStruct((T, S, LANE), F32)],
        compiler_params=_params("arbitrary"),
        name="outproj_ln",
    )(h1p, h1s, h2p, h2s, w_bf, resp, ress, g.reshape(1, D), b.reshape(1, D))


def _hgrn_kernel(*refs, C, sb, tl, bb, hh, has_s0):
    if has_s0:
        qz_ref, fz_ref, iz_ref, gz_ref, lb_ref, go_ref, s0_ref, o_ref, sn_ref, st_ref = refs
    else:
        qz_ref, fz_ref, iz_ref, gz_ref, lb_ref, go_ref, o_ref, sn_ref, st_ref = refs
    t = pl.program_id(2)
    nsb = C // sb

    @pl.when(t == 0)
    def _():
        for s in range(bb):
            for h in range(hh):
                st_ref[s, h] = s0_ref[s, h].T if has_s0 else jnp.zeros((LANE, LANE), F32)

    lower = (_iota((C, C), 1) <= _iota((C, C), 0))
    lower_bf = jnp.where(lower, 1.0, 0.0).astype(BF16)

    def head_chunk(qz, fz, iz, gz, lb, go, st):
        q = _silu(qz)
        f = lb + (1.0 - lb) * _sigmoid(fz)
        kk = 1.0 - f
        i_bf = iz.astype(BF16)
        b = _split3_dot(jnp.log(f), lower_bf, w_left=True)
        b_last = b[C - 1:C, :]
        starts = [jnp.zeros((1, LANE), F32)] + [b[I * sb - 1:I * sb, :] for I in range(1, nsb)]
        lasts = [b[(I + 1) * sb - 1:(I + 1) * sb, :] for I in range(nsb)]
        blk = lambda a, I: a[I * sb:(I + 1) * sb, :]
        kd = [blk(kk, J) * jnp.exp(lasts[J] - blk(b, J)) for J in range(nsb)]
        att_rows = []
        for I in range(nsb):
            qd = blk(q, I) * jnp.exp(blk(b, I) - starts[I])
            parts = [kd[J] * jnp.exp(starts[I] - lasts[J]) for J in range(I)]
            parts.append(blk(kk, I) * jnp.exp(starts[I] - blk(b, I)))
            if I + 1 < nsb:
                parts.append(jnp.zeros(((nsb - I - 1) * sb, LANE), F32))
            kmat = jnp.concatenate(parts, axis=0) if len(parts) > 1 else parts[0]
            att_rows.append(lax.dot_general(qd.astype(BF16), kmat.astype(BF16), NT, preferred_element_type=F32))
        att = jnp.concatenate(att_rows, axis=0) if nsb > 1 else att_rows[0]
        att = jnp.where(lower, att, 0.0)
        o = jnp.dot(att.astype(BF16), i_bf, preferred_element_type=F32)
        o = o + lax.dot_general((q * jnp.exp(b)).astype(BF16), st.astype(BF16), NT, preferred_element_type=F32)
        kst = (kk * jnp.exp(b_last - b)).astype(BF16)
        st_new = st * jnp.exp(b_last) + lax.dot_general(i_bf, kst, TN, preferred_element_type=F32)
        ms = jnp.mean(o * o, axis=-1, keepdims=True)
        return o * lax.rsqrt(ms + LN_EPS) * go * _silu(gz), st_new

    def chunk(s, r0):
        qz, fz, iz, gz = (ref[pl.ds(r0, C), :] for ref in (qz_ref, fz_ref, iz_ref, gz_ref))
        lb, go = lb_ref[...], go_ref[...]
        head = lambda a, h: a[:, h * LANE:(h + 1) * LANE]
        res = [head_chunk(*(head(a, h) for a in (qz, fz, iz, gz, lb, go)), st_ref[s, h]) for h in range(hh)]
        o = jnp.concatenate([r[0] for r in res], axis=1) if hh > 1 else res[0][0]
        return o.astype(o_ref.dtype), jnp.stack([r[1] for r in res])

    if tl == C:
        res = [chunk(s, s * tl) for s in range(bb)]
        o_ref[...] = jnp.concatenate([r[0] for r in res], axis=0) if bb > 1 else res[0][0]
        st_ref[...] = jnp.stack([r[1] for r in res])
    else:
        assert bb == 1

        def body(c, carry):
            r0 = pl.multiple_of(c * C, C)
            o, st_new = chunk(0, r0)
            o_ref[pl.ds(r0, C), :] = o
            st_ref[0] = st_new
            return carry

        lax.fori_loop(0, tl // C, body, 0)

    @pl.when(t == pl.num_programs(2) - 1)
    def _():
        for s in range(bb):
            for h in range(hh):
                sn_ref[s, h] = st_ref[s, h].T


def hgrn(proj, row_start, n_seq, seq_len, lb, g_o, s0, H):
    C = min(HG_CHUNK, seq_len)
    sb = min(HG_SUB, C)
    assert seq_len % C == 0 and C % sb == 0
    if seq_len >= 512:
        tl, bb, hh = _tile(seq_len, (512,)), 1, _tile(H, (8, 4, 2, 1))
    else:
        assert seq_len == C
        tl, bb, hh = seq_len, _tile(n_seq, (16, 8, 4, 2, 1)), 1
    nt = seq_len // tl
    rows = bb * tl
    W = hh * LANE
    assert row_start % rows == 0
    r0 = row_start // rows
    col = lambda k: (lambda b, h, t: (r0 + b * nt + t, k * (H // hh) + h))
    in_specs = [pl.BlockSpec((rows, W), col(0)), pl.BlockSpec((rows, W), col(1)),
                pl.BlockSpec((rows, W), col(2)), pl.BlockSpec((rows, W), col(3)),
                pl.BlockSpec((1, W), lambda b, h, t: (0, h)),
                pl.BlockSpec((1, W), lambda b, h, t: (0, h))]
    args = [proj, proj, proj, proj, lb.reshape(1, H * LANE), g_o.reshape(1, H * LANE)]
    if s0 is not None:
        in_specs.append(pl.BlockSpec((bb, hh, LANE, LANE), lambda b, h, t: (b, h, 0, 0)))
        args.append(s0)
    return pl.pallas_call(
        functools.partial(_hgrn_kernel, C=C, sb=sb, tl=tl, bb=bb, hh=hh, has_s0=s0 is not None),
        grid=(n_seq // bb, H // hh, nt),
        in_specs=in_specs,
        out_specs=[pl.BlockSpec((rows, W), lambda b, h, t: (b * nt + t, h)),
                   pl.BlockSpec((bb, hh, LANE, LANE), lambda b, h, t: (b, h, 0, 0))],
        out_shape=[jax.ShapeDtypeStruct((n_seq * seq_len, H * LANE), BF16),
                   jax.ShapeDtypeStruct((n_seq, H, LANE, LANE), F32)],
        scratch_shapes=[pltpu.VMEM((bb, hh, LANE, LANE), F32)],
        compiler_params=_params("arbitrary", "arbitrary", "arbitrary"),
        name="hgrn",
    )(*args)


def _gmlp_kernel(uz_ref, vz_ref, w_ref, bt_ref, g_ref, b_ref, d_ref, v_ref, *, G, cs):
    n = uz_ref.shape[0]
    v = _layer_norm(_gelu(vz_ref[...]), g_ref[...], b_ref[...])
    v_ref[...] = v
    u = _gelu(uz_ref[...])
    t = _iota((n, n), 0)
    s = _iota((n, n), 1)
    keep = (s <= t) & (t // cs == s // cs)
    for g in range(G):
        w = jnp.where(keep, w_ref[g], 0.0).astype(BF16)
        vg = v[:, g * LANE:(g + 1) * LANE].astype(BF16)
        mixed = jnp.dot(w, vg, preferred_element_type=F32) + bt_ref[:, g:g + 1]
        d_ref[:, g * LANE:(g + 1) * LANE] = (u[:, g * LANE:(g + 1) * LANE] * mixed).astype(d_ref.dtype)


def gmlp(proj, row_start, n_rows, seq_len, w_s, b_s, g_v, b_v):
    G, n, _ = w_s.shape
    Dh = G * LANE
    cs = min(n, seq_len)
    assert n % cs == 0 and seq_len % cs == 0 and n_rows % n == 0 and row_start % n == 0
    if cs < n:
        w_s = jnp.tile(w_s[:, :cs, :cs], (1, n // cs, n // cs))
        b_s = jnp.tile(b_s[:, :cs], (1, n // cs))
    r0 = row_start // n
    return pl.pallas_call(
        functools.partial(_gmlp_kernel, G=G, cs=cs),
        grid=(n_rows // n,),
        in_specs=[pl.BlockSpec((n, Dh), lambda i: (r0 + i, 4)),
                  pl.BlockSpec((n, Dh), lambda i: (r0 + i, 5)),
                  pl.BlockSpec((G, n, n), lambda i: (0, 0, 0)),
                  pl.BlockSpec((n, G), lambda i: (0, 0)),
                  pl.BlockSpec((1, Dh), lambda i: (0, 0)),
                  pl.BlockSpec((1, Dh), lambda i: (0, 0))],
        out_specs=[pl.BlockSpec((n, Dh), lambda i: (i, 0)), pl.BlockSpec((n, Dh), lambda i: (i, 0))],
        out_shape=[jax.ShapeDtypeStruct((n_rows, Dh), BF16), jax.ShapeDtypeStruct((n_rows, Dh), F32)],
        compiler_params=_params("arbitrary"),
        name="gmlp",
    )(proj, proj, w_s, b_s.T, g_v.reshape(1, Dh), b_v.reshape(1, Dh))


def _router_kernel(x_ref, wt_ref, bt_ref, idx_ref, gate_ref, cnt_ref, run_ref, *, E):
    tm = x_ref.shape[0]
    per = E // N_GROUPS
    logits = lax.dot_general(wt_ref[...], x_ref[...], NT, precision=HIGHEST, preferred_element_type=F32)
    z = jnp.exp(logits - jnp.max(logits, axis=0, keepdims=True))
    probs = z / jnp.sum(z, axis=0, keepdims=True)
    sel = probs + bt_ref[...]
    io = _iota((per, tm), 0)

    def top2(sg):
        m1 = jnp.max(sg, axis=0, keepdims=True)
        i1 = jnp.min(jnp.where(sg == m1, io, per), axis=0, keepdims=True)
        rest = jnp.where(io == i1, -jnp.inf, sg)
        m2 = jnp.max(rest, axis=0, keepdims=True)
        i2 = jnp.min(jnp.where(rest == m2, io, per), axis=0, keepdims=True)
        return m1 + m2, i1, i2

    best, e1, e2 = top2(sel[0:per, :])
    for g in range(1, N_GROUPS):
        score, i1, i2 = top2(sel[g * per:(g + 1) * per, :])
        better = score > best
        best = jnp.where(better, score, best)
        e1 = jnp.where(better, i1 + g * per, e1)
        e2 = jnp.where(better, i2 + g * per, e2)
    eo = _iota((E, tm), 0)
    p1 = jnp.sum(jnp.where(eo == e1, probs, 0.0), axis=0, keepdims=True)
    p2 = jnp.sum(jnp.where(eo == e2, probs, 0.0), axis=0, keepdims=True)
    tot = p1 + p2
    gate_ref[...] = jnp.concatenate([p1 / tot, p2 / tot, jnp.zeros((6, tm), F32)], axis=0)

    @pl.when(pl.program_id(0) == 0)
    def _():
        run_ref[...] = jnp.zeros_like(run_ref)

    chosen = jnp.where(eo == e1, 1.0, jnp.where(eo == e2, 1.0, 0.0))
    earlier = (_iota((tm, tm), 0) < _iota((tm, tm), 1)).astype(BF16)
    before = jnp.dot(chosen.astype(BF16), earlier, preferred_element_type=F32) + run_ref[...]
    r1 = jnp.sum(jnp.where(eo == e1, before, 0.0), axis=0, keepdims=True).astype(jnp.int32)
    r2 = jnp.sum(jnp.where(eo == e2, before, 0.0), axis=0, keepdims=True).astype(jnp.int32)
    run_ref[...] = run_ref[...] + jnp.sum(chosen, axis=1, keepdims=True)
    idx_ref[...] = jnp.concatenate([e1, e2, r1, r2, jnp.zeros((4, tm), jnp.int32)], axis=0)
    cnt_ref[...] = jnp.broadcast_to(run_ref[...], cnt_ref.shape).astype(jnp.int32)


def router(x, w_router, b_router):
    T, D = x.shape
    E = w_router.shape[1]
    tm = _tile(T, (512, 256, 128))
    return pl.pallas_call(
        functools.partial(_router_kernel, E=E),
        grid=(T // tm,),
        in_specs=[pl.BlockSpec((tm, D), lambda i: (i, 0)),
                  pl.BlockSpec((E, D), lambda i: (0, 0)),
                  pl.BlockSpec((E, 1), lambda i: (0, 0))],
        out_specs=[pl.BlockSpec((8, tm), lambda i: (0, i)), pl.BlockSpec((8, tm), lambda i: (0, i)),
                   pl.BlockSpec((E, LANE), lambda i: (0, 0))],
        out_shape=[jax.ShapeDtypeStruct((8, T), jnp.int32), jax.ShapeDtypeStruct((8, T), F32),
                   jax.ShapeDtypeStruct((E, LANE), jnp.int32)],
        scratch_shapes=[pltpu.VMEM((E, 1), F32)],
        compiler_params=_params("arbitrary"),
        name="router",
    )(x, w_router.T, b_router.reshape(E, 1))


ROW_DMA_UNROLL = 8


def _start_row_copies(n, copy_of):
    def body(g, carry):
        for u in range(ROW_DMA_UNROLL):
            copy_of(g * ROW_DMA_UNROLL + u).start(priority=u % 2)
        return carry

    lax.fori_loop(0, n // ROW_DMA_UNROLL, body, 0)


def _wait_row_copies(hbm_ref, vmem_ref, sem, to_hbm):
    rows = hbm_ref.at[pl.ds(0, vmem_ref.shape[0])]
    (pltpu.make_async_copy(vmem_ref, rows, sem) if to_hbm else pltpu.make_async_copy(rows, vmem_ref, sem)).wait()


def _expert_kernel(te_ref, nv_ref, fresh_ref, src_ref, dst_ref, x_hbm, wg_ref, wu_ref, wd_ref, y_hbm,
                   wg_bf, wu_bf, wd_bf, xbuf, xs_bf, acc_ref, ybuf, gsem, ssem, zsem, *, tr, n_pairs):
    i = pl.program_id(0)
    j = pl.program_id(1)
    nv = nv_ref[0]
    valid = i < nv
    last = j == pl.num_programs(1) - 1
    slot = i % 2

    def gather(tile, s):
        _start_row_copies(tr, lambda r: pltpu.make_async_copy(x_hbm.at[src_ref[tile * tr + r]], xbuf.at[s, r],
                                                              gsem.at[s]))

    def scatter(tile, s):
        _start_row_copies(tr, lambda r: pltpu.make_async_copy(ybuf.at[s, r], y_hbm.at[dst_ref[tile * tr + r]],
                                                              ssem.at[s]))

    @pl.when((i == 0) & (j == 0))
    def _():
        ybuf[1] = jnp.zeros(ybuf.shape[1:], F32)
        spare = [pltpu.make_async_copy(ybuf.at[1], y_hbm.at[pl.ds(n_pairs + c * tr, tr)], zsem)
                 for c in range((y_hbm.shape[0] - n_pairs) // tr)]
        for cp in spare:
            cp.start()
        for cp in spare:
            cp.wait()

    @pl.when(valid & (j == 0))
    def _():
        @pl.when(i == 0)
        def _():
            gather(0, 0)

        @pl.when(i + 1 < nv)
        def _():
            gather(i + 1, 1 - slot)

        _wait_row_copies(x_hbm, xbuf.at[slot], gsem.at[slot], to_hbm=False)
        xs_bf[...] = _from_slabs(lambda rows, cols: xbuf[slot, rows, cols, :], tr, xbuf.shape[2]).astype(BF16)

    @pl.when(valid & (fresh_ref[i] == 1))
    def _():
        wg_bf[j] = wg_ref[...].astype(BF16)
        wu_bf[j] = wu_ref[...].astype(BF16)
        wd_bf[j] = wd_ref[...].astype(BF16)

    @pl.when(valid)
    def _():
        x = xs_bf[...]
        hg = jnp.dot(x, wg_bf[j], preferred_element_type=F32)
        hu = jnp.dot(x, wu_bf[j], preferred_element_type=F32)
        h = (_silu(hg) * hu).astype(BF16)
        part = jnp.dot(h, wd_bf[j], preferred_element_type=F32)

        @pl.when(j == 0)
        def _():
            acc_ref[...] = part

        @pl.when(j > 0)
        def _():
            acc_ref[...] = acc_ref[...] + part

    @pl.when(valid & last)
    def _():
        _to_slabs(ybuf.at[slot], acc_ref[...])
        scatter(i, slot)

        @pl.when(i > 0)
        def _():
            _wait_row_copies(y_hbm, ybuf.at[1 - slot], ssem.at[1 - slot], to_hbm=True)

        @pl.when(i == nv - 1)
        def _():
            _wait_row_copies(y_hbm, ybuf.at[slot], ssem.at[slot], to_hbm=True)


def expert_ffn(x_slabs, src_rows, dst_rows, tile_expert, n_valid, fresh, w_gate, w_up, w_down, layer, tr, n_out):
    T, S, _ = x_slabs.shape
    n_pairs = 2 * T
    D = S * LANE
    R = src_rows.shape[0]
    De = w_gate.shape[-1]
    te = _tile(De, (512, 256, 128))
    nj = De // te
    assert tr % ROW_DMA_UNROLL == 0
    chunk = lambda i, j, fr: jnp.where(fr[i] == 1, j, nj - 1)
    grid_spec = pltpu.PrefetchScalarGridSpec(
        num_scalar_prefetch=5,
        grid=(R // tr, nj),
        in_specs=[pl.BlockSpec(memory_space=pl.ANY),
                  pl.BlockSpec((None, None, D, te), lambda i, j, e, nv, fr, s, d: (layer, e[i], 0, chunk(i, j, fr))),
                  pl.BlockSpec((None, None, D, te), lambda i, j, e, nv, fr, s, d: (layer, e[i], 0, chunk(i, j, fr))),
                  pl.BlockSpec((None, None, te, D), lambda i, j, e, nv, fr, s, d: (layer, e[i], chunk(i, j, fr), 0))],
        out_specs=pl.BlockSpec(memory_space=pl.ANY),
        scratch_shapes=[pltpu.VMEM((nj, D, te), BF16), pltpu.VMEM((nj, D, te), BF16), pltpu.VMEM((nj, te, D), BF16),
                        pltpu.VMEM((2, tr, S, LANE), F32), pltpu.VMEM((tr, D), BF16), pltpu.VMEM((tr, D), F32),
                        pltpu.VMEM((2, tr, S, LANE), F32),
                        pltpu.SemaphoreType.DMA((2,)), pltpu.SemaphoreType.DMA((2,)), pltpu.SemaphoreType.DMA(())],
    )
    assert (n_out - n_pairs) % tr == 0
    return pl.pallas_call(
        functools.partial(_expert_kernel, tr=tr, n_pairs=n_pairs),
        grid_spec=grid_spec,
        out_shape=jax.ShapeDtypeStruct((n_out, S, LANE), F32),
        compiler_params=_params("arbitrary", "arbitrary"),
        name="moe_experts",
    )(tile_expert, n_valid, fresh, src_rows, dst_rows, x_slabs, w_gate, w_up, w_down)


def _combine_ln_kernel(y_ref, gate_ref, res_ref, g_ref, b_ref, o1_ref, o2_ref, *obf_ref, alpha, n_first):
    i = pl.program_id(0)
    tm, S2, _ = y_ref.shape
    S = S2 // 2
    shift = lambda cols, by: cols + by if isinstance(cols, int) else slice(cols.start + by, cols.stop + by)
    y0 = _from_slabs(lambda rows, cols: y_ref[rows, cols, :], tm, S)
    y1 = _from_slabs(lambda rows, cols: y_ref[rows, shift(cols, S), :], tm, S)
    moe = gate_ref[:, 0:1] * y0 + gate_ref[:, 1:2] * y1
    z = _layer_norm(alpha * res_ref[...] + moe, g_ref[...], b_ref[...])

    @pl.when(i < n_first)
    def _():
        o1_ref[...] = z

    @pl.when(i >= n_first)
    def _():
        o2_ref[...] = z

    if obf_ref:
        obf_ref[0][...] = z.astype(BF16)


def combine_ln(y_pairs, gates_t, res, g, b, alpha, split_rows, with_bf16):
    T, D = res.shape
    S2 = y_pairs.shape[1]
    tm = _tile(math.gcd(split_rows, T - split_rows), (256, 128))
    row = lambda i: (i, 0)
    n1 = split_rows // tm
    assert 0 < n1 < T // tm
    out_specs = [pl.BlockSpec((tm, D), lambda i: (jnp.minimum(i, n1 - 1), 0)),
                 pl.BlockSpec((tm, D), lambda i: (jnp.maximum(i - n1, 0), 0))]
    out_shape = [jax.ShapeDtypeStruct((split_rows, D), F32), jax.ShapeDtypeStruct((T - split_rows, D), F32)]
    if with_bf16:
        out_specs.append(pl.BlockSpec((tm, D), row))
        out_shape.append(jax.ShapeDtypeStruct((T, D), BF16))
    return pl.pallas_call(
        functools.partial(_combine_ln_kernel, alpha=alpha, n_first=n1),
        grid=(T // tm,),
        in_specs=[pl.BlockSpec((tm, S2, LANE), lambda i: (i, 0, 0)),
                  pl.BlockSpec((tm, 8), row),
                  pl.BlockSpec((tm, D), row),
                  pl.BlockSpec((1, D), lambda i: (0, 0)),
                  pl.BlockSpec((1, D), lambda i: (0, 0))],
        out_specs=out_specs,
        out_shape=out_shape,
        compiler_params=_params("arbitrary"),
        name="moe_combine_ln",
    )(y_pairs, gates_t, res, g.reshape(1, D), b.reshape(1, D))


def _dispatch_plan(e_idx, counts, E, tr):
    T = e_idx.shape[1]
    pairs = 2 * T
    e_flat = e_idx[0:2, :].T.reshape(pairs)
    rank = e_idx[2:4, :].T.reshape(pairs)
    padded = ((counts + tr - 1) // tr) * tr
    pstart = jnp.cumsum(padded) - padded
    pos = (pstart[e_flat] + rank).astype(jnp.int32)
    n_tiles = pairs // tr + E
    R = n_tiles * tr
    pair_of_row = jnp.full((R,), -1, jnp.int32).at[pos].set(jnp.arange(pairs, dtype=jnp.int32), unique_indices=True)
    n_valid = (jnp.sum(padded) // tr).astype(jnp.int32).reshape(1)
    tile_end = (pstart + padded) // tr
    tiles = jnp.arange(n_tiles, dtype=jnp.int32)
    tile_e = jnp.minimum(jnp.sum((tiles[:, None] >= tile_end[None, :]).astype(jnp.int32), axis=1), E - 1)
    spare = pairs + tile_e * tr - pstart[tile_e] - counts[tile_e]
    spare_row = jnp.repeat(spare, tr) + jnp.arange(R, dtype=jnp.int32)
    is_pair = pair_of_row >= 0
    src_rows = jnp.where(is_pair, pair_of_row // 2, 0).astype(jnp.int32)
    dst_rows = jnp.where(is_pair, pair_of_row, spare_row).astype(jnp.int32)
    last_e = tile_e[jnp.maximum(n_valid[0] - 1, 0)]
    tile_e = jnp.where(tiles < n_valid[0], tile_e, last_e).astype(jnp.int32)
    fresh = jnp.concatenate([jnp.ones((1,), jnp.int32), (tile_e[1:] != tile_e[:-1]).astype(jnp.int32)])
    return src_rows, dst_rows, tile_e, n_valid, fresh, R


def moe_ln(x, x_slabs, x_res_scale, w_router, b_router, w_gate, w_up, w_down, layer, g, b, split_rows, with_bf16):
    T, D = x.shape
    S = D // LANE
    E = w_router.shape[1]
    tr = _tile(2 * T, (256, 128))
    e_idx, gates, counts = router(x, w_router, b_router)
    src_rows, dst_rows, tile_e, n_valid, fresh, R = _dispatch_plan(e_idx, counts[:, 0], E, tr)
    ys = expert_ffn(x_slabs, src_rows, dst_rows, tile_e, n_valid, fresh, w_gate, w_up, w_down, layer, tr, R)
    return combine_ln(ys.reshape(R // 2, 2 * S, LANE), gates.T, x, g, b, x_res_scale, split_rows, with_bf16)


def kernel(x_prompt, x_sample, cache_k, cache_v, cache_logf, state_conv, state_hgrn, page_table, w_in_even, b_fgate, w_dw, b_dw, g_cnorm, b_cnorm, w_out_even, w_in_odd, lb_logits, g_onorm, g_vnorm, b_vnorm, w_sgu, b_sgu, w_out_odd, ln1_g, ln1_b, ln2_g, ln2_b, w_router, b_router, w_gate, w_up, w_down):
    Bp, Lp, D = x_prompt.shape
    Bs, Ls, _ = x_sample.shape
    Dh = D // 2
    H = b_fgate.shape[1]
    HG = state_hgrn.shape[2]
    assert Dh == H * LANE and Dh == HG * LANE and Dh == w_sgu.shape[1] * LANE
    depth = ln1_g.shape[0]
    alpha = (2 * depth) ** 0.25
    Tp, Ts = Bp * Lp, Bs * Ls
    n_even = cache_k.shape[0]
    n_phys, page = cache_k.shape[1], cache_k.shape[2]

    lb_p = jax.nn.softmax(lb_logits.astype(F32), axis=0)
    lb_all = jnp.cumsum(lb_p, axis=0) - lb_p[0]

    res = (x_prompt.reshape(Tp, D), x_sample.reshape(Ts, D))
    x_bf = jnp.concatenate([r.astype(BF16) for r in res], axis=0)
    rin, tot = page_suffix(cache_logf.astype(F32).reshape(n_even * n_phys, page * H), H)
    rin = rin.reshape(n_even * n_phys, 1, page * H)
    tot = tot.reshape(n_even * n_phys, 1, page * H)

    out = {k: [] for k in ("kp", "vp", "lfp", "convp", "hgp", "ks", "vs", "lfs", "convs", "hgs", "mlpv")}
    for l in range(depth):
        j = l // 2
        if l % 2 == 0:
            w_in = w_in_even[j]
            ag = matmul_cols(x_bf, w_in_even, j, 0, 2 * Dh)
            q = matmul_cols(x_bf, w_in_even, j, 2 * Dh, Dh)
            k_p, k_s = matmul_cols(x_bf, w_in_even, j, 3 * Dh, Dh, split_rows=Tp)
            v_p, v_s = matmul_cols(x_bf, w_in_even, j, 4 * Dh, Dh, split_rows=Tp)
            lf, _, c, ct = logf_project(x_bf, w_in[:, 5 * Dh:], b_fgate[j], Lp)
            conv_args = (w_dw[j], b_dw[j], g_cnorm[j], b_cnorm[j])
            ca_p, cst_p = conv_prompt(ag, Bp, Lp, *conv_args)
            ca_s, cst_s = conv_sample(ag, Tp, Bs, Ls, state_conv[j], *conv_args)
            att_p = fox_prompt(q, k_p, v_p, c, ct, Bp, Lp, H)
            att_s = fox_sample(q, k_s, v_s, lf, Tp, Bs, Ls, cache_k, cache_v, rin, tot, j, page_table, H)
            h1 = (ca_p, ca_s)
            h2 = (att_p, att_s.astype(BF16))
            w_out = w_out_even[j]
            out["kp"].append(k_p.reshape(Bp, Lp, H, LANE))
            out["vp"].append(v_p.reshape(Bp, Lp, H, LANE))
            out["lfp"].append(lf[:Tp].reshape(Bp, Lp, H))
            out["convp"].append(cst_p)
            out["ks"].append(k_s.reshape(Bs, Ls, H, LANE))
            out["vs"].append(v_s.reshape(Bs, Ls, H, LANE))
            out["lfs"].append(lf[Tp:].reshape(Bs, Ls, H))
            out["convs"].append(cst_s)
        else:
            proj = matmul_cols(x_bf, w_in_odd, j, 0, 6 * Dh)
            o_p, s_p = hgrn(proj, 0, Bp, Lp, lb_all[l], g_onorm[j], None, HG)
            o_s, s_s = hgrn(proj, Tp, Bs, Ls, lb_all[l], g_onorm[j], state_hgrn[j].astype(F32), HG)
            mlp_args = (w_sgu[j], b_sgu[j], g_vnorm[j], b_vnorm[j])
            d_p, _ = gmlp(proj, 0, Tp, Lp, *mlp_args)
            d_s, v_s = gmlp(proj, Tp, Ts, Ls, *mlp_args)
            h1 = (o_p, o_s)
            h2 = (d_p, d_s)
            w_out = w_out_odd[j]
            out["hgp"].append(s_p)
            out["hgs"].append(s_s)
            out["mlpv"].append(v_s.reshape(Bs, Ls, Dh))
        x, x_slabs = outproj_ln(h1, h2, w_out.astype(BF16), res, ln1_g[l], ln1_b[l], alpha)
        more = l + 1 < depth
        res_p, res_s, *x_next = moe_ln(x, x_slabs, alpha, w_router, b_router, w_gate, w_up, w_down, l,
                                       ln2_g[l], ln2_b[l], Tp, more)
        res = (res_p, res_s)
        if more:
            x_bf = x_next[0]

    stack = lambda name: jnp.stack(out[name])
    return (res[0].reshape(Bp, Lp, D), res[1].reshape(Bs, Ls, D),
            stack("kp"), stack("vp"), stack("lfp"), stack("convp"), stack("hgp"),
            stack("ks"), stack("vs"), stack("lfs"), stack("convs"), stack("hgs"), stack("mlpv"))
```

```python
import functools
import math

import jax
import jax.numpy as jnp
from jax import lax
from jax.experimental import pallas as pl
from jax.experimental.pallas import tpu as pltpu

F32 = jnp.float32
BF16 = jnp.bfloat16
HIGHEST = lax.Precision.HIGHEST

LANE = 128
SUBLANES = 8
VMEM_LIMIT = 56 * 1024 * 1024
LN_EPS = 1e-5
HG_CHUNK = 64
HG_SUB = 16
CONV_HALO = 32
N_GROUPS = 4
NT = (((1,), (1,)), ((), ()))
TN = (((0,), (0,)), ((), ()))


def _params(*sem):
    return pltpu.CompilerParams(dimension_semantics=sem, vmem_limit_bytes=VMEM_LIMIT)


def _tile(n, prefs):
    for t in prefs:
        if n % t == 0:
            return t
    raise ValueError(f"no tile in {prefs} divides {n}")


def _sigmoid(x):
    return 1.0 / (1.0 + jnp.exp(-x))


def _silu(x):
    return x * _sigmoid(x)


def _gelu(x):
    return 0.5 * x * (1.0 + jnp.tanh(0.7978845608028654 * (x + 0.044715 * (x * x * x))))


def _log_sigmoid(x):
    return -(jnp.maximum(-x, 0.0) + jnp.log1p(jnp.exp(-jnp.abs(x))))


def _layer_norm(x, g, b):
    mu = jnp.mean(x, axis=-1, keepdims=True)
    xc = x - mu
    var = jnp.mean(xc * xc, axis=-1, keepdims=True)
    return xc * lax.rsqrt(var + LN_EPS) * g + b


def _iota(shape, dim):
    return lax.broadcasted_iota(jnp.int32, shape, dim)


def _matmul_kernel(x_ref, w_ref, o_ref, wbf_ref):
    @pl.when(pl.program_id(1) == 0)
    def _():
        wbf_ref[...] = w_ref[...].astype(BF16)

    o_ref[...] = jnp.dot(x_ref[...], wbf_ref[...], preferred_element_type=F32).astype(o_ref.dtype)


def _matmul_split_kernel(x_ref, w_ref, o1_ref, o2_ref, wbf_ref, *, n_first):
    i = pl.program_id(1)

    @pl.when(i == 0)
    def _():
        wbf_ref[...] = w_ref[...].astype(BF16)

    y = jnp.dot(x_ref[...], wbf_ref[...], preferred_element_type=F32)

    @pl.when(i < n_first)
    def _():
        o1_ref[...] = y

    @pl.when(i >= n_first)
    def _():
        o2_ref[...] = y


def matmul_cols(x_bf, w, layer, col_start, n_cols, split_rows=None):
    M, K = x_bf.shape
    tm = _tile(M if split_rows is None else math.gcd(split_rows, M - split_rows), (1024, 512, 256, 128))
    tn = _tile(n_cols, (1024, 512, 256, 128))
    assert col_start % tn == 0 and M % tm == 0
    off = col_start // tn
    common = dict(
        grid=(n_cols // tn, M // tm),
        in_specs=[pl.BlockSpec((tm, K), lambda j, i: (i, 0)),
                  pl.BlockSpec((None, K, tn), lambda j, i: (layer, 0, j + off))],
        scratch_shapes=[pltpu.VMEM((K, tn), BF16)],
        compiler_params=_params("arbitrary", "arbitrary"),
    )
    if split_rows is None:
        return pl.pallas_call(
            _matmul_kernel,
            out_specs=pl.BlockSpec((tm, tn), lambda j, i: (i, j)),
            out_shape=jax.ShapeDtypeStruct((M, n_cols), F32),
            name="matmul_cols", **common,
        )(x_bf, w)
    n1 = split_rows // tm
    assert (M - split_rows) % tm == 0 and 0 < n1 < M // tm
    return pl.pallas_call(
        functools.partial(_matmul_split_kernel, n_first=n1),
        out_specs=[pl.BlockSpec((tm, tn), lambda j, i: (jnp.minimum(i, n1 - 1), j)),
                   pl.BlockSpec((tm, tn), lambda j, i: (jnp.maximum(i - n1, 0), j))],
        out_shape=[jax.ShapeDtypeStruct((split_rows, n_cols), F32),
                   jax.ShapeDtypeStruct((M - split_rows, n_cols), F32)],
        name="matmul_cols_split", **common,
    )(x_bf, w)


def _logf_kernel(x_ref, w_ref, wt_ref, b_ref, bt_ref, lf_ref, lft_ref, c_ref, ct_ref,
                 carry_ref, carryt_ref, *, tiles_per_seq):
    i = pl.program_id(0)
    x = x_ref[...]
    tm = x.shape[0]
    fz = jnp.dot(x, w_ref[...].astype(BF16), preferred_element_type=F32) + b_ref[...]
    fzt = lax.dot_general(wt_ref[...].astype(BF16), x, NT, preferred_element_type=F32) + bt_ref[...]
    lf = _log_sigmoid(fz)
    lft = _log_sigmoid(fzt)
    lf_ref[...] = lf
    lft_ref[...] = lft

    @pl.when(i % tiles_per_seq == 0)
    def _():
        carry_ref[...] = jnp.zeros_like(carry_ref)
        carryt_ref[...] = jnp.zeros_like(carryt_ref)

    row = _iota((tm, tm), 0)
    col = _iota((tm, tm), 1)
    lower = (col <= row).astype(F32)
    upper = (row <= col).astype(F32)
    c = jnp.dot(lower, lf, precision=HIGHEST, preferred_element_type=F32) + carry_ref[...]
    ct = jnp.dot(lft, upper, precision=HIGHEST, preferred_element_type=F32) + carryt_ref[...]
    c_ref[...] = c
    ct_ref[...] = ct
    carry_ref[...] = c[tm - 1:tm, :]
    carryt_ref[...] = ct[:, tm - 1:tm]


def logf_project(x_bf, w_f, b_f, seq_len):
    T, D = x_bf.shape
    H = w_f.shape[1]
    tm = _tile(seq_len, (512, 256, 128))
    assert T % tm == 0
    outs = pl.pallas_call(
        functools.partial(_logf_kernel, tiles_per_seq=seq_len // tm),
        grid=(T // tm,),
        in_specs=[pl.BlockSpec((tm, D), lambda i: (i, 0)),
                  pl.BlockSpec((D, H), lambda i: (0, 0)),
                  pl.BlockSpec((H, D), lambda i: (0, 0)),
                  pl.BlockSpec((1, H), lambda i: (0, 0)),
                  pl.BlockSpec((H, 1), lambda i: (0, 0))],
        out_specs=[pl.BlockSpec((tm, H), lambda i: (i, 0)),
                   pl.BlockSpec((H, tm), lambda i: (0, i)),
                   pl.BlockSpec((tm, H), lambda i: (i, 0)),
                   pl.BlockSpec((H, tm), lambda i: (0, i))],
        out_shape=[jax.ShapeDtypeStruct((T, H), F32), jax.ShapeDtypeStruct((H, T), F32),
                   jax.ShapeDtypeStruct((T, H), F32), jax.ShapeDtypeStruct((H, T), F32)],
        scratch_shapes=[pltpu.VMEM((1, H), F32), pltpu.VMEM((H, 1), F32)],
        compiler_params=_params("arbitrary"),
        name="logf_project",
    )(x_bf, w_f, w_f.T, b_f.reshape(1, H), b_f.reshape(H, 1))
    return outs


def _conv_tail(acc, bdw, g, b):
    return _silu(_layer_norm(acc + bdw, g, b))


def _conv_prompt_kernel(ag_ref, wdw_ref, bdw_ref, g_ref, b_ref, y_ref, st_ref, ext_ref, sh_ref, *, W, tt, C):
    t = pl.program_id(1)

    @pl.when(t == 0)
    def _():
        ext_ref[0:CONV_HALO, :] = jnp.zeros((CONV_HALO, C), F32)
        ext_ref[CONV_HALO + tt:CONV_HALO + tt + SUBLANES, :] = jnp.zeros((SUBLANES, C), F32)

    @pl.when(t > 0)
    def _():
        ext_ref[0:CONV_HALO, :] = ext_ref[tt:tt + CONV_HALO, :]

    u = ag_ref[:, 0:C] * _sigmoid(ag_ref[:, C:2 * C])
    ext_ref[CONV_HALO:CONV_HALO + tt, :] = u
    base = CONV_HALO - (W - 1)
    acc = None
    for r in range(SUBLANES):
        taps = [w for w in range(W) if (base + w) % SUBLANES == r]
        if not taps:
            continue
        sh_ref[...] = ext_ref[r:r + CONV_HALO + tt, :]
        for w in taps:
            a = base + w - r
            term = sh_ref[a:a + tt, :] * wdw_ref[w:w + 1, :]
            acc = term if acc is None else acc + term
    y_ref[...] = _conv_tail(acc, bdw_ref[...], g_ref[...], b_ref[...]).astype(y_ref.dtype)

    @pl.when(t == pl.num_programs(1) - 1)
    def _():
        st_ref[...] = ext_ref[CONV_HALO + tt - (W - 1):CONV_HALO + tt, :]


def conv_prompt(ag, n_seq, seq_len, w_dw, b_dw, g_n, b_n):
    W, C = w_dw.shape
    assert W - 1 <= CONV_HALO
    tt = _tile(seq_len, (256, 128))
    nt = seq_len // tt
    vec = lambda a: a.reshape(1, C)
    return pl.pallas_call(
        functools.partial(_conv_prompt_kernel, W=W, tt=tt, C=C),
        grid=(n_seq, nt),
        in_specs=[pl.BlockSpec((tt, 2 * C), lambda b, t: (b * nt + t, 0)),
                  pl.BlockSpec((W, C), lambda b, t: (0, 0)),
                  pl.BlockSpec((1, C), lambda b, t: (0, 0)),
                  pl.BlockSpec((1, C), lambda b, t: (0, 0)),
                  pl.BlockSpec((1, C), lambda b, t: (0, 0))],
        out_specs=[pl.BlockSpec((tt, C), lambda b, t: (b * nt + t, 0)),
                   pl.BlockSpec((None, W - 1, C), lambda b, t: (b, 0, 0))],
        out_shape=[jax.ShapeDtypeStruct((n_seq * seq_len, C), BF16),
                   jax.ShapeDtypeStruct((n_seq, W - 1, C), F32)],
        scratch_shapes=[pltpu.VMEM((CONV_HALO + tt + SUBLANES, C), F32), pltpu.VMEM((CONV_HALO + tt, C), F32)],
        compiler_params=_params("arbitrary", "arbitrary"),
        name="conv_prompt",
    )(ag, w_dw, vec(b_dw), vec(g_n), vec(b_n))


def _conv_sample_kernel(ag_ref, st_ref, wdw_ref, bdw_ref, g_ref, b_ref, y_ref, nst_ref, ext_ref, *, W, Ls, C, bb):
    u = ag_ref[:, 0:C] * _sigmoid(ag_ref[:, C:2 * C])
    ext_ref[:, 0:W - 1, :] = st_ref[...]
    ext_ref[:, W - 1:W - 1 + Ls, :] = u.reshape(bb, Ls, C)
    acc = ext_ref[:, 0:Ls, :] * wdw_ref[0:1, :]
    for w in range(1, W):
        acc = acc + ext_ref[:, w:w + Ls, :] * wdw_ref[w:w + 1, :]
    y = _conv_tail(acc.reshape(bb * Ls, C), bdw_ref[...], g_ref[...], b_ref[...])
    y_ref[...] = y.astype(y_ref.dtype)
    nst_ref[...] = ext_ref[:, Ls:Ls + W - 1, :]


def conv_sample(ag, row_start, n_seq, Ls, state, w_dw, b_dw, g_n, b_n):
    W, C = w_dw.shape
    bb = _tile(n_seq, (8, 4, 2, 1))
    rows = bb * Ls
    assert Ls % 8 == 0 and row_start % rows == 0
    r0 = row_start // rows
    vec = lambda a: a.reshape(1, C)
    return pl.pallas_call(
        functools.partial(_conv_sample_kernel, W=W, Ls=Ls, C=C, bb=bb),
        grid=(n_seq // bb,),
        in_specs=[pl.BlockSpec((rows, 2 * C), lambda i: (r0 + i, 0)),
                  pl.BlockSpec((bb, W - 1, C), lambda i: (i, 0, 0)),
                  pl.BlockSpec((W, C), lambda i: (0, 0)),
                  pl.BlockSpec((1, C), lambda i: (0, 0)),
                  pl.BlockSpec((1, C), lambda i: (0, 0)),
                  pl.BlockSpec((1, C), lambda i: (0, 0))],
        out_specs=[pl.BlockSpec((rows, C), lambda i: (i, 0)),
                   pl.BlockSpec((bb, W - 1, C), lambda i: (i, 0, 0))],
        out_shape=[jax.ShapeDtypeStruct((n_seq * Ls, C), BF16),
                   jax.ShapeDtypeStruct((n_seq, W - 1, C), F32)],
        scratch_shapes=[pltpu.VMEM((bb, W - 1 + Ls, C), F32)],
        compiler_params=_params("arbitrary"),
        name="conv_sample",
    )(ag, state, w_dw, vec(b_dw), vec(g_n), vec(b_n))


def _fox_prompt_kernel(q_ref, k_ref, v_ref, c_ref, ct_ref, o_ref, *, tq, scale, H):
    h = pl.program_id(1)
    qi = pl.program_id(2)
    q = (q_ref[...] * scale).astype(BF16)
    cq = jnp.sum(jnp.where(_iota((tq, H), 1) == h, c_ref[...], 0.0), axis=1, keepdims=True)

    def step(ki, carry, on_diagonal):
        m, l, acc = carry
        start = pl.multiple_of(ki * tq, tq)
        k = k_ref[pl.ds(start, tq), :].astype(BF16)
        v = v_ref[pl.ds(start, tq), :].astype(BF16)
        ck = ct_ref[pl.ds(h, 1), pl.ds(start, tq)]
        s = lax.dot_general(q, k, NT, preferred_element_type=F32) + cq - ck
        if on_diagonal:
            s = jnp.where(_iota((tq, tq), 0) >= _iota((tq, tq), 1), s, -jnp.inf)
        m_new = jnp.maximum(m, jnp.max(s, axis=1, keepdims=True))
        alpha = jnp.exp(m - m_new)
        p = jnp.exp(s - m_new)
        l = alpha * l + jnp.sum(p, axis=1, keepdims=True)
        acc = alpha * acc + jnp.dot(p.astype(BF16), v, preferred_element_type=F32)
        return m_new, l, acc

    init = (jnp.full((tq, 1), -jnp.inf, F32), jnp.zeros((tq, 1), F32), jnp.zeros((tq, LANE), F32))
    carry = lax.fori_loop(0, qi, lambda ki, c: step(ki, c, False), init)
    _, l, acc = step(qi, carry, True)
    o_ref[...] = (acc / l).astype(o_ref.dtype)


def fox_prompt(q, k, v, c, ct, n_seq, seq_len, H):
    tq = _tile(seq_len, (512, 256, 128))
    nq = seq_len // tq
    return pl.pallas_call(
        functools.partial(_fox_prompt_kernel, tq=tq, scale=LANE ** -0.5, H=H),
        grid=(n_seq, H, nq),
        in_specs=[pl.BlockSpec((tq, LANE), lambda b, h, i: (b * nq + i, h)),
                  pl.BlockSpec((seq_len, LANE), lambda b, h, i: (b, h)),
                  pl.BlockSpec((seq_len, LANE), lambda b, h, i: (b, h)),
                  pl.BlockSpec((tq, H), lambda b, h, i: (b * nq + i, 0)),
                  pl.BlockSpec((H, seq_len), lambda b, h, i: (0, b))],
        out_specs=pl.BlockSpec((tq, LANE), lambda b, h, i: (b * nq + i, h)),
        out_shape=jax.ShapeDtypeStruct((n_seq * seq_len, H * LANE), BF16),
        compiler_params=_params("arbitrary", "arbitrary", "arbitrary"),
        name="fox_prompt",
    )(q, k, v, c, ct)


def _split3_dot(x, w_bf, w_left=False):
    x1 = x.astype(BF16)
    r1 = x - x1.astype(F32)
    x2 = r1.astype(BF16)
    x3 = (r1 - x2.astype(F32)).astype(BF16)
    if w_left:
        dot = lambda a: jnp.dot(w_bf, a, preferred_element_type=F32)
    else:
        dot = lambda a: jnp.dot(a, w_bf, preferred_element_type=F32)
    return dot(x1) + dot(x2) + dot(x3)


def _page_suffix_kernel(lf_ref, rin_ref, tot_ref, later_ref, same_ref, *, H):
    n = lf_ref.shape[1]

    @pl.when(pl.program_id(0) == 0)
    def _():
        r = _iota((n, n), 0)
        c = _iota((n, n), 1)
        same = (r % H) == (c % H)
        later_ref[...] = jnp.where(same & (r > c), 1.0, 0.0).astype(BF16)
        same_ref[...] = jnp.where(same, 1.0, 0.0).astype(BF16)

    x = lf_ref[...]
    rin_ref[...] = _split3_dot(x, later_ref[...])
    tot_ref[...] = _split3_dot(x, same_ref[...])


def page_suffix(lf_pages, H):
    P, n = lf_pages.shape
    tp = _tile(P, (256, 128, 64, 32, 16, 8))
    return pl.pallas_call(
        functools.partial(_page_suffix_kernel, H=H),
        grid=(P // tp,),
        in_specs=[pl.BlockSpec((tp, n), lambda i: (i, 0))],
        out_specs=[pl.BlockSpec((tp, n), lambda i: (i, 0)), pl.BlockSpec((tp, n), lambda i: (i, 0))],
        out_shape=[jax.ShapeDtypeStruct((P, n), F32), jax.ShapeDtypeStruct((P, n), F32)],
        scratch_shapes=[pltpu.VMEM((n, n), BF16), pltpu.VMEM((n, n), BF16)],
        compiler_params=_params("arbitrary"),
        name="page_suffix",
    )(lf_pages)


def _fox_sample_kernel(*refs, H, Ls, scale, pps):
    pt_ref, q_ref, kn_ref, vn_ref, lfn_ref = refs[0:5]
    ck_refs = refs[5:5 + pps]
    cv_refs = refs[5 + pps:5 + 2 * pps]
    rin_refs = refs[5 + 2 * pps:5 + 3 * pps]
    tot_refs = refs[5 + 3 * pps:5 + 4 * pps]
    o_ref, qh_ref, a_ref, m_ref, l_ref, acc_ref, rc_ref = refs[5 + 4 * pps:]
    p = pl.program_id(1)
    HQ = H * Ls
    cols = ck_refs[0].shape[0] * H
    sel = (_iota((HQ, H), 0) // Ls == _iota((HQ, H), 1)).astype(F32)
    causal = _iota((HQ, Ls), 1) <= _iota((HQ, Ls), 0) % Ls

    def new_logf():
        return lax.dot_general(sel, lfn_ref[...], NT, precision=HIGHEST, preferred_element_type=F32)

    @pl.when(p == 0)
    def _():
        q = q_ref[...]
        qh_ref[...] = jnp.concatenate([q[:, h * LANE:(h + 1) * LANE] for h in range(H)], axis=0).astype(BF16)
        a_ref[...] = jnp.sum(jnp.where(causal, new_logf(), 0.0), axis=1, keepdims=True)
        m_ref[...] = jnp.full((HQ, 1), -jnp.inf, F32)
        l_ref[...] = jnp.zeros((HQ, 1), F32)
        acc_ref[...] = jnp.zeros((HQ, LANE), F32)
        rc_ref[...] = jnp.zeros((1, cols), F32)

    def update(scores, pvs):
        m_old = m_ref[...]
        m_new = m_old
        for s in scores:
            m_new = jnp.maximum(m_new, jnp.max(s, axis=1, keepdims=True))
        alpha = jnp.exp(m_old - m_new)
        l_new = alpha * l_ref[...]
        acc = alpha * acc_ref[...]
        for s, pv in zip(scores, pvs):
            pr = jnp.exp(s - m_new)
            l_new = l_new + jnp.sum(pr, axis=1, keepdims=True)
            acc = acc + pv(pr.astype(BF16))
        l_ref[...] = l_new
        acc_ref[...] = acc
        m_ref[...] = m_new

    qh = qh_ref[...]
    own_head = (_iota((HQ, cols), 0) // Ls) == (_iota((HQ, cols), 1) % H)
    rc = rc_ref[...]
    scores, pvs = [], []
    for s_ in range(pps):
        kx = ck_refs[s_][...].reshape(cols, LANE).astype(BF16)
        vx = cv_refs[s_][...].reshape(cols, LANE).astype(BF16)
        bias = a_ref[...] + (rin_refs[s_][...] + rc)
        rc = rc + tot_refs[s_][...]
        sc = lax.dot_general(qh, kx, NT, preferred_element_type=F32) * scale + bias
        scores.append(jnp.where(own_head, sc, -jnp.inf))
        pvs.append(lambda pr, vx=vx: jnp.dot(pr, vx, preferred_element_type=F32))
    rc_ref[...] = rc
    update(scores, pvs)

    @pl.when(p == pl.num_programs(1) - 1)
    def _():
        head = lambda a, h: a[:, h * LANE:(h + 1) * LANE].astype(BF16)
        rows = lambda a, h: a[h * Ls:(h + 1) * Ls, :]
        kn = kn_ref[...]
        vn = vn_ref[...]
        s2 = jnp.concatenate([lax.dot_general(rows(qh, h), head(kn, h), NT, preferred_element_type=F32)
                              for h in range(H)], axis=0)
        upto = (_iota((Ls, Ls), 0) <= _iota((Ls, Ls), 1)).astype(F32)
        cum = jnp.dot(new_logf(), upto, precision=HIGHEST, preferred_element_type=F32)
        s2 = jnp.where(causal, s2 * scale + (a_ref[...] - cum), -jnp.inf)
        update([s2], [lambda pr: jnp.concatenate(
            [jnp.dot(rows(pr, h), head(vn, h), preferred_element_type=F32) for h in range(H)], axis=0)])
        out = acc_ref[...] / l_ref[...]
        for h in range(H):
            o_ref[:, h * LANE:(h + 1) * LANE] = rows(out, h)


def fox_sample(q, k, v, lf, row_start, n_seq, Ls, cache_k, cache_v, rin, tot, layer, page_table, H):
    D = H * LANE
    n_pages = page_table.shape[1]
    n_phys, page = cache_k.shape[1], cache_k.shape[2]
    pps = _tile(n_pages, (16, 8, 4, 2, 1))
    assert Ls % 8 == 0 and row_start % Ls == 0
    r0 = row_start // Ls
    HQ = H * Ls
    rows = lambda b, p, pt: (r0 + b, 0)

    def phys(s_):
        return lambda b, p, pt: pt[b * n_pages + (n_pages - 1 - (p * pps + s_))]

    kv_spec = lambda s_: pl.BlockSpec((None, None, page, H, LANE),
                                      lambda b, p, pt: (layer, phys(s_)(b, p, pt), 0, 0, 0))
    row_spec = lambda s_: pl.BlockSpec((None, 1, page * H),
                                       lambda b, p, pt: (layer * n_phys + phys(s_)(b, p, pt), 0, 0))
    grid_spec = pltpu.PrefetchScalarGridSpec(
        num_scalar_prefetch=1,
        grid=(n_seq, n_pages // pps),
        in_specs=([pl.BlockSpec((Ls, D), rows), pl.BlockSpec((Ls, D), lambda b, p, pt: (b, 0)),
                   pl.BlockSpec((Ls, D), lambda b, p, pt: (b, 0)), pl.BlockSpec((Ls, H), rows)]
                  + [kv_spec(s_) for s_ in range(pps)] + [kv_spec(s_) for s_ in range(pps)]
                  + [row_spec(s_) for s_ in range(pps)] + [row_spec(s_) for s_ in range(pps)]),
        out_specs=pl.BlockSpec((Ls, D), lambda b, p, pt: (b, 0)),
        scratch_shapes=[pltpu.VMEM((HQ, LANE), BF16), pltpu.VMEM((HQ, 1), F32), pltpu.VMEM((HQ, 1), F32),
                        pltpu.VMEM((HQ, 1), F32), pltpu.VMEM((HQ, LANE), F32), pltpu.VMEM((1, page * H), F32)],
    )
    return pl.pallas_call(
        functools.partial(_fox_sample_kernel, H=H, Ls=Ls, scale=LANE ** -0.5, pps=pps),
        grid_spec=grid_spec,
        out_shape=jax.ShapeDtypeStruct((n_seq * Ls, D), F32),
        compiler_params=_params("arbitrary", "arbitrary"),
        name="fox_sample",
    )(page_table.reshape(-1), q, k, v, lf, *([cache_k] * pps), *([cache_v] * pps),
      *([rin] * pps), *([tot] * pps))


def _transpose8(v):
    v = list(v)
    sub = _iota(v[0].shape, 1)
    for s in (4, 2, 1):
        keep = (sub & s) == 0
        for k in range(SUBLANES):
            if k & s:
                continue
            lo, hi = v[k], v[k + s]
            v[k] = jnp.where(keep, lo, pltpu.roll(hi, s, axis=1))
            v[k + s] = jnp.where(keep, pltpu.roll(lo, SUBLANES - s, axis=1), hi)
    return v


def _every8(t, n):
    return pl.ds(t, n // SUBLANES, stride=SUBLANES)


def _to_slabs(ref, x):
    n, S, _ = ref.shape
    if S % SUBLANES or n % SUBLANES:
        for k in range(S):
            ref[:, k, :] = x[:, k * LANE:(k + 1) * LANE]
        return
    for h in range(S // SUBLANES):
        v = [x[:, (SUBLANES * h + k) * LANE:(SUBLANES * h + k + 1) * LANE].reshape(n // SUBLANES, SUBLANES, LANE)
             for k in range(SUBLANES)]
        for t, w in enumerate(_transpose8(v)):
            ref[_every8(t, n), SUBLANES * h:SUBLANES * (h + 1), :] = w


def _from_slabs(load, n, S):
    if S % SUBLANES or n % SUBLANES:
        return jnp.concatenate([load(slice(None), k) for k in range(S)], axis=1)
    tiles = []
    for h in range(S // SUBLANES):
        w = [load(_every8(t, n), slice(SUBLANES * h, SUBLANES * (h + 1))) for t in range(SUBLANES)]
        tiles += [a.reshape(n, LANE) for a in _transpose8(w)]
    return jnp.concatenate(tiles, axis=1)


def _outproj_ln_kernel(h1p_ref, h1s_ref, h2p_ref, h2s_ref, w_ref, resp_ref, ress_ref, g_ref, b_ref, o_ref,
                       oslab_ref, *, alpha, half, n_first):
    first = pl.program_id(0) < n_first
    h1 = jnp.where(first, h1p_ref[...], h1s_ref[...])
    h2 = jnp.where(first, h2p_ref[...], h2s_ref[...])
    res = jnp.where(first, resp_ref[...], ress_ref[...])
    y = jnp.dot(h1, w_ref[0:half, :], preferred_element_type=F32)
    y = y + jnp.dot(h2, w_ref[half:2 * half, :], preferred_element_type=F32)
    z = _layer_norm(alpha * res + y, g_ref[...], b_ref[...])
    o_ref[...] = z
    _to_slabs(oslab_ref, z)


def outproj_ln(h1, h2, w_bf, res, g, b, alpha):
    (h1p, h1s), (h2p, h2s), (resp, ress) = h1, h2, res
    half = h1p.shape[1]
    D = resp.shape[1]
    T = resp.shape[0] + ress.shape[0]
    S = D // LANE
    tm = _tile(math.gcd(resp.shape[0], ress.shape[0]), (256, 128))
    n1 = resp.shape[0] // tm
    assert h1p.shape[0] == resp.shape[0] and h1s.shape[0] == ress.shape[0]
    first = lambda i: (jnp.minimum(i, n1 - 1), 0)
    second = lambda i: (jnp.maximum(i - n1, 0), 0)
    return pl.pallas_call(
        functools.partial(_outproj_ln_kernel, alpha=alpha, half=half, n_first=n1),
        grid=(T // tm,),
        in_specs=[pl.BlockSpec((tm, half), first), pl.BlockSpec((tm, half), second),
                  pl.BlockSpec((tm, half), first), pl.BlockSpec((tm, half), second),
                  pl.BlockSpec((2 * half, D), lambda i: (0, 0)),
                  pl.BlockSpec((tm, D), first), pl.BlockSpec((tm, D), second),
                  pl.BlockSpec((1, D), lambda i: (0, 0)),
                  pl.BlockSpec((1, D), lambda i: (0, 0))],
        out_specs=[pl.BlockSpec((tm, D), lambda i: (i, 0)), pl.BlockSpec((tm, S, LANE), lambda i: (i, 0, 0))],
        out_shape=[jax.ShapeDtypeStruct((T, D), F32), jax.ShapeDtypeStruct((T, S, LANE), F32)],
        compiler_params=_params("arbitrary"),
        name="outproj_ln",
    )(h1p, h1s, h2p, h2s, w_bf, resp, ress, g.reshape(1, D), b.reshape(1, D))


def _hgrn_kernel(*refs, C, sb, tl, bb, hh, has_s0):
    if has_s0:
        qz_ref, fz_ref, iz_ref, gz_ref, lb_ref, go_ref, s0_ref, o_ref, sn_ref, st_ref = refs
    else:
        qz_ref, fz_ref, iz_ref, gz_ref, lb_ref, go_ref, o_ref, sn_ref, st_ref = refs
    t = pl.program_id(2)
    nsb = C // sb

    @pl.when(t == 0)
    def _():
        for s in range(bb):
            for h in range(hh):
                st_ref[s, h] = s0_ref[s, h].T if has_s0 else jnp.zeros((LANE, LANE), F32)

    lower = (_iota((C, C), 1) <= _iota((C, C), 0))
    lower_bf = jnp.where(lower, 1.0, 0.0).astype(BF16)

    def head_chunk(qz, fz, iz, gz, lb, go, st):
        q = _silu(qz)
        f = lb + (1.0 - lb) * _sigmoid(fz)
        kk = 1.0 - f
        i_bf = iz.astype(BF16)
        b = _split3_dot(jnp.log(f), lower_bf, w_left=True)
        b_last = b[C - 1:C, :]
        starts = [jnp.zeros((1, LANE), F32)] + [b[I * sb - 1:I * sb, :] for I in range(1, nsb)]
        lasts = [b[(I + 1) * sb - 1:(I + 1) * sb, :] for I in range(nsb)]
        blk = lambda a, I: a[I * sb:(I + 1) * sb, :]
        kd = [blk(kk, J) * jnp.exp(lasts[J] - blk(b, J)) for J in range(nsb)]
        att_rows = []
        for I in range(nsb):
            qd = blk(q, I) * jnp.exp(blk(b, I) - starts[I])
            parts = [kd[J] * jnp.exp(starts[I] - lasts[J]) for J in range(I)]
            parts.append(blk(kk, I) * jnp.exp(starts[I] - blk(b, I)))
            if I + 1 < nsb:
                parts.append(jnp.zeros(((nsb - I - 1) * sb, LANE), F32))
            kmat = jnp.concatenate(parts, axis=0) if len(parts) > 1 else parts[0]
            att_rows.append(lax.dot_general(qd.astype(BF16), kmat.astype(BF16), NT, preferred_element_type=F32))
        att = jnp.concatenate(att_rows, axis=0) if nsb > 1 else att_rows[0]
        att = jnp.where(lower, att, 0.0)
        o = jnp.dot(att.astype(BF16), i_bf, preferred_element_type=F32)
        o = o + lax.dot_general((q * jnp.exp(b)).astype(BF16), st.astype(BF16), NT, preferred_element_type=F32)
        kst = (kk * jnp.exp(b_last - b)).astype(BF16)
        st_new = st * jnp.exp(b_last) + lax.dot_general(i_bf, kst, TN, preferred_element_type=F32)
        ms = jnp.mean(o * o, axis=-1, keepdims=True)
        return o * lax.rsqrt(ms + LN_EPS) * go * _silu(gz), st_new

    def chunk(s, r0):
        qz, fz, iz, gz = (ref[pl.ds(r0, C), :] for ref in (qz_ref, fz_ref, iz_ref, gz_ref))
        lb, go = lb_ref[...], go_ref[...]
        head = lambda a, h: a[:, h * LANE:(h + 1) * LANE]
        res = [head_chunk(*(head(a, h) for a in (qz, fz, iz, gz, lb, go)), st_ref[s, h]) for h in range(hh)]
        o = jnp.concatenate([r[0] for r in res], axis=1) if hh > 1 else res[0][0]
        return o.astype(o_ref.dtype), jnp.stack([r[1] for r in res])

    if tl == C:
        res = [chunk(s, s * tl) for s in range(bb)]
        o_ref[...] = jnp.concatenate([r[0] for r in res], axis=0) if bb > 1 else res[0][0]
        st_ref[...] = jnp.stack([r[1] for r in res])
    else:
        assert bb == 1

        def body(c, carry):
            r0 = pl.multiple_of(c * C, C)
            o, st_new = chunk(0, r0)
            o_ref[pl.ds(r0, C), :] = o
            st_ref[0] = st_new
            return carry

        lax.fori_loop(0, tl // C, body, 0)

    @pl.when(t == pl.num_programs(2) - 1)
    def _():
        for s in range(bb):
            for h in range(hh):
                sn_ref[s, h] = st_ref[s, h].T


def hgrn(proj, row_start, n_seq, seq_len, lb, g_o, s0, H):
    C = min(HG_CHUNK, seq_len)
    sb = min(HG_SUB, C)
    assert seq_len % C == 0 and C % sb == 0
    if seq_len >= 512:
        tl, bb, hh = _tile(seq_len, (512,)), 1, _tile(H, (8, 4, 2, 1))
    else:
        assert seq_len == C
        tl, bb, hh = seq_len, _tile(n_seq, (16, 8, 4, 2, 1)), 1
    nt = seq_len // tl
    rows = bb * tl
    W = hh * LANE
    assert row_start % rows == 0
    r0 = row_start // rows
    col = lambda k: (lambda b, h, t: (r0 + b * nt + t, k * (H // hh) + h))
    in_specs = [pl.BlockSpec((rows, W), col(0)), pl.BlockSpec((rows, W), col(1)),
                pl.BlockSpec((rows, W), col(2)), pl.BlockSpec((rows, W), col(3)),
                pl.BlockSpec((1, W), lambda b, h, t: (0, h)),
                pl.BlockSpec((1, W), lambda b, h, t: (0, h))]
    args = [proj, proj, proj, proj, lb.reshape(1, H * LANE), g_o.reshape(1, H * LANE)]
    if s0 is not None:
        in_specs.append(pl.BlockSpec((bb, hh, LANE, LANE), lambda b, h, t: (b, h, 0, 0)))
        args.append(s0)
    return pl.pallas_call(
        functools.partial(_hgrn_kernel, C=C, sb=sb, tl=tl, bb=bb, hh=hh, has_s0=s0 is not None),
        grid=(n_seq // bb, H // hh, nt),
        in_specs=in_specs,
        out_specs=[pl.BlockSpec((rows, W), lambda b, h, t: (b * nt + t, h)),
                   pl.BlockSpec((bb, hh, LANE, LANE), lambda b, h, t: (b, h, 0, 0))],
        out_shape=[jax.ShapeDtypeStruct((n_seq * seq_len, H * LANE), BF16),
                   jax.ShapeDtypeStruct((n_seq, H, LANE, LANE), F32)],
        scratch_shapes=[pltpu.VMEM((bb, hh, LANE, LANE), F32)],
        compiler_params=_params("arbitrary", "arbitrary", "arbitrary"),
        name="hgrn",
    )(*args)


def _gmlp_kernel(uz_ref, vz_ref, w_ref, bt_ref, g_ref, b_ref, d_ref, v_ref, *, G, cs):
    n = uz_ref.shape[0]
    v = _layer_norm(_gelu(vz_ref[...]), g_ref[...], b_ref[...])
    v_ref[...] = v
    u = _gelu(uz_ref[...])
    t = _iota((n, n), 0)
    s = _iota((n, n), 1)
    keep = (s <= t) & (t // cs == s // cs)
    for g in range(G):
        w = jnp.where(keep, w_ref[g], 0.0).astype(BF16)
        vg = v[:, g * LANE:(g + 1) * LANE].astype(BF16)
        mixed = jnp.dot(w, vg, preferred_element_type=F32) + bt_ref[:, g:g + 1]
        d_ref[:, g * LANE:(g + 1) * LANE] = (u[:, g * LANE:(g + 1) * LANE] * mixed).astype(d_ref.dtype)


def gmlp(proj, row_start, n_rows, seq_len, w_s, b_s, g_v, b_v):
    G, n, _ = w_s.shape
    Dh = G * LANE
    cs = min(n, seq_len)
    assert n % cs == 0 and seq_len % cs == 0 and n_rows % n == 0 and row_start % n == 0
    if cs < n:
        w_s = jnp.tile(w_s[:, :cs, :cs], (1, n // cs, n // cs))
        b_s = jnp.tile(b_s[:, :cs], (1, n // cs))
    r0 = row_start // n
    return pl.pallas_call(
        functools.partial(_gmlp_kernel, G=G, cs=cs),
        grid=(n_rows // n,),
        in_specs=[pl.BlockSpec((n, Dh), lambda i: (r0 + i, 4)),
                  pl.BlockSpec((n, Dh), lambda i: (r0 + i, 5)),
                  pl.BlockSpec((G, n, n), lambda i: (0, 0, 0)),
                  pl.BlockSpec((n, G), lambda i: (0, 0)),
                  pl.BlockSpec((1, Dh), lambda i: (0, 0)),
                  pl.BlockSpec((1, Dh), lambda i: (0, 0))],
        out_specs=[pl.BlockSpec((n, Dh), lambda i: (i, 0)), pl.BlockSpec((n, Dh), lambda i: (i, 0))],
        out_shape=[jax.ShapeDtypeStruct((n_rows, Dh), BF16), jax.ShapeDtypeStruct((n_rows, Dh), F32)],
        compiler_params=_params("arbitrary"),
        name="gmlp",
    )(proj, proj, w_s, b_s.T, g_v.reshape(1, Dh), b_v.reshape(1, Dh))


def _router_kernel(x_ref, wt_ref, bt_ref, idx_ref, gate_ref, cnt_ref, run_ref, *, E):
    tm = x_ref.shape[0]
    per = E // N_GROUPS
    logits = lax.dot_general(wt_ref[...], x_ref[...], NT, precision=HIGHEST, preferred_element_type=F32)
    z = jnp.exp(logits - jnp.max(logits, axis=0, keepdims=True))
    probs = z / jnp.sum(z, axis=0, keepdims=True)
    sel = probs + bt_ref[...]
    io = _iota((per, tm), 0)

    def top2(sg):
        m1 = jnp.max(sg, axis=0, keepdims=True)
        i1 = jnp.min(jnp.where(sg == m1, io, per), axis=0, keepdims=True)
        rest = jnp.where(io == i1, -jnp.inf, sg)
        m2 = jnp.max(rest, axis=0, keepdims=True)
        i2 = jnp.min(jnp.where(rest == m2, io, per), axis=0, keepdims=True)
        return m1 + m2, i1, i2

    best, e1, e2 = top2(sel[0:per, :])
    for g in range(1, N_GROUPS):
        score, i1, i2 = top2(sel[g * per:(g + 1) * per, :])
        better = score > best
        best = jnp.where(better, score, best)
        e1 = jnp.where(better, i1 + g * per, e1)
        e2 = jnp.where(better, i2 + g * per, e2)
    eo = _iota((E, tm), 0)
    p1 = jnp.sum(jnp.where(eo == e1, probs, 0.0), axis=0, keepdims=True)
    p2 = jnp.sum(jnp.where(eo == e2, probs, 0.0), axis=0, keepdims=True)
    tot = p1 + p2
    gate_ref[...] = jnp.concatenate([p1 / tot, p2 / tot, jnp.zeros((6, tm), F32)], axis=0)

    @pl.when(pl.program_id(0) == 0)
    def _():
        run_ref[...] = jnp.zeros_like(run_ref)

    chosen = jnp.where(eo == e1, 1.0, jnp.where(eo == e2, 1.0, 0.0))
    earlier = (_iota((tm, tm), 0) < _iota((tm, tm), 1)).astype(BF16)
    before = jnp.dot(chosen.astype(BF16), earlier, preferred_element_type=F32) + run_ref[...]
    r1 = jnp.sum(jnp.where(eo == e1, before, 0.0), axis=0, keepdims=True).astype(jnp.int32)
    r2 = jnp.sum(jnp.where(eo == e2, before, 0.0), axis=0, keepdims=True).astype(jnp.int32)
    run_ref[...] = run_ref[...] + jnp.sum(chosen, axis=1, keepdims=True)
    idx_ref[...] = jnp.concatenate([e1, e2, r1, r2, jnp.zeros((4, tm), jnp.int32)], axis=0)
    cnt_ref[...] = jnp.broadcast_to(run_ref[...], cnt_ref.shape).astype(jnp.int32)


def router(x, w_router, b_router):
    T, D = x.shape
    E = w_router.shape[1]
    tm = _tile(T, (512, 256, 128))
    return pl.pallas_call(
        functools.partial(_router_kernel, E=E),
        grid=(T // tm,),
        in_specs=[pl.BlockSpec((tm, D), lambda i: (i, 0)),
                  pl.BlockSpec((E, D), lambda i: (0, 0)),
                  pl.BlockSpec((E, 1), lambda i: (0, 0))],
        out_specs=[pl.BlockSpec((8, tm), lambda i: (0, i)), pl.BlockSpec((8, tm), lambda i: (0, i)),
                   pl.BlockSpec((E, LANE), lambda i: (0, 0))],
        out_shape=[jax.ShapeDtypeStruct((8, T), jnp.int32), jax.ShapeDtypeStruct((8, T), F32),
                   jax.ShapeDtypeStruct((E, LANE), jnp.int32)],
        scratch_shapes=[pltpu.VMEM((E, 1), F32)],
        compiler_params=_params("arbitrary"),
        name="router",
    )(x, w_router.T, b_router.reshape(E, 1))


ROW_DMA_UNROLL = 8


def _start_row_copies(n, copy_of):
    def body(g, carry):
        for u in range(ROW_DMA_UNROLL):
            copy_of(g * ROW_DMA_UNROLL + u).start(priority=u % 2)
        return carry

    lax.fori_loop(0, n // ROW_DMA_UNROLL, body, 0)


def _wait_row_copies(hbm_ref, vmem_ref, sem, to_hbm):
    rows = hbm_ref.at[pl.ds(0, vmem_ref.shape[0])]
    (pltpu.make_async_copy(vmem_ref, rows, sem) if to_hbm else pltpu.make_async_copy(rows, vmem_ref, sem)).wait()


def _expert_kernel(te_ref, nv_ref, fresh_ref, src_ref, dst_ref, x_hbm, wg_ref, wu_ref, wd_ref, y_hbm,
                   wg_bf, wu_bf, wd_bf, xbuf, xs_bf, acc_ref, ybuf, gsem, ssem, zsem, *, tr, nj, n_pairs):
    i = pl.program_id(0)
    j = pl.program_id(1)
    nv = nv_ref[0]
    valid = i < nv
    last = j == pl.num_programs(1) - 1
    slot = i % 2

    def gather(tile, s):
        _start_row_copies(tr, lambda r: pltpu.make_async_copy(x_hbm.at[src_ref[tile * tr + r]], xbuf.at[s, r],
                                                              gsem.at[s]))

    def scatter(tile, s):
        _start_row_copies(tr, lambda r: pltpu.make_async_copy(ybuf.at[s, r], y_hbm.at[dst_ref[tile * tr + r]],
                                                              ssem.at[s]))

    @pl.when((i == 0) & (j == 0))
    def _():
        ybuf[1] = jnp.zeros(ybuf.shape[1:], F32)
        spare = [pltpu.make_async_copy(ybuf.at[1], y_hbm.at[pl.ds(n_pairs + c * tr, tr)], zsem)
                 for c in range((y_hbm.shape[0] - n_pairs) // tr)]
        for cp in spare:
            cp.start()
        for cp in spare:
            cp.wait()

    @pl.when(valid & (j == 0))
    def _():
        @pl.when(i == 0)
        def _():
            gather(0, 0)

        _wait_row_copies(x_hbm, xbuf.at[slot], gsem.at[slot], to_hbm=False)
        xs_bf[...] = _from_slabs(lambda rows, cols: xbuf[slot, rows, cols, :], tr, xbuf.shape[2]).astype(BF16)

    @pl.when(valid & (fresh_ref[i] == 1))
    def _():
        wg_bf[j] = wg_ref[...].astype(BF16)
        wu_bf[j] = wu_ref[...].astype(BF16)
        wd_bf[j] = wd_ref[...].astype(BF16)

    @pl.when(valid)
    def _():
        nxt = jnp.minimum(i + 1, nv - 1)
        per_chunk = tr // nj
        x = xs_bf[...]
        hg = jnp.dot(x, wg_bf[j], preferred_element_type=F32)
        for u in range(per_chunk):
            r = j * per_chunk + u
            pltpu.make_async_copy(x_hbm.at[src_ref[nxt * tr + r]], xbuf.at[1 - slot, r],
                                  gsem.at[1 - slot]).start(priority=u % 2)
        hu = jnp.dot(x, wu_bf[j], preferred_element_type=F32)
        h = (_silu(hg) * hu).astype(BF16)
        part = jnp.dot(h, wd_bf[j], preferred_element_type=F32)

        @pl.when(j == 0)
        def _():
            acc_ref[...] = part

        @pl.when(j > 0)
        def _():
            acc_ref[...] = acc_ref[...] + part

    @pl.when(valid & last)
    def _():
        _to_slabs(ybuf.at[slot], acc_ref[...])
        scatter(i, slot)

        @pl.when(i > 0)
        def _():
            _wait_row_copies(y_hbm, ybuf.at[1 - slot], ssem.at[1 - slot], to_hbm=True)

        @pl.when(i == nv - 1)
        def _():
            _wait_row_copies(y_hbm, ybuf.at[slot], ssem.at[slot], to_hbm=True)
            _wait_row_copies(x_hbm, xbuf.at[1 - slot], gsem.at[1 - slot], to_hbm=False)


def expert_ffn(x_slabs, src_rows, dst_rows, tile_expert, n_valid, fresh, w_gate, w_up, w_down, layer, tr, n_out):
    T, S, _ = x_slabs.shape
    n_pairs = 2 * T
    D = S * LANE
    R = src_rows.shape[0]
    De = w_gate.shape[-1]
    te = _tile(De, (512, 256, 128))
    nj = De // te
    assert tr % ROW_DMA_UNROLL == 0
    chunk = lambda i, j, fr: jnp.where(fr[i] == 1, j, nj - 1)
    grid_spec = pltpu.PrefetchScalarGridSpec(
        num_scalar_prefetch=5,
        grid=(R // tr, nj),
        in_specs=[pl.BlockSpec(memory_space=pl.ANY),
                  pl.BlockSpec((None, None, D, te), lambda i, j, e, nv, fr, s, d: (layer, e[i], 0, chunk(i, j, fr))),
                  pl.BlockSpec((None, None, D, te), lambda i, j, e, nv, fr, s, d: (layer, e[i], 0, chunk(i, j, fr))),
                  pl.BlockSpec((None, None, te, D), lambda i, j, e, nv, fr, s, d: (layer, e[i], chunk(i, j, fr), 0))],
        out_specs=pl.BlockSpec(memory_space=pl.ANY),
        scratch_shapes=[pltpu.VMEM((nj, D, te), BF16), pltpu.VMEM((nj, D, te), BF16), pltpu.VMEM((nj, te, D), BF16),
                        pltpu.VMEM((2, tr, S, LANE), F32), pltpu.VMEM((tr, D), BF16), pltpu.VMEM((tr, D), F32),
                        pltpu.VMEM((2, tr, S, LANE), F32),
                        pltpu.SemaphoreType.DMA((2,)), pltpu.SemaphoreType.DMA((2,)), pltpu.SemaphoreType.DMA(())],
    )
    assert (n_out - n_pairs) % tr == 0
    return pl.pallas_call(
        functools.partial(_expert_kernel, tr=tr, nj=nj, n_pairs=n_pairs),
        grid_spec=grid_spec,
        out_shape=jax.ShapeDtypeStruct((n_out, S, LANE), F32),
        compiler_params=_params("arbitrary", "arbitrary"),
        name="moe_experts",
    )(tile_expert, n_valid, fresh, src_rows, dst_rows, x_slabs, w_gate, w_up, w_down)


def _combine_ln_kernel(y_ref, gate_ref, res_ref, g_ref, b_ref, o1_ref, o2_ref, *obf_ref, alpha, n_first):
    i = pl.program_id(0)
    tm, S2, _ = y_ref.shape
    S = S2 // 2
    shift = lambda cols, by: cols + by if isinstance(cols, int) else slice(cols.start + by, cols.stop + by)
    y0 = _from_slabs(lambda rows, cols: y_ref[rows, cols, :], tm, S)
    y1 = _from_slabs(lambda rows, cols: y_ref[rows, shift(cols, S), :], tm, S)
    moe = gate_ref[:, 0:1] * y0 + gate_ref[:, 1:2] * y1
    z = _layer_norm(alpha * res_ref[...] + moe, g_ref[...], b_ref[...])

    @pl.when(i < n_first)
    def _():
        o1_ref[...] = z

    @pl.when(i >= n_first)
    def _():
        o2_ref[...] = z

    if obf_ref:
        obf_ref[0][...] = z.astype(BF16)


def combine_ln(y_pairs, gates_t, res, g, b, alpha, split_rows, with_bf16):
    T, D = res.shape
    S2 = y_pairs.shape[1]
    tm = _tile(math.gcd(split_rows, T - split_rows), (256, 128))
    row = lambda i: (i, 0)
    n1 = split_rows // tm
    assert 0 < n1 < T // tm
    out_specs = [pl.BlockSpec((tm, D), lambda i: (jnp.minimum(i, n1 - 1), 0)),
                 pl.BlockSpec((tm, D), lambda i: (jnp.maximum(i - n1, 0), 0))]
    out_shape = [jax.ShapeDtypeStruct((split_rows, D), F32), jax.ShapeDtypeStruct((T - split_rows, D), F32)]
    if with_bf16:
        out_specs.append(pl.BlockSpec((tm, D), row))
        out_shape.append(jax.ShapeDtypeStruct((T, D), BF16))
    return pl.pallas_call(
        functools.partial(_combine_ln_kernel, alpha=alpha, n_first=n1),
        grid=(T // tm,),
        in_specs=[pl.BlockSpec((tm, S2, LANE), lambda i: (i, 0, 0)),
                  pl.BlockSpec((tm, 8), row),
                  pl.BlockSpec((tm, D), row),
                  pl.BlockSpec((1, D), lambda i: (0, 0)),
                  pl.BlockSpec((1, D), lambda i: (0, 0))],
        out_specs=out_specs,
        out_shape=out_shape,
        compiler_params=_params("arbitrary"),
        name="moe_combine_ln",
    )(y_pairs, gates_t, res, g.reshape(1, D), b.reshape(1, D))


def _dispatch_plan(e_idx, counts, E, tr):
    T = e_idx.shape[1]
    pairs = 2 * T
    e_flat = e_idx[0:2, :].T.reshape(pairs)
    rank = e_idx[2:4, :].T.reshape(pairs)
    padded = ((counts + tr - 1) // tr) * tr
    pstart = jnp.cumsum(padded) - padded
    pos = (pstart[e_flat] + rank).astype(jnp.int32)
    n_tiles = pairs // tr + E
    R = n_tiles * tr
    pair_of_row = jnp.full((R,), -1, jnp.int32).at[pos].set(jnp.arange(pairs, dtype=jnp.int32), unique_indices=True)
    n_valid = (jnp.sum(padded) // tr).astype(jnp.int32).reshape(1)
    tile_end = (pstart + padded) // tr
    tiles = jnp.arange(n_tiles, dtype=jnp.int32)
    tile_e = jnp.minimum(jnp.sum((tiles[:, None] >= tile_end[None, :]).astype(jnp.int32), axis=1), E - 1)
    spare = pairs + tile_e * tr - pstart[tile_e] - counts[tile_e]
    spare_row = jnp.repeat(spare, tr) + jnp.arange(R, dtype=jnp.int32)
    is_pair = pair_of_row >= 0
    src_rows = jnp.where(is_pair, pair_of_row // 2, 0).astype(jnp.int32)
    dst_rows = jnp.where(is_pair, pair_of_row, spare_row).astype(jnp.int32)
    last_e = tile_e[jnp.maximum(n_valid[0] - 1, 0)]
    tile_e = jnp.where(tiles < n_valid[0], tile_e, last_e).astype(jnp.int32)
    fresh = jnp.concatenate([jnp.ones((1,), jnp.int32), (tile_e[1:] != tile_e[:-1]).astype(jnp.int32)])
    return src_rows, dst_rows, tile_e, n_valid, fresh, R


def moe_ln(x, x_slabs, x_res_scale, w_router, b_router, w_gate, w_up, w_down, layer, g, b, split_rows, with_bf16):
    T, D = x.shape
    S = D // LANE
    E = w_router.shape[1]
    tr = _tile(2 * T, (256, 128))
    e_idx, gates, counts = router(x, w_router, b_router)
    src_rows, dst_rows, tile_e, n_valid, fresh, R = _dispatch_plan(e_idx, counts[:, 0], E, tr)
    ys = expert_ffn(x_slabs, src_rows, dst_rows, tile_e, n_valid, fresh, w_gate, w_up, w_down, layer, tr, R)
    return combine_ln(ys.reshape(R // 2, 2 * S, LANE), gates.T, x, g, b, x_res_scale, split_rows, with_bf16)


def kernel(x_prompt, x_sample, cache_k, cache_v, cache_logf, state_conv, state_hgrn, page_table, w_in_even, b_fgate, w_dw, b_dw, g_cnorm, b_cnorm, w_out_even, w_in_odd, lb_logits, g_onorm, g_vnorm, b_vnorm, w_sgu, b_sgu, w_out_odd, ln1_g, ln1_b, ln2_g, ln2_b, w_router, b_router, w_gate, w_up, w_down):
    Bp, Lp, D = x_prompt.shape
    Bs, Ls, _ = x_sample.shape
    Dh = D // 2
    H = b_fgate.shape[1]
    HG = state_hgrn.shape[2]
    assert Dh == H * LANE and Dh == HG * LANE and Dh == w_sgu.shape[1] * LANE
    depth = ln1_g.shape[0]
    alpha = (2 * depth) ** 0.25
    Tp, Ts = Bp * Lp, Bs * Ls
    n_even = cache_k.shape[0]
    n_phys, page = cache_k.shape[1], cache_k.shape[2]

    lb_p = jax.nn.softmax(lb_logits.astype(F32), axis=0)
    lb_all = jnp.cumsum(lb_p, axis=0) - lb_p[0]

    res = (x_prompt.reshape(Tp, D), x_sample.reshape(Ts, D))
    x_bf = jnp.concatenate([r.astype(BF16) for r in res], axis=0)
    rin, tot = page_suffix(cache_logf.astype(F32).reshape(n_even * n_phys, page * H), H)
    rin = rin.reshape(n_even * n_phys, 1, page * H)
    tot = tot.reshape(n_even * n_phys, 1, page * H)

    out = {k: [] for k in ("kp", "vp", "lfp", "convp", "hgp", "ks", "vs", "lfs", "convs", "hgs", "mlpv")}
    for l in range(depth):
        j = l // 2
        if l % 2 == 0:
            w_in = w_in_even[j]
            ag = matmul_cols(x_bf, w_in_even, j, 0, 2 * Dh)
            q = matmul_cols(x_bf, w_in_even, j, 2 * Dh, Dh)
            k_p, k_s = matmul_cols(x_bf, w_in_even, j, 3 * Dh, Dh, split_rows=Tp)
            v_p, v_s = matmul_cols(x_bf, w_in_even, j, 4 * Dh, Dh, split_rows=Tp)
            lf, _, c, ct = logf_project(x_bf, w_in[:, 5 * Dh:], b_fgate[j], Lp)
            conv_args = (w_dw[j], b_dw[j], g_cnorm[j], b_cnorm[j])
            ca_p, cst_p = conv_prompt(ag, Bp, Lp, *conv_args)
            ca_s, cst_s = conv_sample(ag, Tp, Bs, Ls, state_conv[j], *conv_args)
            att_p = fox_prompt(q, k_p, v_p, c, ct, Bp, Lp, H)
            att_s = fox_sample(q, k_s, v_s, lf, Tp, Bs, Ls, cache_k, cache_v, rin, tot, j, page_table, H)
            h1 = (ca_p, ca_s)
            h2 = (att_p, att_s.astype(BF16))
            w_out = w_out_even[j]
            out["kp"].append(k_p.reshape(Bp, Lp, H, LANE))
            out["vp"].append(v_p.reshape(Bp, Lp, H, LANE))
            out["lfp"].append(lf[:Tp].reshape(Bp, Lp, H))
            out["convp"].append(cst_p)
            out["ks"].append(k_s.reshape(Bs, Ls, H, LANE))
            out["vs"].append(v_s.reshape(Bs, Ls, H, LANE))
            out["lfs"].append(lf[Tp:].reshape(Bs, Ls, H))
            out["convs"].append(cst_s)
        else:
            proj = matmul_cols(x_bf, w_in_odd, j, 0, 6 * Dh)
            o_p, s_p = hgrn(proj, 0, Bp, Lp, lb_all[l], g_onorm[j], None, HG)
            o_s, s_s = hgrn(proj, Tp, Bs, Ls, lb_all[l], g_onorm[j], state_hgrn[j].astype(F32), HG)
            mlp_args = (w_sgu[j], b_sgu[j], g_vnorm[j], b_vnorm[j])
            d_p, _ = gmlp(proj, 0, Tp, Lp, *mlp_args)
            d_s, v_s = gmlp(proj, Tp, Ts, Ls, *mlp_args)
            h1 = (o_p, o_s)
            h2 = (d_p, d_s)
            w_out = w_out_odd[j]
            out["hgp"].append(s_p)
            out["hgs"].append(s_s)
            out["mlpv"].append(v_s.reshape(Bs, Ls, Dh))
        x, x_slabs = outproj_ln(h1, h2, w_out.astype(BF16), res, ln1_g[l], ln1_b[l], alpha)
        more = l + 1 < depth
        res_p, res_s, *x_next = moe_ln(x, x_slabs, alpha, w_router, b_router, w_gate, w_up, w_down, l,
                                       ln2_g[l], ln2_b[l], Tp, more)
        res = (res_p, res_s)
        if more:
            x_bf = x_next[0]

    stack = lambda name: jnp.stack(out[name])
    return (res[0].reshape(Bp, Lp, D), res[1].reshape(Bs, Ls, D),
            stack("kp"), stack("vp"), stack("lfp"), stack("convp"), stack("hgp"),
            stack("ks"), stack("vs"), stack("lfs"), stack("convs"), stack("hgs"), stack("mlpv"))
```

```python
import functools
import math

import jax
import jax.numpy as jnp
from jax import lax
from jax.experimental import pallas as pl
from jax.experimental.pallas import tpu as pltpu

F32 = jnp.float32
BF16 = jnp.bfloat16
HIGHEST = lax.Precision.HIGHEST

LANE = 128
SUBLANES = 8
VMEM_LIMIT = 56 * 1024 * 1024
LN_EPS = 1e-5
HG_CHUNK = 64
HG_SUB = 16
CONV_HALO = 32
N_GROUPS = 4
NT = (((1,), (1,)), ((), ()))
TN = (((0,), (0,)), ((), ()))


def _params(*sem):
    return pltpu.CompilerParams(dimension_semantics=sem, vmem_limit_bytes=VMEM_LIMIT)


def _tile(n, prefs):
    for t in prefs:
        if n % t == 0:
            return t
    raise ValueError(f"no tile in {prefs} divides {n}")


def _sigmoid(x):
    return 1.0 / (1.0 + jnp.exp(-x))


def _silu(x):
    return x * _sigmoid(x)


def _gelu(x):
    return 0.5 * x * (1.0 + jnp.tanh(0.7978845608028654 * (x + 0.044715 * (x * x * x))))


def _log_sigmoid(x):
    return -(jnp.maximum(-x, 0.0) + jnp.log1p(jnp.exp(-jnp.abs(x))))


def _layer_norm(x, g, b):
    mu = jnp.mean(x, axis=-1, keepdims=True)
    xc = x - mu
    var = jnp.mean(xc * xc, axis=-1, keepdims=True)
    return xc * lax.rsqrt(var + LN_EPS) * g + b


def _iota(shape, dim):
    return lax.broadcasted_iota(jnp.int32, shape, dim)


def _matmul_kernel(x_ref, w_ref, o_ref, wbf_ref):
    @pl.when(pl.program_id(1) == 0)
    def _():
        wbf_ref[...] = w_ref[...].astype(BF16)

    o_ref[...] = jnp.dot(x_ref[...], wbf_ref[...], preferred_element_type=F32).astype(o_ref.dtype)


def _matmul_split_kernel(x_ref, w_ref, o1_ref, o2_ref, wbf_ref, *, n_first):
    i = pl.program_id(1)

    @pl.when(i == 0)
    def _():
        wbf_ref[...] = w_ref[...].astype(BF16)

    y = jnp.dot(x_ref[...], wbf_ref[...], preferred_element_type=F32)

    @pl.when(i < n_first)
    def _():
        o1_ref[...] = y

    @pl.when(i >= n_first)
    def _():
        o2_ref[...] = y


def matmul_cols(x_bf, w, layer, col_start, n_cols, split_rows=None):
    M, K = x_bf.shape
    tm = _tile(M if split_rows is None else math.gcd(split_rows, M - split_rows), (1024, 512, 256, 128))
    tn = _tile(n_cols, (1024, 512, 256, 128))
    assert col_start % tn == 0 and M % tm == 0
    off = col_start // tn
    common = dict(
        grid=(n_cols // tn, M // tm),
        in_specs=[pl.BlockSpec((tm, K), lambda j, i: (i, 0)),
                  pl.BlockSpec((None, K, tn), lambda j, i: (layer, 0, j + off))],
        scratch_shapes=[pltpu.VMEM((K, tn), BF16)],
        compiler_params=_params("arbitrary", "arbitrary"),
    )
    if split_rows is None:
        return pl.pallas_call(
            _matmul_kernel,
            out_specs=pl.BlockSpec((tm, tn), lambda j, i: (i, j)),
            out_shape=jax.ShapeDtypeStruct((M, n_cols), F32),
            name="matmul_cols", **common,
        )(x_bf, w)
    n1 = split_rows // tm
    assert (M - split_rows) % tm == 0 and 0 < n1 < M // tm
    return pl.pallas_call(
        functools.partial(_matmul_split_kernel, n_first=n1),
        out_specs=[pl.BlockSpec((tm, tn), lambda j, i: (jnp.minimum(i, n1 - 1), j)),
                   pl.BlockSpec((tm, tn), lambda j, i: (jnp.maximum(i - n1, 0), j))],
        out_shape=[jax.ShapeDtypeStruct((split_rows, n_cols), F32),
                   jax.ShapeDtypeStruct((M - split_rows, n_cols), F32)],
        name="matmul_cols_split", **common,
    )(x_bf, w)


def _logf_kernel(x_ref, w_ref, wt_ref, b_ref, bt_ref, lf_ref, lft_ref, c_ref, ct_ref,
                 carry_ref, carryt_ref, *, tiles_per_seq):
    i = pl.program_id(0)
    x = x_ref[...]
    tm = x.shape[0]
    fz = jnp.dot(x, w_ref[...].astype(BF16), preferred_element_type=F32) + b_ref[...]
    fzt = lax.dot_general(wt_ref[...].astype(BF16), x, NT, preferred_element_type=F32) + bt_ref[...]
    lf = _log_sigmoid(fz)
    lft = _log_sigmoid(fzt)
    lf_ref[...] = lf
    lft_ref[...] = lft

    @pl.when(i % tiles_per_seq == 0)
    def _():
        carry_ref[...] = jnp.zeros_like(carry_ref)
        carryt_ref[...] = jnp.zeros_like(carryt_ref)

    row = _iota((tm, tm), 0)
    col = _iota((tm, tm), 1)
    lower = (col <= row).astype(F32)
    upper = (row <= col).astype(F32)
    c = jnp.dot(lower, lf, precision=HIGHEST, preferred_element_type=F32) + carry_ref[...]
    ct = jnp.dot(lft, upper, precision=HIGHEST, preferred_element_type=F32) + carryt_ref[...]
    c_ref[...] = c
    ct_ref[...] = ct
    carry_ref[...] = c[tm - 1:tm, :]
    carryt_ref[...] = ct[:, tm - 1:tm]


def logf_project(x_bf, w_f, b_f, seq_len):
    T, D = x_bf.shape
    H = w_f.shape[1]
    tm = _tile(seq_len, (512, 256, 128))
    assert T % tm == 0
    outs = pl.pallas_call(
        functools.partial(_logf_kernel, tiles_per_seq=seq_len // tm),
        grid=(T // tm,),
        in_specs=[pl.BlockSpec((tm, D), lambda i: (i, 0)),
                  pl.BlockSpec((D, H), lambda i: (0, 0)),
                  pl.BlockSpec((H, D), lambda i: (0, 0)),
                  pl.BlockSpec((1, H), lambda i: (0, 0)),
                  pl.BlockSpec((H, 1), lambda i: (0, 0))],
        out_specs=[pl.BlockSpec((tm, H), lambda i: (i, 0)),
                   pl.BlockSpec((H, tm), lambda i: (0, i)),
                   pl.BlockSpec((tm, H), lambda i: (i, 0)),
                   pl.BlockSpec((H, tm), lambda i: (0, i))],
        out_shape=[jax.ShapeDtypeStruct((T, H), F32), jax.ShapeDtypeStruct((H, T), F32),
                   jax.ShapeDtypeStruct((T, H), F32), jax.ShapeDtypeStruct((H, T), F32)],
        scratch_shapes=[pltpu.VMEM((1, H), F32), pltpu.VMEM((H, 1), F32)],
        compiler_params=_params("arbitrary"),
        name="logf_project",
    )(x_bf, w_f, w_f.T, b_f.reshape(1, H), b_f.reshape(H, 1))
    return outs


def _conv_tail(acc, bdw, g, b):
    return _silu(_layer_norm(acc + bdw, g, b))


def _conv_prompt_kernel(ag_ref, wdw_ref, bdw_ref, g_ref, b_ref, y_ref, st_ref, ext_ref, sh_ref, *, W, tt, C):
    t = pl.program_id(1)

    @pl.when(t == 0)
    def _():
        ext_ref[0:CONV_HALO, :] = jnp.zeros((CONV_HALO, C), F32)
        ext_ref[CONV_HALO + tt:CONV_HALO + tt + SUBLANES, :] = jnp.zeros((SUBLANES, C), F32)

    @pl.when(t > 0)
    def _():
        ext_ref[0:CONV_HALO, :] = ext_ref[tt:tt + CONV_HALO, :]

    u = ag_ref[:, 0:C] * _sigmoid(ag_ref[:, C:2 * C])
    ext_ref[CONV_HALO:CONV_HALO + tt, :] = u
    base = CONV_HALO - (W - 1)
    acc = None
    for r in range(SUBLANES):
        taps = [w for w in range(W) if (base + w) % SUBLANES == r]
        if not taps:
            continue
        sh_ref[...] = ext_ref[r:r + CONV_HALO + tt, :]
        for w in taps:
            a = base + w - r
            term = sh_ref[a:a + tt, :] * wdw_ref[w:w + 1, :]
            acc = term if acc is None else acc + term
    y_ref[...] = _conv_tail(acc, bdw_ref[...], g_ref[...], b_ref[...]).astype(y_ref.dtype)

    @pl.when(t == pl.num_programs(1) - 1)
    def _():
        st_ref[...] = ext_ref[CONV_HALO + tt - (W - 1):CONV_HALO + tt, :]


def conv_prompt(ag, n_seq, seq_len, w_dw, b_dw, g_n, b_n):
    W, C = w_dw.shape
    assert W - 1 <= CONV_HALO
    tt = _tile(seq_len, (256, 128))
    nt = seq_len // tt
    vec = lambda a: a.reshape(1, C)
    return pl.pallas_call(
        functools.partial(_conv_prompt_kernel, W=W, tt=tt, C=C),
        grid=(n_seq, nt),
        in_specs=[pl.BlockSpec((tt, 2 * C), lambda b, t: (b * nt + t, 0)),
                  pl.BlockSpec((W, C), lambda b, t: (0, 0)),
                  pl.BlockSpec((1, C), lambda b, t: (0, 0)),
                  pl.BlockSpec((1, C), lambda b, t: (0, 0)),
                  pl.BlockSpec((1, C), lambda b, t: (0, 0))],
        out_specs=[pl.BlockSpec((tt, C), lambda b, t: (b * nt + t, 0)),
                   pl.BlockSpec((None, W - 1, C), lambda b, t: (b, 0, 0))],
        out_shape=[jax.ShapeDtypeStruct((n_seq * seq_len, C), BF16),
                   jax.ShapeDtypeStruct((n_seq, W - 1, C), F32)],
        scratch_shapes=[pltpu.VMEM((CONV_HALO + tt + SUBLANES, C), F32), pltpu.VMEM((CONV_HALO + tt, C), F32)],
        compiler_params=_params("arbitrary", "arbitrary"),
        name="conv_prompt",
    )(ag, w_dw, vec(b_dw), vec(g_n), vec(b_n))


def _conv_sample_kernel(ag_ref, st_ref, wdw_ref, bdw_ref, g_ref, b_ref, y_ref, nst_ref, ext_ref, *, W, Ls, C, bb):
    u = ag_ref[:, 0:C] * _sigmoid(ag_ref[:, C:2 * C])
    ext_ref[:, 0:W - 1, :] = st_ref[...]
    ext_ref[:, W - 1:W - 1 + Ls, :] = u.reshape(bb, Ls, C)
    acc = ext_ref[:, 0:Ls, :] * wdw_ref[0:1, :]
    for w in range(1, W):
        acc = acc + ext_ref[:, w:w + Ls, :] * wdw_ref[w:w + 1, :]
    y = _conv_tail(acc.reshape(bb * Ls, C), bdw_ref[...], g_ref[...], b_ref[...])
    y_ref[...] = y.astype(y_ref.dtype)
    nst_ref[...] = ext_ref[:, Ls:Ls + W - 1, :]


def conv_sample(ag, row_start, n_seq, Ls, state, w_dw, b_dw, g_n, b_n):
    W, C = w_dw.shape
    bb = _tile(n_seq, (8, 4, 2, 1))
    rows = bb * Ls
    assert Ls % 8 == 0 and row_start % rows == 0
    r0 = row_start // rows
    vec = lambda a: a.reshape(1, C)
    return pl.pallas_call(
        functools.partial(_conv_sample_kernel, W=W, Ls=Ls, C=C, bb=bb),
        grid=(n_seq // bb,),
        in_specs=[pl.BlockSpec((rows, 2 * C), lambda i: (r0 + i, 0)),
                  pl.BlockSpec((bb, W - 1, C), lambda i: (i, 0, 0)),
                  pl.BlockSpec((W, C), lambda i: (0, 0)),
                  pl.BlockSpec((1, C), lambda i: (0, 0)),
                  pl.BlockSpec((1, C), lambda i: (0, 0)),
                  pl.BlockSpec((1, C), lambda i: (0, 0))],
        out_specs=[pl.BlockSpec((rows, C), lambda i: (i, 0)),
                   pl.BlockSpec((bb, W - 1, C), lambda i: (i, 0, 0))],
        out_shape=[jax.ShapeDtypeStruct((n_seq * Ls, C), BF16),
                   jax.ShapeDtypeStruct((n_seq, W - 1, C), F32)],
        scratch_shapes=[pltpu.VMEM((bb, W - 1 + Ls, C), F32)],
        compiler_params=_params("arbitrary"),
        name="conv_sample",
    )(ag, state, w_dw, vec(b_dw), vec(g_n), vec(b_n))


def _fox_prompt_kernel(q_ref, k_ref, v_ref, c_ref, ct_ref, o_ref, *, tq, scale, H):
    h = pl.program_id(1)
    qi = pl.program_id(2)
    q = (q_ref[...] * scale).astype(BF16)
    cq = jnp.sum(jnp.where(_iota((tq, H), 1) == h, c_ref[...], 0.0), axis=1, keepdims=True)

    def step(ki, carry, on_diagonal):
        m, l, acc = carry
        start = pl.multiple_of(ki * tq, tq)
        k = k_ref[pl.ds(start, tq), :].astype(BF16)
        v = v_ref[pl.ds(start, tq), :].astype(BF16)
        ck = ct_ref[pl.ds(h, 1), pl.ds(start, tq)]
        s = lax.dot_general(q, k, NT, preferred_element_type=F32) + cq - ck
        if on_diagonal:
            s = jnp.where(_iota((tq, tq), 0) >= _iota((tq, tq), 1), s, -jnp.inf)
        m_new = jnp.maximum(m, jnp.max(s, axis=1, keepdims=True))
        alpha = jnp.exp(m - m_new)
        p = jnp.exp(s - m_new)
        l = alpha * l + jnp.sum(p, axis=1, keepdims=True)
        acc = alpha * acc + jnp.dot(p.astype(BF16), v, preferred_element_type=F32)
        return m_new, l, acc

    init = (jnp.full((tq, 1), -jnp.inf, F32), jnp.zeros((tq, 1), F32), jnp.zeros((tq, LANE), F32))
    carry = lax.fori_loop(0, qi, lambda ki, c: step(ki, c, False), init)
    _, l, acc = step(qi, carry, True)
    o_ref[...] = (acc / l).astype(o_ref.dtype)


def fox_prompt(q, k, v, c, ct, n_seq, seq_len, H):
    tq = _tile(seq_len, (512, 256, 128))
    nq = seq_len // tq
    return pl.pallas_call(
        functools.partial(_fox_prompt_kernel, tq=tq, scale=LANE ** -0.5, H=H),
        grid=(n_seq, H, nq),
        in_specs=[pl.BlockSpec((tq, LANE), lambda b, h, i: (b * nq + i, h)),
                  pl.BlockSpec((seq_len, LANE), lambda b, h, i: (b, h)),
                  pl.BlockSpec((seq_len, LANE), lambda b, h, i: (b, h)),
                  pl.BlockSpec((tq, H), lambda b, h, i: (b * nq + i, 0)),
                  pl.BlockSpec((H, seq_len), lambda b, h, i: (0, b))],
        out_specs=pl.BlockSpec((tq, LANE), lambda b, h, i: (b * nq + i, h)),
        out_shape=jax.ShapeDtypeStruct((n_seq * seq_len, H * LANE), BF16),
        compiler_params=_params("arbitrary", "arbitrary", "arbitrary"),
        name="fox_prompt",
    )(q, k, v, c, ct)


def _split3_dot(x, w_bf, w_left=False):
    x1 = x.astype(BF16)
    r1 = x - x1.astype(F32)
    x2 = r1.astype(BF16)
    x3 = (r1 - x2.astype(F32)).astype(BF16)
    if w_left:
        dot = lambda a: jnp.dot(w_bf, a, preferred_element_type=F32)
    else:
        dot = lambda a: jnp.dot(a, w_bf, preferred_element_type=F32)
    return dot(x1) + dot(x2) + dot(x3)


def _page_suffix_kernel(lf_ref, rin_ref, tot_ref, later_ref, same_ref, *, H):
    n = lf_ref.shape[1]

    @pl.when(pl.program_id(0) == 0)
    def _():
        r = _iota((n, n), 0)
        c = _iota((n, n), 1)
        same = (r % H) == (c % H)
        later_ref[...] = jnp.where(same & (r > c), 1.0, 0.0).astype(BF16)
        same_ref[...] = jnp.where(same, 1.0, 0.0).astype(BF16)

    x = lf_ref[...]
    rin_ref[...] = _split3_dot(x, later_ref[...])
    tot_ref[...] = _split3_dot(x, same_ref[...])


def page_suffix(lf_pages, H):
    P, n = lf_pages.shape
    tp = _tile(P, (256, 128, 64, 32, 16, 8))
    return pl.pallas_call(
        functools.partial(_page_suffix_kernel, H=H),
        grid=(P // tp,),
        in_specs=[pl.BlockSpec((tp, n), lambda i: (i, 0))],
        out_specs=[pl.BlockSpec((tp, n), lambda i: (i, 0)), pl.BlockSpec((tp, n), lambda i: (i, 0))],
        out_shape=[jax.ShapeDtypeStruct((P, n), F32), jax.ShapeDtypeStruct((P, n), F32)],
        scratch_shapes=[pltpu.VMEM((n, n), BF16), pltpu.VMEM((n, n), BF16)],
        compiler_params=_params("arbitrary"),
        name="page_suffix",
    )(lf_pages)


def _fox_sample_kernel(*refs, H, Ls, scale, pps):
    pt_ref, q_ref, kn_ref, vn_ref, lfn_ref = refs[0:5]
    ck_refs = refs[5:5 + pps]
    cv_refs = refs[5 + pps:5 + 2 * pps]
    rin_refs = refs[5 + 2 * pps:5 + 3 * pps]
    tot_refs = refs[5 + 3 * pps:5 + 4 * pps]
    o_ref, qh_ref, a_ref, m_ref, l_ref, acc_ref, rc_ref = refs[5 + 4 * pps:]
    p = pl.program_id(1)
    HQ = H * Ls
    cols = ck_refs[0].shape[0] * H
    sel = (_iota((HQ, H), 0) // Ls == _iota((HQ, H), 1)).astype(F32)
    causal = _iota((HQ, Ls), 1) <= _iota((HQ, Ls), 0) % Ls

    def new_logf():
        return lax.dot_general(sel, lfn_ref[...], NT, precision=HIGHEST, preferred_element_type=F32)

    @pl.when(p == 0)
    def _():
        q = q_ref[...]
        qh_ref[...] = jnp.concatenate([q[:, h * LANE:(h + 1) * LANE] for h in range(H)], axis=0).astype(BF16)
        a_ref[...] = jnp.sum(jnp.where(causal, new_logf(), 0.0), axis=1, keepdims=True)
        m_ref[...] = jnp.full((HQ, 1), -jnp.inf, F32)
        l_ref[...] = jnp.zeros((HQ, 1), F32)
        acc_ref[...] = jnp.zeros((HQ, LANE), F32)
        rc_ref[...] = jnp.zeros((1, cols), F32)

    def update(scores, pvs):
        m_old = m_ref[...]
        m_new = m_old
        for s in scores:
            m_new = jnp.maximum(m_new, jnp.max(s, axis=1, keepdims=True))
        alpha = jnp.exp(m_old - m_new)
        l_new = alpha * l_ref[...]
        acc = alpha * acc_ref[...]
        for s, pv in zip(scores, pvs):
            pr = jnp.exp(s - m_new)
            l_new = l_new + jnp.sum(pr, axis=1, keepdims=True)
            acc = acc + pv(pr.astype(BF16))
        l_ref[...] = l_new
        acc_ref[...] = acc
        m_ref[...] = m_new

    qh = qh_ref[...]
    own_head = (_iota((HQ, cols), 0) // Ls) == (_iota((HQ, cols), 1) % H)
    rc = rc_ref[...]
    scores, pvs = [], []
    for s_ in range(pps):
        kx = ck_refs[s_][...].reshape(cols, LANE).astype(BF16)
        vx = cv_refs[s_][...].reshape(cols, LANE).astype(BF16)
        bias = a_ref[...] + (rin_refs[s_][...] + rc)
        rc = rc + tot_refs[s_][...]
        sc = lax.dot_general(qh, kx, NT, preferred_element_type=F32) * scale + bias
        scores.append(jnp.where(own_head, sc, -jnp.inf))
        pvs.append(lambda pr, vx=vx: jnp.dot(pr, vx, preferred_element_type=F32))
    rc_ref[...] = rc
    update(scores, pvs)

    @pl.when(p == pl.num_programs(1) - 1)
    def _():
        head = lambda a, h: a[:, h * LANE:(h + 1) * LANE].astype(BF16)
        rows = lambda a, h: a[h * Ls:(h + 1) * Ls, :]
        kn = kn_ref[...]
        vn = vn_ref[...]
        s2 = jnp.concatenate([lax.dot_general(rows(qh, h), head(kn, h), NT, preferred_element_type=F32)
                              for h in range(H)], axis=0)
        upto = (_iota((Ls, Ls), 0) <= _iota((Ls, Ls), 1)).astype(F32)
        cum = jnp.dot(new_logf(), upto, precision=HIGHEST, preferred_element_type=F32)
        s2 = jnp.where(causal, s2 * scale + (a_ref[...] - cum), -jnp.inf)
        update([s2], [lambda pr: jnp.concatenate(
            [jnp.dot(rows(pr, h), head(vn, h), preferred_element_type=F32) for h in range(H)], axis=0)])
        out = acc_ref[...] / l_ref[...]
        for h in range(H):
            o_ref[:, h * LANE:(h + 1) * LANE] = rows(out, h)


def fox_sample(q, k, v, lf, row_start, n_seq, Ls, cache_k, cache_v, rin, tot, layer, page_table, H):
    D = H * LANE
    n_pages = page_table.shape[1]
    n_phys, page = cache_k.shape[1], cache_k.shape[2]
    pps = _tile(n_pages, (16, 8, 4, 2, 1))
    assert Ls % 8 == 0 and row_start % Ls == 0
    r0 = row_start // Ls
    HQ = H * Ls
    rows = lambda b, p, pt: (r0 + b, 0)

    def phys(s_):
        return lambda b, p, pt: pt[b * n_pages + (n_pages - 1 - (p * pps + s_))]

    kv_spec = lambda s_: pl.BlockSpec((None, None, page, H, LANE),
                                      lambda b, p, pt: (layer, phys(s_)(b, p, pt), 0, 0, 0))
    row_spec = lambda s_: pl.BlockSpec((None, 1, page * H),
                                       lambda b, p, pt: (layer * n_phys + phys(s_)(b, p, pt), 0, 0))
    grid_spec = pltpu.PrefetchScalarGridSpec(
        num_scalar_prefetch=1,
        grid=(n_seq, n_pages // pps),
        in_specs=([pl.BlockSpec((Ls, D), rows), pl.BlockSpec((Ls, D), lambda b, p, pt: (b, 0)),
                   pl.BlockSpec((Ls, D), lambda b, p, pt: (b, 0)), pl.BlockSpec((Ls, H), rows)]
                  + [kv_spec(s_) for s_ in range(pps)] + [kv_spec(s_) for s_ in range(pps)]
                  + [row_spec(s_) for s_ in range(pps)] + [row_spec(s_) for s_ in range(pps)]),
        out_specs=pl.BlockSpec((Ls, D), lambda b, p, pt: (b, 0)),
        scratch_shapes=[pltpu.VMEM((HQ, LANE), BF16), pltpu.VMEM((HQ, 1), F32), pltpu.VMEM((HQ, 1), F32),
                        pltpu.VMEM((HQ, 1), F32), pltpu.VMEM((HQ, LANE), F32), pltpu.VMEM((1, page * H), F32)],
    )
    return pl.pallas_call(
        functools.partial(_fox_sample_kernel, H=H, Ls=Ls, scale=LANE ** -0.5, pps=pps),
        grid_spec=grid_spec,
        out_shape=jax.ShapeDtypeStruct((n_seq * Ls, D), F32),
        compiler_params=_params("arbitrary", "arbitrary"),
        name="fox_sample",
    )(page_table.reshape(-1), q, k, v, lf, *([cache_k] * pps), *([cache_v] * pps),
      *([rin] * pps), *([tot] * pps))


def _transpose8(v):
    v = list(v)
    sub = _iota(v[0].shape, 1)
    for s in (4, 2, 1):
        keep = (sub & s) == 0
        for k in range(SUBLANES):
            if k & s:
                continue
            lo, hi = v[k], v[k + s]
            v[k] = jnp.where(keep, lo, pltpu.roll(hi, s, axis=1))
            v[k + s] = jnp.where(keep, pltpu.roll(lo, SUBLANES - s, axis=1), hi)
    return v


def _every8(t, n):
    return pl.ds(t, n // SUBLANES, stride=SUBLANES)


def _to_slabs(ref, x):
    n, S, _ = ref.shape
    if S % SUBLANES or n % SUBLANES:
        for k in range(S):
            ref[:, k, :] = x[:, k * LANE:(k + 1) * LANE]
        return
    for h in range(S // SUBLANES):
        v = [x[:, (SUBLANES * h + k) * LANE:(SUBLANES * h + k + 1) * LANE].reshape(n // SUBLANES, SUBLANES, LANE)
             for k in range(SUBLANES)]
        for t, w in enumerate(_transpose8(v)):
            ref[_every8(t, n), SUBLANES * h:SUBLANES * (h + 1), :] = w


def _from_slabs(load, n, S):
    if S % SUBLANES or n % SUBLANES:
        return jnp.concatenate([load(slice(None), k) for k in range(S)], axis=1)
    tiles = []
    for h in range(S // SUBLANES):
        w = [load(_every8(t, n), slice(SUBLANES * h, SUBLANES * (h + 1))) for t in range(SUBLANES)]
        tiles += [a.reshape(n, LANE) for a in _transpose8(w)]
    return jnp.concatenate(tiles, axis=1)


def _outproj_ln_kernel(h1p_ref, h1s_ref, h2p_ref, h2s_ref, w_ref, resp_ref, ress_ref, g_ref, b_ref, o_ref,
                       oslab_ref, *, alpha, half, n_first):
    first = pl.program_id(0) < n_first
    h1 = jnp.where(first, h1p_ref[...], h1s_ref[...])
    h2 = jnp.where(first, h2p_ref[...], h2s_ref[...])
    res = jnp.where(first, resp_ref[...], ress_ref[...])
    y = jnp.dot(h1, w_ref[0:half, :], preferred_element_type=F32)
    y = y + jnp.dot(h2, w_ref[half:2 * half, :], preferred_element_type=F32)
    z = _layer_norm(alpha * res + y, g_ref[...], b_ref[...])
    o_ref[...] = z
    _to_slabs(oslab_ref, z)


def outproj_ln(h1, h2, w_bf, res, g, b, alpha):
    (h1p, h1s), (h2p, h2s), (resp, ress) = h1, h2, res
    half = h1p.shape[1]
    D = resp.shape[1]
    T = resp.shape[0] + ress.shape[0]
    S = D // LANE
    tm = _tile(math.gcd(resp.shape[0], ress.shape[0]), (256, 128))
    n1 = resp.shape[0] // tm
    assert h1p.shape[0] == resp.shape[0] and h1s.shape[0] == ress.shape[0]
    first = lambda i: (jnp.minimum(i, n1 - 1), 0)
    second = lambda i: (jnp.maximum(i - n1, 0), 0)
    return pl.pallas_call(
        functools.partial(_outproj_ln_kernel, alpha=alpha, half=half, n_first=n1),
        grid=(T // tm,),
        in_specs=[pl.BlockSpec((tm, half), first), pl.BlockSpec((tm, half), second),
                  pl.BlockSpec((tm, half), first), pl.BlockSpec((tm, half), second),
                  pl.BlockSpec((2 * half, D), lambda i: (0, 0)),
                  pl.BlockSpec((tm, D), first), pl.BlockSpec((tm, D), second),
                  pl.BlockSpec((1, D), lambda i: (0, 0)),
                  pl.BlockSpec((1, D), lambda i: (0, 0))],
        out_specs=[pl.BlockSpec((tm, D), lambda i: (i, 0)), pl.BlockSpec((tm, S, LANE), lambda i: (i, 0, 0))],
        out_shape=[jax.ShapeDtypeStruct((T, D), F32), jax.ShapeDtypeStruct((T, S, LANE), F32)],
        compiler_params=_params("arbitrary"),
        name="outproj_ln",
    )(h1p, h1s, h2p, h2s, w_bf, resp, ress, g.reshape(1, D), b.reshape(1, D))


def _hgrn_kernel(*refs, C, sb, tl, bb, hh, has_s0):
    if has_s0:
        qz_ref, fz_ref, iz_ref, gz_ref, lb_ref, go_ref, s0_ref, o_ref, sn_ref, st_ref = refs
    else:
        qz_ref, fz_ref, iz_ref, gz_ref, lb_ref, go_ref, o_ref, sn_ref, st_ref = refs
    t = pl.program_id(2)
    nsb = C // sb

    @pl.when(t == 0)
    def _():
        for s in range(bb):
            for h in range(hh):
                st_ref[s, h] = s0_ref[s, h].T if has_s0 else jnp.zeros((LANE, LANE), F32)

    lower = (_iota((C, C), 1) <= _iota((C, C), 0))
    lower_bf = jnp.where(lower, 1.0, 0.0).astype(BF16)

    def head_chunk(qz, fz, iz, gz, lb, go, st):
        q = _silu(qz)
        f = lb + (1.0 - lb) * _sigmoid(fz)
        kk = 1.0 - f
        i_bf = iz.astype(BF16)
        b = _split3_dot(jnp.log(f), lower_bf, w_left=True)
        b_last = b[C - 1:C, :]
        starts = [jnp.zeros((1, LANE), F32)] + [b[I * sb - 1:I * sb, :] for I in range(1, nsb)]
        lasts = [b[(I + 1) * sb - 1:(I + 1) * sb, :] for I in range(nsb)]
        blk = lambda a, I: a[I * sb:(I + 1) * sb, :]
        kd = [blk(kk, J) * jnp.exp(lasts[J] - blk(b, J)) for J in range(nsb)]
        att_rows = []
        for I in range(nsb):
            qd = blk(q, I) * jnp.exp(blk(b, I) - starts[I])
            parts = [kd[J] * jnp.exp(starts[I] - lasts[J]) for J in range(I)]
            parts.append(blk(kk, I) * jnp.exp(starts[I] - blk(b, I)))
            if I + 1 < nsb:
                parts.append(jnp.zeros(((nsb - I - 1) * sb, LANE), F32))
            kmat = jnp.concatenate(parts, axis=0) if len(parts) > 1 else parts[0]
            att_rows.append(lax.dot_general(qd.astype(BF16), kmat.astype(BF16), NT, preferred_element_type=F32))
        att = jnp.concatenate(att_rows, axis=0) if nsb > 1 else att_rows[0]
        att = jnp.where(lower, att, 0.0)
        o = jnp.dot(att.astype(BF16), i_bf, preferred_element_type=F32)
        o = o + lax.dot_general((q * jnp.exp(b)).astype(BF16), st.astype(BF16), NT, preferred_element_type=F32)
        kst = (kk * jnp.exp(b_last - b)).astype(BF16)
        st_new = st * jnp.exp(b_last) + lax.dot_general(i_bf, kst, TN, preferred_element_type=F32)
        ms = jnp.mean(o * o, axis=-1, keepdims=True)
        return o * lax.rsqrt(ms + LN_EPS) * go * _silu(gz), st_new

    def chunk(s, r0):
        qz, fz, iz, gz = (ref[pl.ds(r0, C), :] for ref in (qz_ref, fz_ref, iz_ref, gz_ref))
        lb, go = lb_ref[...], go_ref[...]
        head = lambda a, h: a[:, h * LANE:(h + 1) * LANE]
        res = [head_chunk(*(head(a, h) for a in (qz, fz, iz, gz, lb, go)), st_ref[s, h]) for h in range(hh)]
        o = jnp.concatenate([r[0] for r in res], axis=1) if hh > 1 else res[0][0]
        return o.astype(o_ref.dtype), jnp.stack([r[1] for r in res])

    if tl == C:
        res = [chunk(s, s * tl) for s in range(bb)]
        o_ref[...] = jnp.concatenate([r[0] for r in res], axis=0) if bb > 1 else res[0][0]
        st_ref[...] = jnp.stack([r[1] for r in res])
    else:
        assert bb == 1

        def body(c, carry):
            r0 = pl.multiple_of(c * C, C)
            o, st_new = chunk(0, r0)
            o_ref[pl.ds(r0, C), :] = o
            st_ref[0] = st_new
            return carry

        lax.fori_loop(0, tl // C, body, 0)

    @pl.when(t == pl.num_programs(2) - 1)
    def _():
        for s in range(bb):
            for h in range(hh):
                sn_ref[s, h] = st_ref[s, h].T


def hgrn(proj, row_start, n_seq, seq_len, lb, g_o, s0, H):
    C = min(HG_CHUNK, seq_len)
    sb = min(HG_SUB, C)
    assert seq_len % C == 0 and C % sb == 0
    if seq_len >= 512:
        tl, bb, hh = _tile(seq_len, (512,)), 1, _tile(H, (8, 4, 2, 1))
    else:
        assert seq_len == C
        tl, bb, hh = seq_len, _tile(n_seq, (32, 16, 8, 4, 2, 1)), 1
    nt = seq_len // tl
    rows = bb * tl
    W = hh * LANE
    assert row_start % rows == 0
    r0 = row_start // rows
    col = lambda k: (lambda b, h, t: (r0 + b * nt + t, k * (H // hh) + h))
    in_specs = [pl.BlockSpec((rows, W), col(0)), pl.BlockSpec((rows, W), col(1)),
                pl.BlockSpec((rows, W), col(2)), pl.BlockSpec((rows, W), col(3)),
                pl.BlockSpec((1, W), lambda b, h, t: (0, h)),
                pl.BlockSpec((1, W), lambda b, h, t: (0, h))]
    args = [proj, proj, proj, proj, lb.reshape(1, H * LANE), g_o.reshape(1, H * LANE)]
    if s0 is not None:
        in_specs.append(pl.BlockSpec((bb, hh, LANE, LANE), lambda b, h, t: (b, h, 0, 0)))
        args.append(s0)
    return pl.pallas_call(
        functools.partial(_hgrn_kernel, C=C, sb=sb, tl=tl, bb=bb, hh=hh, has_s0=s0 is not None),
        grid=(n_seq // bb, H // hh, nt),
        in_specs=in_specs,
        out_specs=[pl.BlockSpec((rows, W), lambda b, h, t: (b * nt + t, h)),
                   pl.BlockSpec((bb, hh, LANE, LANE), lambda b, h, t: (b, h, 0, 0))],
        out_shape=[jax.ShapeDtypeStruct((n_seq * seq_len, H * LANE), BF16),
                   jax.ShapeDtypeStruct((n_seq, H, LANE, LANE), F32)],
        scratch_shapes=[pltpu.VMEM((bb, hh, LANE, LANE), F32)],
        compiler_params=_params("arbitrary", "arbitrary", "arbitrary"),
        name="hgrn",
    )(*args)


def _gmlp_kernel(uz_ref, vz_ref, w_ref, bt_ref, g_ref, b_ref, d_ref, v_ref, *, G, cs):
    n = uz_ref.shape[0]
    v = _layer_norm(_gelu(vz_ref[...]), g_ref[...], b_ref[...])
    v_ref[...] = v
    u = _gelu(uz_ref[...])
    t = _iota((n, n), 0)
    s = _iota((n, n), 1)
    keep = (s <= t) & (t // cs == s // cs)
    for g in range(G):
        w = jnp.where(keep, w_ref[g], 0.0).astype(BF16)
        vg = v[:, g * LANE:(g + 1) * LANE].astype(BF16)
        mixed = jnp.dot(w, vg, preferred_element_type=F32) + bt_ref[:, g:g + 1]
        d_ref[:, g * LANE:(g + 1) * LANE] = (u[:, g * LANE:(g + 1) * LANE] * mixed).astype(d_ref.dtype)


def gmlp(proj, row_start, n_rows, seq_len, w_s, b_s, g_v, b_v):
    G, n, _ = w_s.shape
    Dh = G * LANE
    cs = min(n, seq_len)
    assert n % cs == 0 and seq_len % cs == 0 and n_rows % n == 0 and row_start % n == 0
    if cs < n:
        w_s = jnp.tile(w_s[:, :cs, :cs], (1, n // cs, n // cs))
        b_s = jnp.tile(b_s[:, :cs], (1, n // cs))
    r0 = row_start // n
    return pl.pallas_call(
        functools.partial(_gmlp_kernel, G=G, cs=cs),
        grid=(n_rows // n,),
        in_specs=[pl.BlockSpec((n, Dh), lambda i: (r0 + i, 4)),
                  pl.BlockSpec((n, Dh), lambda i: (r0 + i, 5)),
                  pl.BlockSpec((G, n, n), lambda i: (0, 0, 0)),
                  pl.BlockSpec((n, G), lambda i: (0, 0)),
                  pl.BlockSpec((1, Dh), lambda i: (0, 0)),
                  pl.BlockSpec((1, Dh), lambda i: (0, 0))],
        out_specs=[pl.BlockSpec((n, Dh), lambda i: (i, 0)), pl.BlockSpec((n, Dh), lambda i: (i, 0))],
        out_shape=[jax.ShapeDtypeStruct((n_rows, Dh), BF16), jax.ShapeDtypeStruct((n_rows, Dh), F32)],
        compiler_params=_params("arbitrary"),
        name="gmlp",
    )(proj, proj, w_s, b_s.T, g_v.reshape(1, Dh), b_v.reshape(1, Dh))


def _router_kernel(x_ref, wt_ref, bt_ref, idx_ref, gate_ref, cnt_ref, run_ref, *, E):
    tm = x_ref.shape[0]
    per = E // N_GROUPS
    logits = lax.dot_general(wt_ref[...], x_ref[...], NT, precision=HIGHEST, preferred_element_type=F32)
    z = jnp.exp(logits - jnp.max(logits, axis=0, keepdims=True))
    probs = z / jnp.sum(z, axis=0, keepdims=True)
    sel = probs + bt_ref[...]
    io = _iota((per, tm), 0)

    def top2(sg):
        m1 = jnp.max(sg, axis=0, keepdims=True)
        i1 = jnp.min(jnp.where(sg == m1, io, per), axis=0, keepdims=True)
        rest = jnp.where(io == i1, -jnp.inf, sg)
        m2 = jnp.max(rest, axis=0, keepdims=True)
        i2 = jnp.min(jnp.where(rest == m2, io, per), axis=0, keepdims=True)
        return m1 + m2, i1, i2

    best, e1, e2 = top2(sel[0:per, :])
    for g in range(1, N_GROUPS):
        score, i1, i2 = top2(sel[g * per:(g + 1) * per, :])
        better = score > best
        best = jnp.where(better, score, best)
        e1 = jnp.where(better, i1 + g * per, e1)
        e2 = jnp.where(better, i2 + g * per, e2)
    eo = _iota((E, tm), 0)
    p1 = jnp.sum(jnp.where(eo == e1, probs, 0.0), axis=0, keepdims=True)
    p2 = jnp.sum(jnp.where(eo == e2, probs, 0.0), axis=0, keepdims=True)
    tot = p1 + p2
    gate_ref[...] = jnp.concatenate([p1 / tot, p2 / tot, jnp.zeros((6, tm), F32)], axis=0)

    @pl.when(pl.program_id(0) == 0)
    def _():
        run_ref[...] = jnp.zeros_like(run_ref)

    chosen = jnp.where(eo == e1, 1.0, jnp.where(eo == e2, 1.0, 0.0))
    earlier = (_iota((tm, tm), 0) < _iota((tm, tm), 1)).astype(BF16)
    before = jnp.dot(chosen.astype(BF16), earlier, preferred_element_type=F32) + run_ref[...]
    r1 = jnp.sum(jnp.where(eo == e1, before, 0.0), axis=0, keepdims=True).astype(jnp.int32)
    r2 = jnp.sum(jnp.where(eo == e2, before, 0.0), axis=0, keepdims=True).astype(jnp.int32)
    run_ref[...] = run_ref[...] + jnp.sum(chosen, axis=1, keepdims=True)
    idx_ref[...] = jnp.concatenate([e1, e2, r1, r2, jnp.zeros((4, tm), jnp.int32)], axis=0)
    cnt_ref[...] = jnp.broadcast_to(run_ref[...], cnt_ref.shape).astype(jnp.int32)


def router(x, w_router, b_router):
    T, D = x.shape
    E = w_router.shape[1]
    tm = _tile(T, (1024, 512, 256, 128))
    return pl.pallas_call(
        functools.partial(_router_kernel, E=E),
        grid=(T // tm,),
        in_specs=[pl.BlockSpec((tm, D), lambda i: (i, 0)),
                  pl.BlockSpec((E, D), lambda i: (0, 0)),
                  pl.BlockSpec((E, 1), lambda i: (0, 0))],
        out_specs=[pl.BlockSpec((8, tm), lambda i: (0, i)), pl.BlockSpec((8, tm), lambda i: (0, i)),
                   pl.BlockSpec((E, LANE), lambda i: (0, 0))],
        out_shape=[jax.ShapeDtypeStruct((8, T), jnp.int32), jax.ShapeDtypeStruct((8, T), F32),
                   jax.ShapeDtypeStruct((E, LANE), jnp.int32)],
        scratch_shapes=[pltpu.VMEM((E, 1), F32)],
        compiler_params=_params("arbitrary"),
        name="router",
    )(x, w_router.T, b_router.reshape(E, 1))


ROW_DMA_UNROLL = 8


def _start_row_copies(n, copy_of):
    def body(g, carry):
        for u in range(ROW_DMA_UNROLL):
            copy_of(g * ROW_DMA_UNROLL + u).start(priority=u % 2)
        return carry

    lax.fori_loop(0, n // ROW_DMA_UNROLL, body, 0)


def _wait_row_copies(hbm_ref, vmem_ref, sem, to_hbm):
    rows = hbm_ref.at[pl.ds(0, vmem_ref.shape[0])]
    (pltpu.make_async_copy(vmem_ref, rows, sem) if to_hbm else pltpu.make_async_copy(rows, vmem_ref, sem)).wait()


def _expert_kernel(te_ref, nv_ref, fresh_ref, src_ref, dst_ref, x_hbm, wg_ref, wu_ref, wd_ref, y_hbm,
                   wg_bf, wu_bf, wd_bf, xbuf, xs_bf, acc_ref, ybuf, gsem, ssem, zsem, *, tr, n_pairs):
    i = pl.program_id(0)
    j = pl.program_id(1)
    nv = nv_ref[0]
    valid = i < nv
    last = j == pl.num_programs(1) - 1
    slot = i % 2

    def gather(tile, s):
        _start_row_copies(tr, lambda r: pltpu.make_async_copy(x_hbm.at[src_ref[tile * tr + r]], xbuf.at[s, r],
                                                              gsem.at[s]))

    def scatter(tile, s):
        _start_row_copies(tr, lambda r: pltpu.make_async_copy(ybuf.at[s, r], y_hbm.at[dst_ref[tile * tr + r]],
                                                              ssem.at[s]))

    @pl.when((i == 0) & (j == 0))
    def _():
        ybuf[1] = jnp.zeros(ybuf.shape[1:], F32)
        spare = [pltpu.make_async_copy(ybuf.at[1], y_hbm.at[pl.ds(n_pairs + c * tr, tr)], zsem)
                 for c in range((y_hbm.shape[0] - n_pairs) // tr)]
        for cp in spare:
            cp.start()
        for cp in spare:
            cp.wait()

    @pl.when(valid & (j == 0))
    def _():
        @pl.when(i == 0)
        def _():
            gather(0, 0)

        @pl.when(i + 1 < nv)
        def _():
            gather(i + 1, 1 - slot)

        _wait_row_copies(x_hbm, xbuf.at[slot], gsem.at[slot], to_hbm=False)
        xs_bf[...] = _from_slabs(lambda rows, cols: xbuf[slot, rows, cols, :], tr, xbuf.shape[2]).astype(BF16)

    @pl.when(valid & (fresh_ref[i] == 1))
    def _():
        wg_bf[j] = wg_ref[...].astype(BF16)
        wu_bf[j] = wu_ref[...].astype(BF16)
        wd_bf[j] = wd_ref[...].astype(BF16)

    @pl.when(valid)
    def _():
        x = xs_bf[...]
        hg = jnp.dot(x, wg_bf[j], preferred_element_type=F32)
        hu = jnp.dot(x, wu_bf[j], preferred_element_type=F32)
        h = (_silu(hg) * hu).astype(BF16)
        part = jnp.dot(h, wd_bf[j], preferred_element_type=F32)

        @pl.when(j == 0)
        def _():
            acc_ref[...] = part

        @pl.when(j > 0)
        def _():
            acc_ref[...] = acc_ref[...] + part

    @pl.when(valid & last)
    def _():
        _to_slabs(ybuf.at[slot], acc_ref[...])
        scatter(i, slot)

        @pl.when(i > 0)
        def _():
            _wait_row_copies(y_hbm, ybuf.at[1 - slot], ssem.at[1 - slot], to_hbm=True)

        @pl.when(i == nv - 1)
        def _():
            _wait_row_copies(y_hbm, ybuf.at[slot], ssem.at[slot], to_hbm=True)


def expert_ffn(x_slabs, src_rows, dst_rows, tile_expert, n_valid, fresh, w_gate, w_up, w_down, layer, tr, n_out):
    T, S, _ = x_slabs.shape
    n_pairs = 2 * T
    D = S * LANE
    R = src_rows.shape[0]
    De = w_gate.shape[-1]
    te = _tile(De, (512, 256, 128))
    nj = De // te
    assert tr % ROW_DMA_UNROLL == 0
    chunk = lambda i, j, fr: jnp.where(fr[i] == 1, j, nj - 1)
    grid_spec = pltpu.PrefetchScalarGridSpec(
        num_scalar_prefetch=5,
        grid=(R // tr, nj),
        in_specs=[pl.BlockSpec(memory_space=pl.ANY),
                  pl.BlockSpec((None, None, D, te), lambda i, j, e, nv, fr, s, d: (layer, e[i], 0, chunk(i, j, fr))),
                  pl.BlockSpec((None, None, D, te), lambda i, j, e, nv, fr, s, d: (layer, e[i], 0, chunk(i, j, fr))),
                  pl.BlockSpec((None, None, te, D), lambda i, j, e, nv, fr, s, d: (layer, e[i], chunk(i, j, fr), 0))],
        out_specs=pl.BlockSpec(memory_space=pl.ANY),
        scratch_shapes=[pltpu.VMEM((nj, D, te), BF16), pltpu.VMEM((nj, D, te), BF16), pltpu.VMEM((nj, te, D), BF16),
                        pltpu.VMEM((2, tr, S, LANE), F32), pltpu.VMEM((tr, D), BF16), pltpu.VMEM((tr, D), F32),
                        pltpu.VMEM((2, tr, S, LANE), F32),
                        pltpu.SemaphoreType.DMA((2,)), pltpu.SemaphoreType.DMA((2,)), pltpu.SemaphoreType.DMA(())],
    )
    assert (n_out - n_pairs) % tr == 0
    return pl.pallas_call(
        functools.partial(_expert_kernel, tr=tr, n_pairs=n_pairs),
        grid_spec=grid_spec,
        out_shape=jax.ShapeDtypeStruct((n_out, S, LANE), F32),
        compiler_params=_params("arbitrary", "arbitrary"),
        name="moe_experts",
    )(tile_expert, n_valid, fresh, src_rows, dst_rows, x_slabs, w_gate, w_up, w_down)


def _combine_ln_kernel(y_ref, gate_ref, res_ref, g_ref, b_ref, o1_ref, o2_ref, *obf_ref, alpha, n_first):
    i = pl.program_id(0)
    tm, S2, _ = y_ref.shape
    S = S2 // 2
    shift = lambda cols, by: cols + by if isinstance(cols, int) else slice(cols.start + by, cols.stop + by)
    y0 = _from_slabs(lambda rows, cols: y_ref[rows, cols, :], tm, S)
    y1 = _from_slabs(lambda rows, cols: y_ref[rows, shift(cols, S), :], tm, S)
    moe = gate_ref[:, 0:1] * y0 + gate_ref[:, 1:2] * y1
    z = _layer_norm(alpha * res_ref[...] + moe, g_ref[...], b_ref[...])

    @pl.when(i < n_first)
    def _():
        o1_ref[...] = z

    @pl.when(i >= n_first)
    def _():
        o2_ref[...] = z

    if obf_ref:
        obf_ref[0][...] = z.astype(BF16)


def combine_ln(y_pairs, gates_t, res, g, b, alpha, split_rows, with_bf16):
    T, D = res.shape
    S2 = y_pairs.shape[1]
    tm = _tile(math.gcd(split_rows, T - split_rows), (256, 128))
    row = lambda i: (i, 0)
    n1 = split_rows // tm
    assert 0 < n1 < T // tm
    out_specs = [pl.BlockSpec((tm, D), lambda i: (jnp.minimum(i, n1 - 1), 0)),
                 pl.BlockSpec((tm, D), lambda i: (jnp.maximum(i - n1, 0), 0))]
    out_shape = [jax.ShapeDtypeStruct((split_rows, D), F32), jax.ShapeDtypeStruct((T - split_rows, D), F32)]
    if with_bf16:
        out_specs.append(pl.BlockSpec((tm, D), row))
        out_shape.append(jax.ShapeDtypeStruct((T, D), BF16))
    return pl.pallas_call(
        functools.partial(_combine_ln_kernel, alpha=alpha, n_first=n1),
        grid=(T // tm,),
        in_specs=[pl.BlockSpec((tm, S2, LANE), lambda i: (i, 0, 0)),
                  pl.BlockSpec((tm, 8), row),
                  pl.BlockSpec((tm, D), row),
                  pl.BlockSpec((1, D), lambda i: (0, 0)),
                  pl.BlockSpec((1, D), lambda i: (0, 0))],
        out_specs=out_specs,
        out_shape=out_shape,
        compiler_params=_params("arbitrary"),
        name="moe_combine_ln",
    )(y_pairs, gates_t, res, g.reshape(1, D), b.reshape(1, D))


def _dispatch_plan(e_idx, counts, E, tr):
    T = e_idx.shape[1]
    pairs = 2 * T
    e_flat = e_idx[0:2, :].T.reshape(pairs)
    rank = e_idx[2:4, :].T.reshape(pairs)
    padded = ((counts + tr - 1) // tr) * tr
    pstart = jnp.cumsum(padded) - padded
    pos = (pstart[e_flat] + rank).astype(jnp.int32)
    n_tiles = pairs // tr + E
    R = n_tiles * tr
    pair_of_row = jnp.full((R,), -1, jnp.int32).at[pos].set(jnp.arange(pairs, dtype=jnp.int32), unique_indices=True)
    n_valid = (jnp.sum(padded) // tr).astype(jnp.int32).reshape(1)
    tile_end = (pstart + padded) // tr
    tiles = jnp.arange(n_tiles, dtype=jnp.int32)
    tile_e = jnp.minimum(jnp.sum((tiles[:, None] >= tile_end[None, :]).astype(jnp.int32), axis=1), E - 1)
    spare = pairs + tile_e * tr - pstart[tile_e] - counts[tile_e]
    spare_row = jnp.repeat(spare, tr) + jnp.arange(R, dtype=jnp.int32)
    is_pair = pair_of_row >= 0
    src_rows = jnp.where(is_pair, pair_of_row // 2, 0).astype(jnp.int32)
    dst_rows = jnp.where(is_pair, pair_of_row, spare_row).astype(jnp.int32)
    last_e = tile_e[jnp.maximum(n_valid[0] - 1, 0)]
    tile_e = jnp.where(tiles < n_valid[0], tile_e, last_e).astype(jnp.int32)
    fresh = jnp.concatenate([jnp.ones((1,), jnp.int32), (tile_e[1:] != tile_e[:-1]).astype(jnp.int32)])
    return src_rows, dst_rows, tile_e, n_valid, fresh, R


def moe_ln(x, x_slabs, x_res_scale, w_router, b_router, w_gate, w_up, w_down, layer, g, b, split_rows, with_bf16):
    T, D = x.shape
    S = D // LANE
    E = w_router.shape[1]
    tr = _tile(2 * T, (256, 128))
    e_idx, gates, counts = router(x, w_router, b_router)
    src_rows, dst_rows, tile_e, n_valid, fresh, R = _dispatch_plan(e_idx, counts[:, 0], E, tr)
    ys = expert_ffn(x_slabs, src_rows, dst_rows, tile_e, n_valid, fresh, w_gate, w_up, w_down, layer, tr, R)
    return combine_ln(ys.reshape(R // 2, 2 * S, LANE), gates.T, x, g, b, x_res_scale, split_rows, with_bf16)


def kernel(x_prompt, x_sample, cache_k, cache_v, cache_logf, state_conv, state_hgrn, page_table, w_in_even, b_fgate, w_dw, b_dw, g_cnorm, b_cnorm, w_out_even, w_in_odd, lb_logits, g_onorm, g_vnorm, b_vnorm, w_sgu, b_sgu, w_out_odd, ln1_g, ln1_b, ln2_g, ln2_b, w_router, b_router, w_gate, w_up, w_down):
    Bp, Lp, D = x_prompt.shape
    Bs, Ls, _ = x_sample.shape
    Dh = D // 2
    H = b_fgate.shape[1]
    HG = state_hgrn.shape[2]
    assert Dh == H * LANE and Dh == HG * LANE and Dh == w_sgu.shape[1] * LANE
    depth = ln1_g.shape[0]
    alpha = (2 * depth) ** 0.25
    Tp, Ts = Bp * Lp, Bs * Ls
    n_even = cache_k.shape[0]
    n_phys, page = cache_k.shape[1], cache_k.shape[2]

    lb_p = jax.nn.softmax(lb_logits.astype(F32), axis=0)
    lb_all = jnp.cumsum(lb_p, axis=0) - lb_p[0]

    res = (x_prompt.reshape(Tp, D), x_sample.reshape(Ts, D))
    x_bf = jnp.concatenate([r.astype(BF16) for r in res], axis=0)
    rin, tot = page_suffix(cache_logf.astype(F32).reshape(n_even * n_phys, page * H), H)
    rin = rin.reshape(n_even * n_phys, 1, page * H)
    tot = tot.reshape(n_even * n_phys, 1, page * H)

    out = {k: [] for k in ("kp", "vp", "lfp", "convp", "hgp", "ks", "vs", "lfs", "convs", "hgs", "mlpv")}
    for l in range(depth):
        j = l // 2
        if l % 2 == 0:
            w_in = w_in_even[j]
            ag = matmul_cols(x_bf, w_in_even, j, 0, 2 * Dh)
            q = matmul_cols(x_bf, w_in_even, j, 2 * Dh, Dh)
            k_p, k_s = matmul_cols(x_bf, w_in_even, j, 3 * Dh, Dh, split_rows=Tp)
            v_p, v_s = matmul_cols(x_bf, w_in_even, j, 4 * Dh, Dh, split_rows=Tp)
            lf, _, c, ct = logf_project(x_bf, w_in[:, 5 * Dh:], b_fgate[j], Lp)
            conv_args = (w_dw[j], b_dw[j], g_cnorm[j], b_cnorm[j])
            ca_p, cst_p = conv_prompt(ag, Bp, Lp, *conv_args)
            ca_s, cst_s = conv_sample(ag, Tp, Bs, Ls, state_conv[j], *conv_args)
            att_p = fox_prompt(q, k_p, v_p, c, ct, Bp, Lp, H)
            att_s = fox_sample(q, k_s, v_s, lf, Tp, Bs, Ls, cache_k, cache_v, rin, tot, j, page_table, H)
            h1 = (ca_p, ca_s)
            h2 = (att_p, att_s.astype(BF16))
            w_out = w_out_even[j]
            out["kp"].append(k_p.reshape(Bp, Lp, H, LANE))
            out["vp"].append(v_p.reshape(Bp, Lp, H, LANE))
            out["lfp"].append(lf[:Tp].reshape(Bp, Lp, H))
            out["convp"].append(cst_p)
            out["ks"].append(k_s.reshape(Bs, Ls, H, LANE))
            out["vs"].append(v_s.reshape(Bs, Ls, H, LANE))
            out["lfs"].append(lf[Tp:].reshape(Bs, Ls, H))
            out["convs"].append(cst_s)
        else:
            proj = matmul_cols(x_bf, w_in_odd, j, 0, 6 * Dh)
            o_p, s_p = hgrn(proj, 0, Bp, Lp, lb_all[l], g_onorm[j], None, HG)
            o_s, s_s = hgrn(proj, Tp, Bs, Ls, lb_all[l], g_onorm[j], state_hgrn[j].astype(F32), HG)
            mlp_args = (w_sgu[j], b_sgu[j], g_vnorm[j], b_vnorm[j])
            d_p, _ = gmlp(proj, 0, Tp, Lp, *mlp_args)
            d_s, v_s = gmlp(proj, Tp, Ts, Ls, *mlp_args)
            h1 = (o_p, o_s)
            h2 = (d_p, d_s)
            w_out = w_out_odd[j]
            out["hgp"].append(s_p)
            out["hgs"].append(s_s)
            out["mlpv"].append(v_s.reshape(Bs, Ls, Dh))
        x, x_slabs = outproj_ln(h1, h2, w_out.astype(BF16), res, ln1_g[l], ln1_b[l], alpha)
        more = l + 1 < depth
        res_p, res_s, *x_next = moe_ln(x, x_slabs, alpha, w_router, b_router, w_gate, w_up, w_down, l,
                                       ln2_g[l], ln2_b[l], Tp, more)
        res = (res_p, res_s)
        if more:
            x_bf = x_next[0]

    stack = lambda name: jnp.stack(out[name])
    return (res[0].reshape(Bp, Lp, D), res[1].reshape(Bs, Ls, D),
            stack("kp"), stack("vp"), stack("lfp"), stack("convp"), stack("hgp"),
            stack("ks"), stack("vs"), stack("lfs"), stack("convs"), stack("hgs"), stack("mlpv"))
```
